```python
import math
import jax
import jax.numpy as jnp
from jax import lax
import numpy as np

D_MODEL = 2048
BATCH = 8
SEQ = 2048
DEPTH = 2

FFN_HIDDEN = ((8 * D_MODEL + 3 * 256 - 1) // (3 * 256)) * 256

SC_WIDTH = D_MODEL // 2
SC_KERNEL = 3
DN_HEADS = 8
DN_HEAD_DIM = 128
DN_WIDTH = DN_HEADS * DN_HEAD_DIM
DN_CONV = 4
DN_CHUNK = 64
HY_WIDTHS = [SC_WIDTH, SC_WIDTH, SC_WIDTH, 3 * DN_WIDTH, DN_WIDTH, DN_HEADS, DN_HEADS]
HY_COLS = sum(HY_WIDTHS)

NSA_HEADS = 16
NSA_KV_HEADS = 4
NSA_HEAD_DIM = 128
NSA_KV_WIDTH = NSA_KV_HEADS * NSA_HEAD_DIM
CMP_BLOCK = 32
CMP_STRIDE = 16
CMP_HIDDEN = 2 * NSA_HEAD_DIM
SEL_BLOCK = 64
N_SELECT = 16
WINDOW = 512
SEL_Q_BLOCK = 16
WIN_Q_BLOCK = 128
NSA_WIDTHS = [NSA_HEADS * NSA_HEAD_DIM] + [NSA_KV_WIDTH] * 6 + [3 * NSA_HEADS]
NSA_COLS = sum(NSA_WIDTHS)
ATTN_SCALE = NSA_HEAD_DIM ** -0.5

ROPE_THETA = 10000.0
LN_EPS = 1e-5
NORM_EPS = 1e-6
NEG_INF = -1e30
DEEPNORM_ALPHA = (2 * DEPTH) ** 0.25
DEEPNORM_BETA = (8 * DEPTH) ** -0.25
N_EVEN = (DEPTH + 1) // 2
N_ODD = DEPTH // 2

kernel_name = 'hybrid_shortconv_deltanet_nsa_deepnorm'


def _split_cols(t, widths):
    cuts = [int(c) for c in np.cumsum(widths)[:-1]]
    return jnp.split(t, cuts, axis=-1)


def layer_norm(x, g, b):
    xf = x.astype(jnp.float32)
    mu = jnp.mean(xf, axis=-1, keepdims=True)
    var = jnp.mean(jnp.square(xf - mu), axis=-1, keepdims=True)
    return ((xf - mu) * lax.rsqrt(var + LN_EPS) * g + b).astype(x.dtype)


def causal_depthwise_conv(x, w):
    k, s = w.shape[0], x.shape[1]
    xp = jnp.pad(x, ((0, 0), (k - 1, 0), (0, 0)))
    out = xp[:, 0:s] * w[0]
    for j in range(1, k):
        out = out + xp[:, j:j + s] * w[j]
    return out


def l2norm(t):
    return t * lax.rsqrt(jnp.sum(jnp.square(t), axis=-1, keepdims=True) + NORM_EPS)


def rope_angles(pos):
    half = NSA_HEAD_DIM // 2
    inv = jnp.power(ROPE_THETA, -jnp.arange(half, dtype=jnp.float32) / half)
    ang = pos.astype(jnp.float32)[..., None] * inv
    return jnp.cos(ang), jnp.sin(ang)


def apply_rope(x, cos, sin):
    x1, x2 = jnp.split(x.astype(jnp.float32), 2, axis=-1)
    return jnp.concatenate([x1 * cos - x2 * sin, x2 * cos + x1 * sin], axis=-1).astype(x.dtype)


def gated_delta_rule(q, k, v, g, beta):
    b, h, s, dk = q.shape
    dv = v.shape[-1]
    c = DN_CHUNK
    n = s // c
    q = q * dk ** -0.5
    qc = q.reshape(b, h, n, c, dk)
    kc = k.reshape(b, h, n, c, dk)
    vc = v.reshape(b, h, n, c, dv)
    bc = beta.reshape(b, h, n, c, 1)
    gc = jnp.cumsum(g.reshape(b, h, n, c), axis=-1)
    incl = jnp.tril(jnp.ones((c, c), dtype=bool))
    strict = jnp.tril(jnp.ones((c, c), dtype=bool), -1)
    decay = jnp.where(incl, jnp.exp(jnp.where(incl, gc[..., :, None] - gc[..., None, :], 0.0)), 0.0)
    kbeta = kc * bc
    a_mat = jnp.where(strict, jnp.einsum('bhnid,bhnjd->bhnij', kbeta, kc) * decay, 0.0)
    rhs = jnp.concatenate([vc * bc, kbeta * jnp.exp(gc)[..., None]], axis=-1)
    sol = lax.linalg.triangular_solve(jnp.eye(c, dtype=a_mat.dtype) + a_mat, rhs,
                                      left_side=True, lower=True, unit_diagonal=True)
    u, w = sol[..., :dv], sol[..., dv:]
    intra = jnp.einsum('bhnid,bhnjd->bhnij', qc, kc) * decay
    q_dec = qc * jnp.exp(gc)[..., None]
    k_dec = kc * jnp.exp(gc[..., -1:] - gc)[..., None]
    g_last = jnp.exp(gc[..., -1])

    def step(state, inp):
        q_i, k_i, u_i, w_i, a_i, gl = inp
        v_new = u_i - jnp.einsum('bhcd,bhde->bhce', w_i, state)
        o = jnp.einsum('bhcd,bhde->bhce', q_i, state) + jnp.einsum('bhij,bhje->bhie', a_i, v_new)
        state = state * gl[..., None, None] + jnp.einsum('bhcd,bhce->bhde', k_i, v_new)
        return state, o

    xs = (jnp.moveaxis(q_dec, 2, 0), jnp.moveaxis(k_dec, 2, 0), jnp.moveaxis(u, 2, 0),
          jnp.moveaxis(w, 2, 0), jnp.moveaxis(intra, 2, 0), jnp.moveaxis(g_last, 2, 0))
    _, o = lax.scan(step, jnp.zeros((b, h, dk, dv), q.dtype), xs)
    return jnp.moveaxis(o, 0, 2).reshape(b, h, s, dv)


def conv_deltanet_mixer(x, w_in, sc_conv_w, dn_conv_w, a_log, dt_bias, norm_w, w_out):
    b, s, _ = x.shape
    f32 = jnp.float32
    sc_b, sc_c, sc_h, dn_qkv, dn_z, dn_b, dn_a = _split_cols(x @ w_in, HY_WIDTHS)
    y_sc = sc_b * causal_depthwise_conv(sc_c * sc_h, sc_conv_w)
    qkv = jax.nn.silu(causal_depthwise_conv(dn_qkv, dn_conv_w))
    q, k, v = jnp.split(qkv, 3, axis=-1)

    def heads(t):
        return t.reshape(b, s, DN_HEADS, DN_HEAD_DIM).transpose(0, 2, 1, 3).astype(f32)

    q, k, v, z = l2norm(heads(q)), l2norm(heads(k)), heads(v), heads(dn_z)
    beta = jax.nn.sigmoid(dn_b.astype(f32)).transpose(0, 2, 1)
    g = (-jnp.exp(a_log.astype(f32)) * jax.nn.softplus(dn_a.astype(f32) + dt_bias.astype(f32))).transpose(0, 2, 1)
    o = gated_delta_rule(q, k, v, g, beta)
    o = o * lax.rsqrt(jnp.mean(jnp.square(o), axis=-1, keepdims=True) + NORM_EPS) * norm_w * jax.nn.silu(z)
    y_dn = o.transpose(0, 2, 1, 3).reshape(b, s, DN_WIDTH).astype(x.dtype)
    return jnp.concatenate([y_sc, y_dn], axis=-1) @ w_out


def compress_blocks(kt, pos_emb, w1, w2):
    b, s, g_, dh = kt.shape
    r = CMP_BLOCK // CMP_STRIDE
    n_sub = s // CMP_STRIDE
    n_cmp = n_sub - r + 1
    sub = kt.reshape(b, n_sub, CMP_STRIDE, g_, dh)
    blocks = jnp.concatenate([sub[:, i:i + n_cmp] for i in range(r)], axis=2)
    blocks = blocks + pos_emb[:, None, :]
    blocks = blocks.transpose(0, 3, 1, 2, 4).reshape(b, g_, n_cmp, CMP_BLOCK * dh)
    return jax.nn.gelu(blocks @ w1) @ w2


def select_blocks(p_grp, s):
    rs = SEL_BLOCK // CMP_STRIDE
    rc = CMP_BLOCK // CMP_STRIDE
    n_sel = s // SEL_BLOCK
    n_cmp = p_grp.shape[-1]
    length = rs * n_sel + rc - 1
    pp = jnp.pad(p_grp, ((0, 0), (0, 0), (0, 0), (rc - 1, length - (rc - 1) - n_cmp)))
    score = None
    for m in range(rs):
        for n in range(rc):
            term = pp[..., m + n:m + n + rs * n_sel:rs]
            score = term if score is None else score + term
    t = jnp.arange(s)[:, None]
    j = jnp.arange(n_sel)[None, :]
    cur = t // SEL_BLOCK
    forced = (j == 0) | (j == cur) | (j == cur - 1)
    future = j * SEL_BLOCK > t
    score = jnp.where(forced, jnp.inf, jnp.where(future, -jnp.inf, score))
    _, idx = lax.top_k(score, min(N_SELECT, n_sel))
    return idx


def selected_attention(q, k, v, idx):
    b, g_, hpg, s, dh = q.shape
    ksel = idx.shape[-1]
    n_sel = s // SEL_BLOCK
    nq = s // SEL_Q_BLOCK
    k_blk = k.reshape(b, g_, n_sel, SEL_BLOCK, dh)
    v_blk = v.reshape(b, g_, n_sel, SEL_BLOCK, dh)
    bi = jnp.arange(b)[:, None, None, None]
    gi = jnp.arange(g_)[None, :, None, None]
    offs = jnp.arange(SEL_BLOCK)
    q_chunks = jnp.moveaxis(q.reshape(b, g_, hpg, nq, SEL_Q_BLOCK, dh), 3, 0)
    i_chunks = jnp.moveaxis(idx.reshape(b, g_, nq, SEL_Q_BLOCK, ksel), 2, 0)

    def one_chunk(args):
        qb, ib, c = args
        kg = k_blk[bi, gi, ib]
        vg = v_blk[bi, gi, ib]
        kpos = ib[..., None] * SEL_BLOCK + offs
        tq = c * SEL_Q_BLOCK + jnp.arange(SEL_Q_BLOCK)
        valid = (kpos <= tq[:, None, None])[:, :, None]
        sc = jnp.einsum('bghqd,bgqkld->bghqkl', qb, kg).astype(jnp.float32) * ATTN_SCALE
        p = jax.nn.softmax(jnp.where(valid, sc, NEG_INF), axis=(-2, -1))
        return jnp.einsum('bghqkl,bgqkld->bghqd', p.astype(vg.dtype), vg)

    o = lax.map(one_chunk, (q_chunks, i_chunks, jnp.arange(nq)))
    return jnp.moveaxis(o, 0, 3).reshape(b, g_, hpg, s, dh)


def window_attention(q, k, v):
    b, g_, hpg, s, dh = q.shape
    nq = s // WIN_Q_BLOCK
    span = WIN_Q_BLOCK + WINDOW
    kp = jnp.pad(k, ((0, 0), (0, 0), (WINDOW, 0), (0, 0)))
    vp = jnp.pad(v, ((0, 0), (0, 0), (WINDOW, 0), (0, 0)))
    q_chunks = jnp.moveaxis(q.reshape(b, g_, hpg, nq, WIN_Q_BLOCK, dh), 3, 0)

    def one_chunk(args):
        qb, c = args
        start = c * WIN_Q_BLOCK
        kb = lax.dynamic_slice_in_dim(kp, start, span, axis=2)
        vb = lax.dynamic_slice_in_dim(vp, start, span, axis=2)
        kpos = start - WINDOW + jnp.arange(span)
        tq = start + jnp.arange(WIN_Q_BLOCK)
        valid = (kpos[None, :] <= tq[:, None]) & (kpos[None, :] > tq[:, None] - WINDOW) & (kpos[None, :] >= 0)
        sc = jnp.einsum('bghqd,bgkd->bghqk', qb, kb).astype(jnp.float32) * ATTN_SCALE
        p = jax.nn.softmax(jnp.where(valid, sc, NEG_INF), axis=-1)
        return jnp.einsum('bghqk,bgkd->bghqd', p.astype(vb.dtype), vb)

    o = lax.map(one_chunk, (q_chunks, jnp.arange(nq)))
    return jnp.moveaxis(o, 0, 3).reshape(b, g_, hpg, s, dh)


def nsa_mixer(x, positions, w_in, cmp_pos_k, cmp_w1_k, cmp_w2_k, cmp_pos_v, cmp_w1_v, cmp_w2_v, w_out):
    b, s, _ = x.shape
    hq, g_, dh = NSA_HEADS, NSA_KV_HEADS, NSA_HEAD_DIM
    hpg = hq // g_
    q, k_c, v_c, k_s, v_s, k_w, v_w, gates = _split_cols(x @ w_in, NSA_WIDTHS)
    cos, sin = rope_angles(positions)
    cos, sin = cos[:, :, None], sin[:, :, None]
    q = apply_rope(q.reshape(b, s, hq, dh), cos, sin)
    q = q.reshape(b, s, g_, hpg, dh).transpose(0, 2, 3, 1, 4)

    def kv_heads(t):
        return t.reshape(b, s, g_, dh)

    k_s = apply_rope(kv_heads(k_s), cos, sin).transpose(0, 2, 1, 3)
    v_s = kv_heads(v_s).transpose(0, 2, 1, 3)
    k_w = apply_rope(kv_heads(k_w), cos, sin).transpose(0, 2, 1, 3)
    v_w = kv_heads(v_w).transpose(0, 2, 1, 3)
    k_cmp = compress_blocks(kv_heads(k_c), cmp_pos_k, cmp_w1_k, cmp_w2_k)
    v_cmp = compress_blocks(kv_heads(v_c), cmp_pos_v, cmp_w1_v, cmp_w2_v)
    n_cmp = k_cmp.shape[2]
    cmp_end = jnp.arange(n_cmp) * CMP_STRIDE + CMP_BLOCK - 1
    ccos, csin = rope_angles(positions[:, cmp_end])
    k_cmp = apply_rope(k_cmp, ccos[:, None], csin[:, None])
    valid_c = cmp_end[None, :] <= jnp.arange(s)[:, None]
    sc = jnp.einsum('bghsd,bgnd->bghsn', q, k_cmp).astype(jnp.float32) * ATTN_SCALE
    p_cmp = jax.nn.softmax(jnp.where(valid_c, sc, NEG_INF), axis=-1) * valid_c
    o_cmp = jnp.einsum('bghsn,bgnd->bghsd', p_cmp.astype(v_cmp.dtype), v_cmp)
    idx = select_blocks(jnp.sum(p_cmp, axis=2), s)
    o_slc = selected_attention(q, k_s, v_s, idx)
    o_win = window_attention(q, k_w, v_w)
    gt = jax.nn.sigmoid(gates.astype(jnp.float32)).reshape(b, s, g_, hpg, 3)
    gt = gt.transpose(0, 2, 3, 1, 4).astype(x.dtype)
    o = gt[..., 0:1] * o_cmp + gt[..., 1:2] * o_slc + gt[..., 2:3] * o_win
    o = o.transpose(0, 3, 1, 2, 4).reshape(b, s, hq * dh)
    return o @ w_out


def swiglu_ffn(x, w_in, w_out):
    gate, up = jnp.split(x @ w_in, 2, axis=-1)
    return (jax.nn.silu(gate) * up) @ w_out


def setup_inputs(seed: int = 0) -> dict:
    key = jax.random.key(seed)
    ks = jax.random.split(key, 24)
    f32 = jnp.float32

    def nrm(k, shape, fan_in, gain=1.0):
        return jax.random.normal(k, shape, f32) * (gain * fan_in ** -0.5)

    x = jax.random.normal(ks[0], (BATCH, SEQ, D_MODEL), f32)
    offset = jax.random.randint(ks[1], (BATCH, 1), 0, 4096, dtype=jnp.int32)
    positions = offset + jnp.arange(SEQ, dtype=jnp.int32)[None, :]
    ln_mix_g = 1.0 + 0.02 * jax.random.normal(ks[2], (DEPTH, D_MODEL), f32)
    ln_mix_b = 0.02 * jax.random.normal(ks[3], (DEPTH, D_MODEL), f32)
    ln_ffn_g = 1.0 + 0.02 * jax.random.normal(ks[4], (DEPTH, D_MODEL), f32)
    ln_ffn_b = 0.02 * jax.random.normal(ks[5], (DEPTH, D_MODEL), f32)
    ffn_w_in = nrm(ks[6], (DEPTH, D_MODEL, 2 * FFN_HIDDEN), D_MODEL)
    ffn_w_out = nrm(ks[7], (DEPTH, FFN_HIDDEN, D_MODEL), FFN_HIDDEN, DEEPNORM_BETA)
    hy_w_in = nrm(ks[8], (N_EVEN, D_MODEL, HY_COLS), D_MODEL)
    sc_conv_w = nrm(ks[9], (N_EVEN, SC_KERNEL, SC_WIDTH), SC_KERNEL)
    dn_conv_w = nrm(ks[10], (N_EVEN, DN_CONV, 3 * DN_WIDTH), DN_CONV)
    dn_a_log = jnp.log(jax.random.uniform(ks[11], (N_EVEN, DN_HEADS), f32, 1.0, 16.0))
    dt = jnp.exp(jax.random.uniform(ks[12], (N_EVEN, DN_HEADS), f32, math.log(1e-3), math.log(1e-1)))
    dn_dt_bias = dt + jnp.log(-jnp.expm1(-dt))
    dn_norm_w = 1.0 + 0.02 * jax.random.normal(ks[13], (N_EVEN, DN_HEAD_DIM), f32)
    hy_w_out = nrm(ks[14], (N_EVEN, D_MODEL, D_MODEL), D_MODEL, DEEPNORM_BETA)
    nsa_w_in = nrm(ks[15], (N_ODD, D_MODEL, NSA_COLS), D_MODEL)
    cmp_pos_k = 0.1 * jax.random.normal(ks[16], (N_ODD, CMP_BLOCK, NSA_HEAD_DIM), f32)
    cmp_w1_k = nrm(ks[17], (N_ODD, CMP_BLOCK * NSA_HEAD_DIM, CMP_HIDDEN), CMP_BLOCK * NSA_HEAD_DIM)
    cmp_w2_k = nrm(ks[18], (N_ODD, CMP_HIDDEN, NSA_HEAD_DIM), CMP_HIDDEN)
    cmp_pos_v = 0.1 * jax.random.normal(ks[19], (N_ODD, CMP_BLOCK, NSA_HEAD_DIM), f32)
    cmp_w1_v = nrm(ks[20], (N_ODD, CMP_BLOCK * NSA_HEAD_DIM, CMP_HIDDEN), CMP_BLOCK * NSA_HEAD_DIM)
    cmp_w2_v = nrm(ks[21], (N_ODD, CMP_HIDDEN, NSA_HEAD_DIM), CMP_HIDDEN)
    nsa_w_out = nrm(ks[22], (N_ODD, D_MODEL, D_MODEL), D_MODEL, DEEPNORM_BETA)
    return {'x': x, 'positions': positions,
            'ln_mix_g': ln_mix_g, 'ln_mix_b': ln_mix_b, 'ln_ffn_g': ln_ffn_g, 'ln_ffn_b': ln_ffn_b,
            'ffn_w_in': ffn_w_in, 'ffn_w_out': ffn_w_out,
            'hy_w_in': hy_w_in, 'sc_conv_w': sc_conv_w, 'dn_conv_w': dn_conv_w,
            'dn_a_log': dn_a_log, 'dn_dt_bias': dn_dt_bias, 'dn_norm_w': dn_norm_w, 'hy_w_out': hy_w_out,
            'nsa_w_in': nsa_w_in, 'cmp_pos_k': cmp_pos_k, 'cmp_w1_k': cmp_w1_k, 'cmp_w2_k': cmp_w2_k,
            'cmp_pos_v': cmp_pos_v, 'cmp_w1_v': cmp_w1_v, 'cmp_w2_v': cmp_w2_v, 'nsa_w_out': nsa_w_out}


def reference(x, positions, ln_mix_g, ln_mix_b, ln_ffn_g, ln_ffn_b, ffn_w_in, ffn_w_out,
              hy_w_in, sc_conv_w, dn_conv_w, dn_a_log, dn_dt_bias, dn_norm_w, hy_w_out,
              nsa_w_in, cmp_pos_k, cmp_w1_k, cmp_w2_k, cmp_pos_v, cmp_w1_v, cmp_w2_v, nsa_w_out):
    for i in range(DEPTH):
        j = i // 2
        if i % 2 == 0:
            y = conv_deltanet_mixer(x, hy_w_in[j], sc_conv_w[j], dn_conv_w[j], dn_a_log[j],
                                    dn_dt_bias[j], dn_norm_w[j], hy_w_out[j])
        else:
            y = nsa_mixer(x, positions, nsa_w_in[j], cmp_pos_k[j], cmp_w1_k[j], cmp_w2_k[j],
                          cmp_pos_v[j], cmp_w1_v[j], cmp_w2_v[j], nsa_w_out[j])
        x = layer_norm(DEEPNORM_ALPHA * x + y, ln_mix_g[i], ln_mix_b[i])
        x = layer_norm(DEEPNORM_ALPHA * x + swiglu_ffn(x, ffn_w_in[i], ffn_w_out[i]), ln_ffn_g[i], ln_ffn_b[i])
    return x
```

```python
import functools
import math

import jax
import jax.numpy as jnp
from jax import lax
from jax.experimental import pallas as pl
from jax.experimental.pallas import tpu as pltpu

F32 = jnp.float32
BF16 = jnp.bfloat16
HIGHEST = lax.Precision.HIGHEST

LANES = 128
VMEM_LIMIT = 48 * 1024 * 1024

DN_HEADS = 8
DN_CHUNK = 64
DN_CONV = 4
SC_KERNEL = 3
NSA_HEADS = 16
NSA_KV_HEADS = 4
HPG = NSA_HEADS // NSA_KV_HEADS
HEAD_DIM = 128
CMP_BLOCK = 32
CMP_STRIDE = 16
SEL_BLOCK = 64
N_SELECT = 16
WINDOW = 512
ROPE_THETA = 10000.0
LN_EPS = 1e-5
NORM_EPS = 1e-6
NEG_INF = -1e30
DEPTH = 2
ALPHA = (2 * DEPTH) ** 0.25
ATTN_SCALE = HEAD_DIM ** -0.5


def _params(*sem):
    return pltpu.CompilerParams(dimension_semantics=sem, vmem_limit_bytes=VMEM_LIMIT)


def _sigmoid(x):
    return 1.0 / (1.0 + jnp.exp(-x))


def _silu(x):
    return x * _sigmoid(x)


def _dot(a, b):
    return jnp.dot(a, b, preferred_element_type=F32)


def _dot_nt(a, b):
    return lax.dot_general(a, b, (((1,), (1,)), ((), ())), preferred_element_type=F32)


def _dot_tn(a, b):
    return lax.dot_general(a, b, (((0,), (0,)), ((), ())), preferred_element_type=F32)


def _dot_hi(a, b):
    return jnp.dot(a, b, precision=HIGHEST, preferred_element_type=F32)


def _mm_kernel(x_ref, w_ref, o_ref):
    o_ref[...] = _dot(x_ref[...].astype(BF16), w_ref[...]).astype(o_ref.dtype)


def _matmul(x, w, *, tm, tn, out_dtype):
    m, k = x.shape
    n = w.shape[1]
    return pl.pallas_call(
        _mm_kernel,
        grid=(m // tm, n // tn),
        in_specs=[pl.BlockSpec((tm, k), lambda i, j: (i, 0)),
                  pl.BlockSpec((k, tn), lambda i, j: (0, j))],
        out_specs=pl.BlockSpec((tm, tn), lambda i, j: (i, j)),
        out_shape=jax.ShapeDtypeStruct((m, n), out_dtype),
        compiler_params=_params("parallel", "parallel"),
        name="matmul",
    )(x, w)


def _mm2_kernel(x1_ref, x2_ref, w1_ref, w2_ref, o_ref):
    acc = _dot(x1_ref[...], w1_ref[...]) + _dot(x2_ref[...], w2_ref[...])
    o_ref[...] = acc.astype(o_ref.dtype)


def _matmul2(x1, x2, w1, w2, *, tm, tn):
    m, k1 = x1.shape
    k2 = x2.shape[1]
    n = w1.shape[1]
    return pl.pallas_call(
        _mm2_kernel,
        grid=(m // tm, n // tn),
        in_specs=[pl.BlockSpec((tm, k1), lambda i, j: (i, 0)),
                  pl.BlockSpec((tm, k2), lambda i, j: (i, 0)),
                  pl.BlockSpec((k1, tn), lambda i, j: (0, j)),
                  pl.BlockSpec((k2, tn), lambda i, j: (0, j))],
        out_specs=pl.BlockSpec((tm, tn), lambda i, j: (i, j)),
        out_shape=jax.ShapeDtypeStruct((m, n), F32),
        compiler_params=_params("parallel", "parallel"),
        name="matmul2",
    )(x1, x2, w1, w2)


def _ffn_in_kernel(x_ref, wg_ref, wu_ref, o_ref):
    x = x_ref[...]
    gate = _dot(x, wg_ref[...])
    up = _dot(x, wu_ref[...])
    o_ref[...] = (_silu(gate) * up).astype(o_ref.dtype)


def _ffn_in(xb, w_in, *, tm, tn):
    m, k = xb.shape
    hidden = w_in.shape[1] // 2
    nj = hidden // tn
    return pl.pallas_call(
        _ffn_in_kernel,
        grid=(m // tm, nj),
        in_specs=[pl.BlockSpec((tm, k), lambda i, j: (i, 0)),
                  pl.BlockSpec((k, tn), lambda i, j: (0, j)),
                  pl.BlockSpec((k, tn), lambda i, j: (0, j + nj))],
        out_specs=pl.BlockSpec((tm, tn), lambda i, j: (i, j)),
        out_shape=jax.ShapeDtypeStruct((m, hidden), BF16),
        compiler_params=_params("parallel", "parallel"),
        name="ffn_in",
    )(xb, w_in, w_in)


def _add_ln_kernel(x_ref, y_ref, g_ref, b_ref, o_ref, ob_ref):
    v = ALPHA * x_ref[...] + y_ref[...]
    mu = jnp.mean(v, axis=-1, keepdims=True)
    d = v - mu
    var = jnp.mean(d * d, axis=-1, keepdims=True)
    out = d * lax.rsqrt(var + LN_EPS) * g_ref[...] + b_ref[...]
    o_ref[...] = out
    ob_ref[...] = out.astype(BF16)


def _add_ln(x, y, g, b, *, tm):
    m, d = x.shape
    return pl.pallas_call(
        _add_ln_kernel,
        grid=(m // tm,),
        in_specs=[pl.BlockSpec((tm, d), lambda i: (i, 0)),
                  pl.BlockSpec((tm, d), lambda i: (i, 0)),
                  pl.BlockSpec((1, d), lambda i: (0, 0)),
                  pl.BlockSpec((1, d), lambda i: (0, 0))],
        out_specs=[pl.BlockSpec((tm, d), lambda i: (i, 0)),
                   pl.BlockSpec((tm, d), lambda i: (i, 0))],
        out_shape=[jax.ShapeDtypeStruct((m, d), F32), jax.ShapeDtypeStruct((m, d), BF16)],
        compiler_params=_params("parallel"),
        name="add_ln",
    )(x, y, g.reshape(1, d), b.reshape(1, d))


def _causal_conv(u, w_ref, taps):
    row = lax.broadcasted_iota(jnp.int32, u.shape, 0)
    acc = u * w_ref[taps - 1:taps, :]
    for sh in range(1, taps):
        shifted = jnp.where(row >= sh, pltpu.roll(u, sh, axis=0), 0.0)
        acc = acc + shifted * w_ref[taps - 1 - sh:taps - sh, :]
    return acc


def _sc_kernel(b_ref, c_ref, h_ref, w_ref, o_ref):
    u = c_ref[0] * h_ref[0]
    o_ref[0] = (b_ref[0] * _causal_conv(u, w_ref, SC_KERNEL)).astype(o_ref.dtype)


def _short_conv(proj, conv_w, width, *, tc):
    bsz, s, _ = proj.shape
    nb = width // tc
    w = jnp.zeros((8, width), F32).at[:SC_KERNEL].set(conv_w)
    return pl.pallas_call(
        _sc_kernel,
        grid=(bsz, nb),
        in_specs=[pl.BlockSpec((1, s, tc), lambda b, j: (b, 0, j)),
                  pl.BlockSpec((1, s, tc), lambda b, j: (b, 0, j + nb)),
                  pl.BlockSpec((1, s, tc), lambda b, j: (b, 0, j + 2 * nb)),
                  pl.BlockSpec((8, tc), lambda b, j: (0, j))],
        out_specs=pl.BlockSpec((1, s, tc), lambda b, j: (b, 0, j)),
        out_shape=jax.ShapeDtypeStruct((bsz, s, width), BF16),
        compiler_params=_params("parallel", "parallel"),
        name="short_conv",
    )(proj, proj, proj, w)


def _dn_conv_kernel(x_ref, w_ref, o_ref, *, n_heads):
    j = pl.program_id(1)
    y = _silu(_causal_conv(x_ref[0], w_ref, DN_CONV))
    inv = lax.rsqrt(jnp.sum(y * y, axis=-1, keepdims=True) + NORM_EPS)
    scale = jnp.where(j < n_heads, inv * (HEAD_DIM ** -0.5), jnp.where(j < 2 * n_heads, inv, 1.0))
    o_ref[0] = y * scale


def _dn_conv(proj, conv_w, col0, n_heads):
    bsz, s, _ = proj.shape
    width = 3 * n_heads * HEAD_DIM
    blk0 = col0 // HEAD_DIM
    w = jnp.zeros((8, width), F32).at[:DN_CONV].set(conv_w)
    return pl.pallas_call(
        functools.partial(_dn_conv_kernel, n_heads=n_heads),
        grid=(bsz, 3 * n_heads),
        in_specs=[pl.BlockSpec((1, s, HEAD_DIM), lambda b, j: (b, 0, j + blk0)),
                  pl.BlockSpec((8, HEAD_DIM), lambda b, j: (0, j))],
        out_specs=pl.BlockSpec((1, s, HEAD_DIM), lambda b, j: (b, 0, j)),
        out_shape=jax.ShapeDtypeStruct((bsz, s, width), F32),
        compiler_params=_params("parallel", "parallel"),
        name="dn_conv",
    )(proj, w)


def _unit_lower_inverse(a, row, col):
    c = a.shape[0]
    eye = (row == col).astype(F32)
    ad = jnp.where((row >> 3) == (col >> 3), a, 0.0)
    p = eye - ad
    a2 = _dot_hi(ad, ad)
    p = p + _dot_hi(p, a2)
    a4 = _dot_hi(a2, a2)
    p = p + _dot_hi(p, a4)
    shift = 4
    while (1 << shift) <= c:
        same_big = (row >> shift) == (col >> shift)
        same_small = (row >> (shift - 1)) == (col >> (shift - 1))
        off = jnp.where(same_big, jnp.where(same_small, 0.0, a), 0.0)
        p = p - _dot_hi(_dot_hi(p, off), p)
        shift += 1
    return p


def _dn_kernel(q_ref, k_ref, v_ref, z_ref, ba_ref, arow_ref, dtrow_ref, nw_ref, o_ref, beta_s, g_s, *, n_heads):
    h = pl.program_id(1)
    s = q_ref.shape[1]
    c = DN_CHUNK
    ba = ba_ref[0]
    lane = lax.broadcasted_iota(jnp.int32, ba.shape, 1)
    beta_full = _sigmoid(ba)
    xa = ba + dtrow_ref[...]
    softplus = jnp.maximum(xa, 0.0) + jnp.log(1.0 + jnp.exp(-jnp.abs(xa)))
    g_full = -jnp.exp(arow_ref[...]) * softplus
    beta_col = jnp.sum(jnp.where(lane == h, beta_full, 0.0), axis=1, keepdims=True)
    g_col = jnp.sum(jnp.where(lane == h + n_heads, g_full, 0.0), axis=1, keepdims=True)
    beta_s[...] = jnp.broadcast_to(beta_col, (s, LANES))
    g_s[...] = jnp.broadcast_to(g_col, (s, LANES))

    row = lax.broadcasted_iota(jnp.int32, (c, c), 0)
    col = lax.broadcasted_iota(jnp.int32, (c, c), 1)
    incl = row >= col
    strict = row > col
    ltri = incl.astype(F32)
    ones_cc = jnp.ones((c, c), F32)
    eye_b = row == col
    nw = nw_ref[...]

    def chunk(n, state):
        r = pl.ds(pl.multiple_of(n * c, c), c)
        q = q_ref[0, r, :]
        k = k_ref[0, r, :]
        v = v_ref[0, r, :]
        beta = beta_s[r, :]
        gc = _dot_hi(ltri, g_s[r, :])
        gc_i = gc[:, :c]
        gc_j = _dot_hi(ones_cc, jnp.where(eye_b, gc_i, 0.0))
        decay = jnp.where(incl, jnp.exp(jnp.where(incl, gc_i - gc_j, 0.0)), 0.0)
        eg = jnp.exp(gc)
        g_last = gc[c - 1:c, :]
        kb = k * beta
        kbf = k.astype(BF16)
        a_mat = jnp.where(strict, _dot_nt(kb.astype(BF16), kbf) * decay, 0.0)
        t_inv = _unit_lower_inverse(a_mat, row, col)
        rhs = jnp.concatenate([v * beta, kb * eg], axis=1)
        sol = _dot_hi(t_inv, rhs)
        u = sol[:, :HEAD_DIM]
        w = sol[:, HEAD_DIM:]
        intra = _dot_nt(q.astype(BF16), kbf) * decay
        q_dec = q * eg
        k_dec = k * jnp.exp(g_last - gc)
        sb = state.astype(BF16)
        v_new = u - _dot(w.astype(BF16), sb)
        o = _dot(q_dec.astype(BF16), sb) + _dot(intra.astype(BF16), v_new.astype(BF16))
        state = state * jnp.exp(g_last) + _dot_tn(k_dec.astype(BF16), v_new.astype(BF16))
        z = z_ref[0, r, :]
        o = o * lax.rsqrt(jnp.mean(o * o, axis=-1, keepdims=True) + NORM_EPS) * nw * _silu(z)
        o_ref[0, r, :] = o.astype(o_ref.dtype)
        return state

    lax.fori_loop(0, s // c, chunk, jnp.zeros((HEAD_DIM, HEAD_DIM), F32))


def _deltanet(qkv, proj, z_col0, ba, a_log, dt_bias, norm_w, n_heads):
    bsz, s, _ = qkv.shape
    zb0 = z_col0 // HEAD_DIM
    arow = jnp.zeros((1, LANES), F32).at[0, n_heads:2 * n_heads].set(a_log)
    dtrow = jnp.zeros((1, LANES), F32).at[0, n_heads:2 * n_heads].set(dt_bias)
    blk = (1, s, HEAD_DIM)
    return pl.pallas_call(
        functools.partial(_dn_kernel, n_heads=n_heads),
        grid=(bsz, n_heads),
        in_specs=[pl.BlockSpec(blk, lambda b, h: (b, 0, h)),
                  pl.BlockSpec(blk, lambda b, h: (b, 0, h + n_heads)),
                  pl.BlockSpec(blk, lambda b, h: (b, 0, h + 2 * n_heads)),
                  pl.BlockSpec(blk, lambda b, h: (b, 0, h + zb0)),
                  pl.BlockSpec((1, s, LANES), lambda b, h: (b, 0, 0)),
                  pl.BlockSpec((1, LANES), lambda b, h: (0, 0)),
                  pl.BlockSpec((1, LANES), lambda b, h: (0, 0)),
                  pl.BlockSpec((1, HEAD_DIM), lambda b, h: (0, 0))],
        out_specs=pl.BlockSpec(blk, lambda b, h: (b, 0, h)),
        out_shape=jax.ShapeDtypeStruct((bsz, s, n_heads * HEAD_DIM), BF16),
        scratch_shapes=[pltpu.VMEM((s, LANES), F32), pltpu.VMEM((s, LANES), F32)],
        compiler_params=_params("parallel", "arbitrary"),
        name="deltanet",
    )(qkv, qkv, qkv, proj, ba, arow, dtrow, norm_w.reshape(1, HEAD_DIM))


def _rope_tables(ang):
    lane = lax.broadcasted_iota(jnp.int32, ang.shape, 1)
    sin = jnp.sin(ang)
    return jnp.cos(ang), jnp.where(lane < HEAD_DIM // 2, -sin, sin)


def _rope(x, cos, sin_signed):
    return x * cos + pltpu.roll(x, HEAD_DIM // 2, axis=1) * sin_signed


def _rope_kernel(ang_ref, q_ref, ks_ref, kw_ref, vs_ref, vw_ref, qo_ref, kso_ref, kwo_ref, vso_ref, vwo_ref):
    cos, sin = _rope_tables(ang_ref[0])
    for hh in range(NSA_HEADS):
        sl = slice(hh * HEAD_DIM, (hh + 1) * HEAD_DIM)
        qo_ref[0, :, sl] = (_rope(q_ref[0, :, sl], cos, sin) * ATTN_SCALE).astype(BF16)
    for g in range(NSA_KV_HEADS):
        sl = slice(g * HEAD_DIM, (g + 1) * HEAD_DIM)
        kso_ref[0, :, sl] = _rope(ks_ref[0, :, sl], cos, sin).astype(BF16)
        kwo_ref[0, :, sl] = _rope(kw_ref[0, :, sl], cos, sin).astype(BF16)
    vso_ref[0] = vs_ref[0].astype(BF16)
    vwo_ref[0] = vw_ref[0].astype(BF16)


def _rope_qkv(proj, ang, *, ts):
    bsz, s, _ = proj.shape
    qw = NSA_HEADS * HEAD_DIM
    kvw = NSA_KV_HEADS * HEAD_DIM
    kv_spec = lambda blk: pl.BlockSpec((1, ts, kvw), lambda b, i, blk=blk: (b, i, blk))
    kv_out = pl.BlockSpec((1, ts, kvw), lambda b, i: (b, i, 0))
    kv_shape = jax.ShapeDtypeStruct((bsz, s, kvw), BF16)
    base = qw // kvw
    return pl.pallas_call(
        _rope_kernel,
        grid=(bsz, s // ts),
        in_specs=[pl.BlockSpec((1, ts, HEAD_DIM), lambda b, i: (b, i, 0)),
                  pl.BlockSpec((1, ts, qw), lambda b, i: (b, i, 0)),
                  kv_spec(base + 2), kv_spec(base + 4), kv_spec(base + 3), kv_spec(base + 5)],
        out_specs=[pl.BlockSpec((1, ts, qw), lambda b, i: (b, i, 0)), kv_out, kv_out, kv_out, kv_out],
        out_shape=[jax.ShapeDtypeStruct((bsz, s, qw), BF16), kv_shape, kv_shape, kv_shape, kv_shape],
        compiler_params=_params("parallel", "parallel"),
        name="rope_qkv",
    )(ang, proj, proj, proj, proj, proj)


def _gelu_tanh(x):
    return x * (0.5 * (1.0 + jnp.tanh(math.sqrt(2.0 / math.pi) * (x + 0.044715 * (x * x * x)))))


def _compress_kernel(x_ref, w1_ref, w2_ref, pos_ref, ang_ref, o_ref, *, rope):
    nsub = x_ref.shape[1] // CMP_STRIDE
    hid = w1_ref.shape[1]
    pa = jnp.zeros((nsub, hid), F32)
    pb = jnp.zeros((nsub, hid), F32)
    for l in range(CMP_STRIDE):
        xl = x_ref[0, pl.ds(l, nsub, stride=CMP_STRIDE), :].astype(BF16)
        pa = pa + _dot(xl, w1_ref[l * HEAD_DIM:(l + 1) * HEAD_DIM, :])
        pb = pb + _dot(xl, w1_ref[(CMP_STRIDE + l) * HEAD_DIM:(CMP_STRIDE + l + 1) * HEAD_DIM, :])
    bias = _dot(pos_ref[...], w1_ref[...])[0:1, :]
    hpre = pa + pltpu.roll(pb, nsub - 1, axis=0) + bias
    out = _dot(_gelu_tanh(hpre).astype(BF16), w2_ref[...])
    if rope:
        cos, sin = _rope_tables(ang_ref[0])
        out = _rope(out, cos, sin)
    o_ref[0, 0] = out.astype(o_ref.dtype)


def _compress(proj, col0, pos_emb, w1, w2, ang_cmp, *, rope):
    bsz, s, _ = proj.shape
    nsub = s // CMP_STRIDE
    blk0 = col0 // HEAD_DIM
    hid = w1.shape[1]
    pos = jnp.zeros((8, CMP_BLOCK * HEAD_DIM), BF16).at[0].set(pos_emb.reshape(-1).astype(BF16))
    return pl.pallas_call(
        functools.partial(_compress_kernel, rope=rope),
        grid=(bsz, NSA_KV_HEADS),
        in_specs=[pl.BlockSpec((1, s, HEAD_DIM), lambda b, g: (b, 0, g + blk0)),
                  pl.BlockSpec((CMP_BLOCK * HEAD_DIM, hid), lambda b, g: (0, 0)),
                  pl.BlockSpec((hid, HEAD_DIM), lambda b, g: (0, 0)),
                  pl.BlockSpec((8, CMP_BLOCK * HEAD_DIM), lambda b, g: (0, 0)),
                  pl.BlockSpec((1, nsub, HEAD_DIM), lambda b, g: (b, 0, 0))],
        out_specs=pl.BlockSpec((1, 1, nsub, HEAD_DIM), lambda b, g: (b, g, 0, 0)),
        out_shape=jax.ShapeDtypeStruct((bsz, NSA_KV_HEADS, nsub, HEAD_DIM), BF16),
        compiler_params=_params("parallel", "parallel"),
        name="compress",
    )(proj, w1.astype(BF16), w2.astype(BF16), pos, ang_cmp)


def _cmp_attn_kernel(q_ref, kc_ref, vc_ref, smat_ref, o_ref, sel_ref):
    tq = q_ref.shape[1]
    ncol = kc_ref.shape[2]
    t = pl.program_id(2) * tq + lax.broadcasted_iota(jnp.int32, (tq, ncol), 0)
    n = lax.broadcasted_iota(jnp.int32, (tq, ncol), 1)
    valid = (n * CMP_STRIDE + CMP_BLOCK - 1) <= t
    kc = kc_ref[0, 0]
    vc = vc_ref[0, 0]
    p_grp = jnp.zeros((tq, ncol), F32)
    for hh in range(HPG):
        sl = slice(hh * HEAD_DIM, (hh + 1) * HEAD_DIM)
        sc = jnp.where(valid, _dot_nt(q_ref[0, :, sl], kc), NEG_INF)
        e = jnp.exp(sc - jnp.max(sc, axis=-1, keepdims=True))
        p = jnp.where(valid, e / jnp.sum(e, axis=-1, keepdims=True), 0.0)
        o_ref[0, :, sl] = _dot(p.astype(BF16), vc)
        p_grp = p_grp + p
    score = _dot_hi(p_grp, smat_ref[...])
    cur = t >> int(math.log2(SEL_BLOCK))
    forced = (n == 0) | (n == cur) | (n == cur - 1)
    future = n * SEL_BLOCK > t
    score = jnp.where(forced, jnp.inf, jnp.where(future, -jnp.inf, score))
    rank = jnp.zeros((tq, ncol), jnp.int32)
    for kk in range(sel_ref.shape[3]):
        ck = score[:, kk:kk + 1]
        ahead = (ck > score) | ((ck == score) & (kk < n))
        rank = rank + ahead.astype(jnp.int32)
    sel = (rank < N_SELECT).astype(sel_ref.dtype)
    sel_ref[0, 0] = sel[:, :sel_ref.shape[3]]


def _sel_matrix(ncol, n_sel):
    rs = SEL_BLOCK // CMP_STRIDE
    rc = CMP_BLOCK // CMP_STRIDE
    mat = [[0.0] * ncol for _ in range(ncol)]
    for j in range(n_sel):
        for m in range(rs):
            for n in range(rc):
                i = rs * j + m + n - (rc - 1)
                if 0 <= i < ncol - 1:
                    mat[i][j] += 1.0
    return jnp.array(mat, F32)


def _cmp_attention(q_r, k_cmp, v_cmp, *, tq):
    bsz, s, _ = q_r.shape
    ncol = k_cmp.shape[2]
    n_sel = s // SEL_BLOCK
    gw = HPG * HEAD_DIM
    return pl.pallas_call(
        _cmp_attn_kernel,
        grid=(bsz, NSA_KV_HEADS, s // tq),
        in_specs=[pl.BlockSpec((1, tq, gw), lambda b, g, i: (b, i, g)),
                  pl.BlockSpec((1, 1, ncol, HEAD_DIM), lambda b, g, i: (b, g, 0, 0)),
                  pl.BlockSpec((1, 1, ncol, HEAD_DIM), lambda b, g, i: (b, g, 0, 0)),
                  pl.BlockSpec((ncol, ncol), lambda b, g, i: (0, 0))],
        out_specs=[pl.BlockSpec((1, tq, gw), lambda b, g, i: (b, i, g)),
                   pl.BlockSpec((1, 1, tq, n_sel), lambda b, g, i: (b, g, i, 0))],
        out_shape=[jax.ShapeDtypeStruct((bsz, s, NSA_HEADS * HEAD_DIM), F32),
                   jax.ShapeDtypeStruct((bsz, NSA_KV_HEADS, s, n_sel), BF16)],
        compiler_params=_params("parallel", "parallel", "parallel"),
        name="cmp_attention",
    )(q_r, k_cmp, v_cmp, _sel_matrix(ncol, n_sel))


def _online_update(carry, sc, vv):
    m, l, acc = carry
    m_new = jnp.maximum(m, jnp.max(sc, axis=-1, keepdims=True))
    alpha = jnp.exp(m - m_new)
    p = jnp.exp(sc - m_new)
    l = alpha * l + jnp.sum(p, axis=-1, keepdims=True)
    acc = alpha * acc + _dot(p.astype(BF16), vv)
    return m_new, l, acc


def _slc_win_kernel(q_ref, ks_ref, vs_ref, kw_ref, vw_ref, sel_ref, expand_ref, oc_ref, gate_ref, o_ref, selx_s,
                    *, tk):
    tq = q_ref.shape[1]
    g = pl.program_id(1)
    t0 = pl.program_id(2) * tq
    rows = HPG * tq
    s = ks_ref.shape[1]
    qs = jnp.concatenate([q_ref[0, :, hh * HEAD_DIM:(hh + 1) * HEAD_DIM] for hh in range(HPG)], axis=0)
    init = (jnp.full((rows, 1), NEG_INF, F32), jnp.zeros((rows, 1), F32), jnp.zeros((rows, HEAD_DIM), F32))

    selx = _dot(sel_ref[0, 0], expand_ref[...])
    for cc in range(s // tk):
        selx_s[cc] = selx[:, cc * tk:(cc + 1) * tk]
    t_loc = lax.broadcasted_iota(jnp.int32, (tq, tk), 0)
    k_loc = lax.broadcasted_iota(jnp.int32, (tq, tk), 1)

    def slc_step(cc, carry):
        k0 = pl.multiple_of(cc * tk, tk)
        kk = ks_ref[0, pl.ds(k0, tk), :]
        vv = vs_ref[0, pl.ds(k0, tk), :]
        ok = jnp.where((k0 + k_loc) <= (t0 + t_loc), selx_s[cc], 0.0) > 0.5
        bias = jnp.where(ok, 0.0, NEG_INF)
        sc = _dot_nt(qs, kk) + jnp.concatenate([bias] * HPG, axis=0)
        return _online_update(carry, sc, vv)

    _, l_s, acc_s = lax.fori_loop(0, (t0 + tq + tk - 1) // tk, slc_step, init)
    o_slc = acc_s / l_s

    d_loc = (lax.broadcasted_iota(jnp.int32, (tq, tq), 1) - lax.broadcasted_iota(jnp.int32, (tq, tq), 0))

    def win_step(cc, carry):
        k0 = pl.multiple_of(t0 - WINDOW + cc * tq, tq)
        kk = kw_ref[0, pl.ds(k0, tq), :]
        vv = vw_ref[0, pl.ds(k0, tq), :]
        dist = d_loc + (k0 - t0)
        bias = jnp.where((dist <= 0) & (dist > -WINDOW), 0.0, NEG_INF)
        sc = _dot_nt(qs, kk) + jnp.concatenate([bias] * HPG, axis=0)
        return _online_update(carry, sc, vv)

    n_win = WINDOW // tq + 1
    first = jnp.maximum(0, (WINDOW - t0) // tq)
    _, l_w, acc_w = lax.fori_loop(first, n_win, win_step, init)
    o_win = acc_w / l_w

    gt = _sigmoid(gate_ref[0])
    lane = lax.broadcasted_iota(jnp.int32, gt.shape, 1)
    for hh in range(HPG):
        base = (g * HPG + hh) * 3
        gsel = [jnp.sum(jnp.where(lane == base + c, gt, 0.0), axis=1, keepdims=True) for c in range(3)]
        sl = slice(hh * HEAD_DIM, (hh + 1) * HEAD_DIM)
        rs = slice(hh * tq, (hh + 1) * tq)
        o = gsel[0] * oc_ref[0, :, sl] + gsel[1] * o_slc[rs] + gsel[2] * o_win[rs]
        o_ref[0, :, sl] = o.astype(o_ref.dtype)


def _slc_win_attention(q_r, ks, vs, kw, vw, sel, o_cmp, gates, *, tq, tk):
    bsz, s, _ = q_r.shape
    n_sel = s // SEL_BLOCK
    gw = HPG * HEAD_DIM
    expand = (jnp.arange(s)[None, :] // SEL_BLOCK == jnp.arange(n_sel)[:, None]).astype(BF16)
    kv_spec = pl.BlockSpec((1, s, HEAD_DIM), lambda b, g, i: (b, 0, g))
    return pl.pallas_call(
        functools.partial(_slc_win_kernel, tk=tk),
        grid=(bsz, NSA_KV_HEADS, s // tq),
        in_specs=[pl.BlockSpec((1, tq, gw), lambda b, g, i: (b, i, g)),
                  kv_spec, kv_spec, kv_spec, kv_spec,
                  pl.BlockSpec((1, 1, tq, n_sel), lambda b, g, i: (b, g, i, 0)),
                  pl.BlockSpec((n_sel, s), lambda b, g, i: (0, 0)),
                  pl.BlockSpec((1, tq, gw), lambda b, g, i: (b, i, g)),
                  pl.BlockSpec((1, tq, LANES), lambda b, g, i: (b, i, 0))],
        out_specs=pl.BlockSpec((1, tq, gw), lambda b, g, i: (b, i, g)),
        out_shape=jax.ShapeDtypeStruct((bsz, s, NSA_HEADS * HEAD_DIM), BF16),
        scratch_shapes=[pltpu.VMEM((s // tk, tq, tk), F32)],
        compiler_params=_params("parallel", "parallel", "arbitrary"),
        name="slc_win_attention",
    )(q_r, ks, vs, kw, vw, sel, expand, o_cmp, gates)


def _pad_cols(w, n):
    return jnp.pad(w, ((0, 0), (0, n - w.shape[1])))


def _conv_deltanet_mixer(xb, bsz, s, w_in, sc_conv_w, dn_conv_w, a_log, dt_bias, norm_w, w_out):
    sc_w = sc_conv_w.shape[1]
    dn_w = dn_conv_w.shape[1] // 3
    n_heads = dn_w // HEAD_DIM
    main = 3 * sc_w + 4 * dn_w
    proj = _matmul(xb, w_in[:, :main].astype(BF16), tm=1024, tn=1024, out_dtype=F32).reshape(bsz, s, main)
    ba = _matmul(xb, _pad_cols(w_in[:, main:], LANES).astype(BF16), tm=1024, tn=LANES, out_dtype=F32)
    y_sc = _short_conv(proj, sc_conv_w, sc_w, tc=256)
    qkv = _dn_conv(proj, dn_conv_w, 3 * sc_w, n_heads)
    y_dn = _deltanet(qkv, proj, 3 * sc_w + 3 * dn_w, ba.reshape(bsz, s, LANES), a_log, dt_bias, norm_w, n_heads)
    wo = w_out.astype(BF16)
    return _matmul2(y_sc.reshape(bsz * s, sc_w), y_dn.reshape(bsz * s, dn_w), wo[:sc_w], wo[sc_w:], tm=1024, tn=1024)


def _nsa_mixer(xb, bsz, s, positions, w_in, cmp_pos_k, cmp_w1_k, cmp_w2_k, cmp_pos_v, cmp_w1_v, cmp_w2_v, w_out):
    qw = NSA_HEADS * HEAD_DIM
    kvw = NSA_KV_HEADS * HEAD_DIM
    main = qw + 6 * kvw
    proj = _matmul(xb, w_in[:, :main].astype(BF16), tm=1024, tn=1024, out_dtype=F32).reshape(bsz, s, main)
    gates = _matmul(xb, _pad_cols(w_in[:, main:], LANES).astype(BF16), tm=1024, tn=LANES, out_dtype=F32)
    half = HEAD_DIM // 2
    inv = jnp.power(ROPE_THETA, -jnp.arange(half, dtype=F32) / half)
    inv = jnp.concatenate([inv, inv])
    ang = positions.astype(F32)[..., None] * inv
    cmp_end = jnp.minimum(jnp.arange(s // CMP_STRIDE) * CMP_STRIDE + CMP_BLOCK - 1, s - 1)
    ang_cmp = positions[:, cmp_end].astype(F32)[..., None] * inv
    q_r, ks, kw, vs, vw = _rope_qkv(proj, ang, ts=512)
    k_cmp = _compress(proj, qw, cmp_pos_k, cmp_w1_k, cmp_w2_k, ang_cmp, rope=True)
    v_cmp = _compress(proj, qw + kvw, cmp_pos_v, cmp_w1_v, cmp_w2_v, ang_cmp, rope=False)
    o_cmp, sel = _cmp_attention(q_r, k_cmp, v_cmp, tq=256)
    o = _slc_win_attention(q_r, ks, vs, kw, vw, sel, o_cmp, gates.reshape(bsz, s, LANES), tq=128, tk=256)
    return _matmul(o.reshape(bsz * s, qw), w_out.astype(BF16), tm=1024, tn=1024, out_dtype=F32)


def _ffn(xb, w_in, w_out):
    hmid = _ffn_in(xb, w_in.astype(BF16), tm=1024, tn=512)
    return _matmul(hmid, w_out.astype(BF16), tm=1024, tn=512, out_dtype=F32)


def kernel(x, positions, ln_mix_g, ln_mix_b, ln_ffn_g, ln_ffn_b, ffn_w_in, ffn_w_out, hy_w_in, sc_conv_w, dn_conv_w, dn_a_log, dn_dt_bias, dn_norm_w, hy_w_out, nsa_w_in, cmp_pos_k, cmp_w1_k, cmp_w2_k, cmp_pos_v, cmp_w1_v, cmp_w2_v, nsa_w_out):
    bsz, s, d = x.shape
    xf = x.reshape(bsz * s, d)
    xb = xf
    for i in range(DEPTH):
        j = i // 2
        if i % 2 == 0:
            y = _conv_deltanet_mixer(xb, bsz, s, hy_w_in[j], sc_conv_w[j], dn_conv_w[j], dn_a_log[j],
                                     dn_dt_bias[j], dn_norm_w[j], hy_w_out[j])
        else:
            y = _nsa_mixer(xb, bsz, s, positions, nsa_w_in[j], cmp_pos_k[j], cmp_w1_k[j], cmp_w2_k[j],
                           cmp_pos_v[j], cmp_w1_v[j], cmp_w2_v[j], nsa_w_out[j])
        xf, xb = _add_ln(xf, y, ln_mix_g[i], ln_mix_b[i], tm=512)
        xf, xb = _add_ln(xf, _ffn(xb, ffn_w_in[i], ffn_w_out[i]), ln_ffn_g[i], ln_ffn_b[i], tm=512)
    return xf.reshape(bsz, s, d)
```

```python
import functools
import math

import jax
import jax.numpy as jnp
from jax import lax
from jax.experimental import pallas as pl
from jax.experimental.pallas import tpu as pltpu

F32 = jnp.float32
BF16 = jnp.bfloat16
HIGHEST = lax.Precision.HIGHEST

LANES = 128
VMEM_LIMIT = 48 * 1024 * 1024

DN_HEADS = 8
DN_CHUNK = 64
DN_CONV = 4
SC_KERNEL = 3
NSA_HEADS = 16
NSA_KV_HEADS = 4
HPG = NSA_HEADS // NSA_KV_HEADS
HEAD_DIM = 128
CMP_BLOCK = 32
CMP_STRIDE = 16
SEL_BLOCK = 64
N_SELECT = 16
WINDOW = 512
ROPE_THETA = 10000.0
LN_EPS = 1e-5
NORM_EPS = 1e-6
NEG_INF = -1e30
DEPTH = 2
ALPHA = (2 * DEPTH) ** 0.25
ATTN_SCALE = HEAD_DIM ** -0.5


def _params(*sem):
    return pltpu.CompilerParams(dimension_semantics=sem, vmem_limit_bytes=VMEM_LIMIT)


def _sigmoid(x):
    return 1.0 / (1.0 + jnp.exp(-x))


def _silu(x):
    return x * _sigmoid(x)


def _dot(a, b):
    return jnp.dot(a, b, preferred_element_type=F32)


def _dot_nt(a, b):
    return lax.dot_general(a, b, (((1,), (1,)), ((), ())), preferred_element_type=F32)


def _dot_tn(a, b):
    return lax.dot_general(a, b, (((0,), (0,)), ((), ())), preferred_element_type=F32)


def _dot_hi(a, b):
    return jnp.dot(a, b, precision=HIGHEST, preferred_element_type=F32)


def _mm_kernel(x_ref, w_ref, o_ref):
    o_ref[...] = _dot(x_ref[...].astype(BF16), w_ref[...]).astype(o_ref.dtype)


def _matmul(x, w, *, tm, tn, out_dtype):
    m, k = x.shape
    n = w.shape[1]
    return pl.pallas_call(
        _mm_kernel,
        grid=(m // tm, n // tn),
        in_specs=[pl.BlockSpec((tm, k), lambda i, j: (i, 0)),
                  pl.BlockSpec((k, tn), lambda i, j: (0, j))],
        out_specs=pl.BlockSpec((tm, tn), lambda i, j: (i, j)),
        out_shape=jax.ShapeDtypeStruct((m, n), out_dtype),
        compiler_params=_params("parallel", "parallel"),
        name="matmul",
    )(x, w)


def _mm2_kernel(x1_ref, x2_ref, w1_ref, w2_ref, o_ref):
    acc = _dot(x1_ref[...], w1_ref[...]) + _dot(x2_ref[...], w2_ref[...])
    o_ref[...] = acc.astype(o_ref.dtype)


def _matmul2(x1, x2, w1, w2, *, tm, tn):
    m, k1 = x1.shape
    k2 = x2.shape[1]
    n = w1.shape[1]
    return pl.pallas_call(
        _mm2_kernel,
        grid=(m // tm, n // tn),
        in_specs=[pl.BlockSpec((tm, k1), lambda i, j: (i, 0)),
                  pl.BlockSpec((tm, k2), lambda i, j: (i, 0)),
                  pl.BlockSpec((k1, tn), lambda i, j: (0, j)),
                  pl.BlockSpec((k2, tn), lambda i, j: (0, j))],
        out_specs=pl.BlockSpec((tm, tn), lambda i, j: (i, j)),
        out_shape=jax.ShapeDtypeStruct((m, n), F32),
        compiler_params=_params("parallel", "parallel"),
        name="matmul2",
    )(x1, x2, w1, w2)


def _ffn_in_kernel(x_ref, wg_ref, wu_ref, o_ref):
    x = x_ref[...]
    gate = _dot(x, wg_ref[...])
    up = _dot(x, wu_ref[...])
    o_ref[...] = (_silu(gate) * up).astype(o_ref.dtype)


def _ffn_in(xb, w_in, *, tm, tn):
    m, k = xb.shape
    hidden = w_in.shape[1] // 2
    nj = hidden // tn
    return pl.pallas_call(
        _ffn_in_kernel,
        grid=(m // tm, nj),
        in_specs=[pl.BlockSpec((tm, k), lambda i, j: (i, 0)),
                  pl.BlockSpec((k, tn), lambda i, j: (0, j)),
                  pl.BlockSpec((k, tn), lambda i, j: (0, j + nj))],
        out_specs=pl.BlockSpec((tm, tn), lambda i, j: (i, j)),
        out_shape=jax.ShapeDtypeStruct((m, hidden), BF16),
        compiler_params=_params("parallel", "parallel"),
        name="ffn_in",
    )(xb, w_in, w_in)


def _add_ln_kernel(x_ref, y_ref, g_ref, b_ref, o_ref, ob_ref):
    v = ALPHA * x_ref[...] + y_ref[...]
    mu = jnp.mean(v, axis=-1, keepdims=True)
    d = v - mu
    var = jnp.mean(d * d, axis=-1, keepdims=True)
    out = d * lax.rsqrt(var + LN_EPS) * g_ref[...] + b_ref[...]
    o_ref[...] = out
    ob_ref[...] = out.astype(BF16)


def _add_ln(x, y, g, b, *, tm):
    m, d = x.shape
    return pl.pallas_call(
        _add_ln_kernel,
        grid=(m // tm,),
        in_specs=[pl.BlockSpec((tm, d), lambda i: (i, 0)),
                  pl.BlockSpec((tm, d), lambda i: (i, 0)),
                  pl.BlockSpec((1, d), lambda i: (0, 0)),
                  pl.BlockSpec((1, d), lambda i: (0, 0))],
        out_specs=[pl.BlockSpec((tm, d), lambda i: (i, 0)),
                   pl.BlockSpec((tm, d), lambda i: (i, 0))],
        out_shape=[jax.ShapeDtypeStruct((m, d), F32), jax.ShapeDtypeStruct((m, d), BF16)],
        compiler_params=_params("parallel"),
        name="add_ln",
    )(x, y, g.reshape(1, d), b.reshape(1, d))


def _causal_conv(u, w_ref, taps):
    row = lax.broadcasted_iota(jnp.int32, u.shape, 0)
    acc = u * w_ref[taps - 1:taps, :]
    for sh in range(1, taps):
        shifted = jnp.where(row >= sh, pltpu.roll(u, sh, axis=0), 0.0)
        acc = acc + shifted * w_ref[taps - 1 - sh:taps - sh, :]
    return acc


def _sc_kernel(b_ref, c_ref, h_ref, w_ref, o_ref):
    u = c_ref[0] * h_ref[0]
    o_ref[0] = (b_ref[0] * _causal_conv(u, w_ref, SC_KERNEL)).astype(o_ref.dtype)


def _short_conv(proj, conv_w, width, *, tc):
    bsz, s, _ = proj.shape
    nb = width // tc
    w = jnp.zeros((8, width), F32).at[:SC_KERNEL].set(conv_w)
    return pl.pallas_call(
        _sc_kernel,
        grid=(bsz, nb),
        in_specs=[pl.BlockSpec((1, s, tc), lambda b, j: (b, 0, j)),
                  pl.BlockSpec((1, s, tc), lambda b, j: (b, 0, j + nb)),
                  pl.BlockSpec((1, s, tc), lambda b, j: (b, 0, j + 2 * nb)),
                  pl.BlockSpec((8, tc), lambda b, j: (0, j))],
        out_specs=pl.BlockSpec((1, s, tc), lambda b, j: (b, 0, j)),
        out_shape=jax.ShapeDtypeStruct((bsz, s, width), BF16),
        compiler_params=_params("parallel", "parallel"),
        name="short_conv",
    )(proj, proj, proj, w)


def _split(x):
    hi = x.astype(BF16)
    return hi, (x - hi.astype(F32)).astype(BF16)


def _dot3(a, b):
    return _dot(a[0], b[0]) + (_dot(a[0], b[1]) + _dot(a[1], b[0]))


DN_GROUP = 8


def _dn_group_local(base, scr, masks, out):
    q_s, k_s, kb_s, qd_s, kf_s, kbe_s, vb_s, gc_s, gct_s = scr
    incl, strict, m8, m16, eye = masks
    c = DN_CHUNK
    idx = range(DN_GROUP)
    rows = [pl.ds(base + cc * c, c) for cc in idx]
    gc = [gc_s[r, :] for r in rows]
    decay = []
    for cc in idx:
        gc_j = gct_s[pl.ds(base + (cc // 2) * LANES, c), (cc % 2) * c:(cc % 2) * c + c]
        decay.append(jnp.where(incl, jnp.exp(jnp.where(incl, gc[cc][:, :c] - gc_j, 0.0)), 0.0))
    kbf = [k_s[r, :] for r in rows]
    kk = [_dot_nt(kb_s[rows[cc], :], kbf[cc]) for cc in idx]
    qk = [_dot_nt(q_s[rows[cc], :], kbf[cc]) for cc in idx]
    yield
    a = [jnp.where(strict, kk[cc] * decay[cc], 0.0) for cc in idx]
    intra = [(qk[cc] * decay[cc]).astype(BF16) for cc in idx]
    ad = [jnp.where(m8, x, 0.0) for x in a]
    ads = [_split(x) for x in ad]
    a2s = [_split(_dot3(x, x)) for x in ads]
    yield
    p = [eye - x for x in ad]
    p1, a4s = [], []
    for cc in idx:
        p1.append(p[cc] + _dot3(_split(p[cc]), a2s[cc]))
        a4s.append(_split(_dot3(a2s[cc], a2s[cc])))
    yield
    p2 = [p1[cc] + _dot3(_split(p1[cc]), a4s[cc]) for cc in idx]
    yield
    ps = [_split(x) for x in p2]
    t = [_split(_dot3(ps[cc], _split(jnp.where(m16, a[cc] - ad[cc], 0.0)))) for cc in idx]
    yield
    ds = [_split(p2[cc] - _dot3(t[cc], ps[cc])) for cc in idx]
    yield
    db, das = [], []
    for cc in idx:
        rhs = jnp.concatenate([vb_s[rows[cc], :], kbe_s[rows[cc], :]], axis=1)
        db.append(_dot3(ds[cc], _split(rhs)))
        das.append(_split(_dot3(ds[cc], _split(jnp.where(m16, 0.0, a[cc])))))
    yield
    blocks = [[x[0:16]] for x in db]
    for s4 in range(1, c // 16):
        rs = slice(16 * s4, 16 * s4 + 16)
        for cc in idx:
            xprev = jnp.concatenate(blocks[cc] + [jnp.zeros((c - 16 * s4, 2 * HEAD_DIM), F32)], axis=0)
            blocks[cc].append(db[cc][rs] - _dot3((das[cc][0][rs], das[cc][1][rs]), _split(xprev)))
        yield
    for cc in idx:
        sol = jnp.concatenate(blocks[cc], axis=0)
        sol_hi, sol_lo = _split(sol)
        g_last = gc[cc][c - 1:c, :]
        k_dec_t = (kf_s[rows[cc], :] * jnp.exp(g_last - gc[cc])).T.astype(BF16)
        kw = _dot(k_dec_t, sol_hi) + _dot(k_dec_t, sol_lo)
        iw = _dot(intra[cc], sol_hi) + _dot(intra[cc], sol_lo)
        out.append((kw[:, HEAD_DIM:].astype(BF16), kw[:, :HEAD_DIM],
                    (qd_s[rows[cc], :] - iw[:, HEAD_DIM:]).astype(BF16), iw[:, :HEAD_DIM], jnp.exp(g_last)))
    yield


def _dn_chunk_seq(state, loc, z, nw):
    w2, n_mat, qp, op, eg_last = loc
    sb = state.astype(BF16)
    o = _dot(qp, sb) + op
    state = (state * eg_last - _dot(w2, sb)) + n_mat
    o = o * lax.rsqrt(jnp.mean(o * o, axis=-1, keepdims=True) + NORM_EPS) * nw * _silu(z)
    return state, o


def _dn_kernel(qp_ref, kp_ref, vp_ref, z_ref, ba_ref, arow_ref, dtrow_ref, cwq_ref, cwk_ref, cwv_ref, nw_ref, o_ref,
               q_s, k_s, kb_s, qd_s, kf_s, kbe_s, vb_s, gc_s, gct_s, *, n_heads):
    h = pl.program_id(1)
    s = qp_ref.shape[1]
    c = DN_CHUNK
    ba = ba_ref[0]
    lane = lax.broadcasted_iota(jnp.int32, ba.shape, 1)
    xa = ba + dtrow_ref[...]
    softplus = jnp.maximum(xa, 0.0) + jnp.log(1.0 + jnp.exp(-jnp.abs(xa)))
    g_full = -jnp.exp(arow_ref[...]) * softplus
    beta = jnp.sum(jnp.where(lane == h, _sigmoid(ba), 0.0), axis=1, keepdims=True)
    g_col = jnp.sum(jnp.where(lane == h + n_heads, g_full, 0.0), axis=1, keepdims=True)
    gc = jnp.broadcast_to(g_col, (s, LANES))
    pos = lax.broadcasted_iota(jnp.int32, (s, LANES), 0) & (c - 1)
    sh = 1
    while sh < c:
        gc = gc + jnp.where(pos >= sh, pltpu.roll(gc, sh, axis=0), 0.0)
        sh *= 2
    gc_s[...] = gc
    for blk in range(s // LANES):
        rs = slice(blk * LANES, (blk + 1) * LANES)
        gct_s[rs, :] = gc[rs, :].T
    eg = jnp.exp(gc)
    q = _silu(_causal_conv(qp_ref[0], cwq_ref, DN_CONV))
    q = q * (lax.rsqrt(jnp.sum(q * q, axis=-1, keepdims=True) + NORM_EPS) * (HEAD_DIM ** -0.5))
    q_s[...] = q.astype(BF16)
    qd_s[...] = q * eg
    k = _silu(_causal_conv(kp_ref[0], cwk_ref, DN_CONV))
    k = k * lax.rsqrt(jnp.sum(k * k, axis=-1, keepdims=True) + NORM_EPS)
    kb = k * beta
    kf_s[...] = k
    k_s[...] = k.astype(BF16)
    kb_s[...] = kb.astype(BF16)
    kbe_s[...] = kb * eg
    vb_s[...] = _silu(_causal_conv(vp_ref[0], cwv_ref, DN_CONV)) * beta

    row = lax.broadcasted_iota(jnp.int32, (c, c), 0)
    col = lax.broadcasted_iota(jnp.int32, (c, c), 1)
    masks = (row >= col, row > col, (row >> 3) == (col >> 3), (row >> 4) == (col >> 4), (row == col).astype(F32))
    scr = (q_s, k_s, kb_s, qd_s, kf_s, kbe_s, vb_s, gc_s, gct_s)
    nw = nw_ref[...]
    rows_per_group = DN_GROUP * c
    n_groups = s // rows_per_group

    def group_base(gi):
        base = gi * rows_per_group
        return base if isinstance(base, int) else pl.multiple_of(base, rows_per_group)

    def run(gi_local, gi_seq, state, locs):
        nxt = []
        stages = iter(()) if gi_local is None else _dn_group_local(group_base(gi_local), scr, masks, nxt)
        todo = list(range(DN_GROUP)) if gi_seq is not None else []
        done = False
        while todo or not done:
            if not done:
                done = next(stages, "end") == "end"
            if todo:
                cc = todo.pop(0)
                rows = pl.ds(group_base(gi_seq) + cc * c, c)
                state, o = _dn_chunk_seq(state, locs[cc], z_ref[0, rows, :], nw)
                o_ref[0, rows, :] = o.astype(o_ref.dtype)
        return state, tuple(nxt)

    def body(gi, carry):
        return run(gi + 1, gi, *carry)

    carry = run(0, None, jnp.zeros((HEAD_DIM, HEAD_DIM), F32), None)
    carry = lax.fori_loop(0, n_groups - 1, body, carry)
    run(None, n_groups - 1, *carry)


def _deltanet(proj, qkv_col0, z_col0, conv_w, ba, a_log, dt_bias, norm_w, n_heads):
    bsz, s, _ = proj.shape
    qb0 = qkv_col0 // HEAD_DIM
    zb0 = z_col0 // HEAD_DIM
    arow = jnp.zeros((1, LANES), F32).at[0, n_heads:2 * n_heads].set(a_log)
    dtrow = jnp.zeros((1, LANES), F32).at[0, n_heads:2 * n_heads].set(dt_bias)
    cw = jnp.zeros((8, 3 * n_heads * HEAD_DIM), F32).at[:DN_CONV].set(conv_w)
    blk = (1, s, HEAD_DIM)
    col_spec = lambda off: pl.BlockSpec(blk, lambda b, h, off=off: (b, 0, h + off))
    cw_spec = lambda off: pl.BlockSpec((8, HEAD_DIM), lambda b, h, off=off: (0, h + off))
    row_spec = pl.BlockSpec((1, LANES), lambda b, h: (0, 0))
    return pl.pallas_call(
        functools.partial(_dn_kernel, n_heads=n_heads),
        grid=(bsz, n_heads),
        in_specs=[col_spec(qb0), col_spec(qb0 + n_heads), col_spec(qb0 + 2 * n_heads), col_spec(zb0),
                  pl.BlockSpec((1, s, LANES), lambda b, h: (b, 0, 0)), row_spec, row_spec,
                  cw_spec(0), cw_spec(n_heads), cw_spec(2 * n_heads), row_spec],
        out_specs=pl.BlockSpec(blk, lambda b, h: (b, 0, h)),
        out_shape=jax.ShapeDtypeStruct((bsz, s, n_heads * HEAD_DIM), BF16),
        scratch_shapes=[pltpu.VMEM((s, HEAD_DIM), BF16)] * 3 + [pltpu.VMEM((s, HEAD_DIM), F32)] * 6,
        compiler_params=_params("parallel", "arbitrary"),
        name="deltanet",
    )(proj, proj, proj, proj, ba, arow, dtrow, cw, cw, cw, norm_w.reshape(1, HEAD_DIM))


def _rope_tables(ang):
    lane = lax.broadcasted_iota(jnp.int32, ang.shape, 1)
    sin = jnp.sin(ang)
    return jnp.cos(ang), jnp.where(lane < HEAD_DIM // 2, -sin, sin)


def _rope(x, cos, sin_signed):
    return x * cos + pltpu.roll(x, HEAD_DIM // 2, axis=1) * sin_signed


def _rope_kernel(ang_ref, q_ref, ks_ref, kw_ref, vs_ref, vw_ref, qo_ref, kso_ref, kwo_ref, vso_ref, vwo_ref):
    cos, sin = _rope_tables(ang_ref[0])
    for hh in range(NSA_HEADS):
        sl = slice(hh * HEAD_DIM, (hh + 1) * HEAD_DIM)
        qo_ref[0, :, sl] = (_rope(q_ref[0, :, sl], cos, sin) * ATTN_SCALE).astype(BF16)
    for g in range(NSA_KV_HEADS):
        sl = slice(g * HEAD_DIM, (g + 1) * HEAD_DIM)
        kso_ref[0, :, sl] = _rope(ks_ref[0, :, sl], cos, sin).astype(BF16)
        kwo_ref[0, :, sl] = _rope(kw_ref[0, :, sl], cos, sin).astype(BF16)
    vso_ref[0] = vs_ref[0].astype(BF16)
    vwo_ref[0] = vw_ref[0].astype(BF16)


def _rope_qkv(proj, ang, *, ts):
    bsz, s, _ = proj.shape
    qw = NSA_HEADS * HEAD_DIM
    kvw = NSA_KV_HEADS * HEAD_DIM
    kv_spec = lambda blk: pl.BlockSpec((1, ts, kvw), lambda b, i, blk=blk: (b, i, blk))
    kv_out = pl.BlockSpec((1, ts, kvw), lambda b, i: (b, i, 0))
    kv_shape = jax.ShapeDtypeStruct((bsz, s, kvw), BF16)
    base = qw // kvw
    return pl.pallas_call(
        _rope_kernel,
        grid=(bsz, s // ts),
        in_specs=[pl.BlockSpec((1, ts, HEAD_DIM), lambda b, i: (b, i, 0)),
                  pl.BlockSpec((1, ts, qw), lambda b, i: (b, i, 0)),
                  kv_spec(base + 2), kv_spec(base + 4), kv_spec(base + 3), kv_spec(base + 5)],
        out_specs=[pl.BlockSpec((1, ts, qw), lambda b, i: (b, i, 0)), kv_out, kv_out, kv_out, kv_out],
        out_shape=[jax.ShapeDtypeStruct((bsz, s, qw), BF16), kv_shape, kv_shape, kv_shape, kv_shape],
        compiler_params=_params("parallel", "parallel"),
        name="rope_qkv",
    )(ang, proj, proj, proj, proj, proj)


def _gelu_tanh(x):
    return x * (0.5 * (1.0 + jnp.tanh(math.sqrt(2.0 / math.pi) * (x + 0.044715 * (x * x * x)))))


def _compress_kernel(x_ref, w1_ref, w2_ref, pos_ref, ang_ref, o_ref, *, rope):
    nsub = x_ref.shape[1] // CMP_STRIDE
    hid = w1_ref.shape[1]
    pa = jnp.zeros((nsub, hid), F32)
    pb = jnp.zeros((nsub, hid), F32)
    for l in range(CMP_STRIDE):
        xl = x_ref[0, pl.ds(l, nsub, stride=CMP_STRIDE), :].astype(BF16)
        pa = pa + _dot(xl, w1_ref[l * HEAD_DIM:(l + 1) * HEAD_DIM, :])
        pb = pb + _dot(xl, w1_ref[(CMP_STRIDE + l) * HEAD_DIM:(CMP_STRIDE + l + 1) * HEAD_DIM, :])
    bias = _dot(pos_ref[...], w1_ref[...])[0:1, :]
    hpre = pa + pltpu.roll(pb, nsub - 1, axis=0) + bias
    out = _dot(_gelu_tanh(hpre).astype(BF16), w2_ref[...])
    if rope:
        cos, sin = _rope_tables(ang_ref[0])
        out = _rope(out, cos, sin)
    o_ref[0, 0] = out.astype(o_ref.dtype)


def _compress(proj, col0, pos_emb, w1, w2, ang_cmp, *, rope):
    bsz, s, _ = proj.shape
    nsub = s // CMP_STRIDE
    blk0 = col0 // HEAD_DIM
    hid = w1.shape[1]
    pos = jnp.zeros((8, CMP_BLOCK * HEAD_DIM), BF16).at[0].set(pos_emb.reshape(-1).astype(BF16))
    return pl.pallas_call(
        functools.partial(_compress_kernel, rope=rope),
        grid=(bsz, NSA_KV_HEADS),
        in_specs=[pl.BlockSpec((1, s, HEAD_DIM), lambda b, g: (b, 0, g + blk0)),
                  pl.BlockSpec((CMP_BLOCK * HEAD_DIM, hid), lambda b, g: (0, 0)),
                  pl.BlockSpec((hid, HEAD_DIM), lambda b, g: (0, 0)),
                  pl.BlockSpec((8, CMP_BLOCK * HEAD_DIM), lambda b, g: (0, 0)),
                  pl.BlockSpec((1, nsub, HEAD_DIM), lambda b, g: (b, 0, 0))],
        out_specs=pl.BlockSpec((1, 1, nsub, HEAD_DIM), lambda b, g: (b, g, 0, 0)),
        out_shape=jax.ShapeDtypeStruct((bsz, NSA_KV_HEADS, nsub, HEAD_DIM), BF16),
        compiler_params=_params("parallel", "parallel"),
        name="compress",
    )(proj, w1.astype(BF16), w2.astype(BF16), pos, ang_cmp)


def _cmp_attn_kernel(q_ref, kc_ref, vc_ref, smat_ref, o_ref, sel_ref):
    tq = q_ref.shape[1]
    ncol = kc_ref.shape[2]
    t = pl.program_id(2) * tq + lax.broadcasted_iota(jnp.int32, (tq, ncol), 0)
    n = lax.broadcasted_iota(jnp.int32, (tq, ncol), 1)
    valid = (n * CMP_STRIDE + CMP_BLOCK - 1) <= t
    kc = kc_ref[0, 0]
    vc = vc_ref[0, 0]
    p_grp = jnp.zeros((tq, ncol), F32)
    for hh in range(HPG):
        sl = slice(hh * HEAD_DIM, (hh + 1) * HEAD_DIM)
        sc = jnp.where(valid, _dot_nt(q_ref[0, :, sl], kc), NEG_INF)
        e = jnp.exp(sc - jnp.max(sc, axis=-1, keepdims=True))
        p = jnp.where(valid, e / jnp.sum(e, axis=-1, keepdims=True), 0.0)
        o_ref[0, :, sl] = _dot(p.astype(BF16), vc)
        p_grp = p_grp + p
    score = _dot_hi(p_grp, smat_ref[...])
    cur = t >> int(math.log2(SEL_BLOCK))
    forced = (n == 0) | (n == cur) | (n == cur - 1)
    future = n * SEL_BLOCK > t
    score = jnp.where(forced, jnp.inf, jnp.where(future, -jnp.inf, score))
    rank = jnp.zeros((tq, ncol), jnp.int32)
    for kk in range(sel_ref.shape[3]):
        ck = score[:, kk:kk + 1]
        ahead = (ck > score) | ((ck == score) & (kk < n))
        rank = rank + ahead.astype(jnp.int32)
    sel = (rank < N_SELECT).astype(sel_ref.dtype)
    sel_ref[0, 0] = sel[:, :sel_ref.shape[3]]


def _sel_matrix(ncol, n_sel):
    rs = SEL_BLOCK // CMP_STRIDE
    rc = CMP_BLOCK // CMP_STRIDE
    mat = [[0.0] * ncol for _ in range(ncol)]
    for j in range(n_sel):
        for m in range(rs):
            for n in range(rc):
                i = rs * j + m + n - (rc - 1)
                if 0 <= i < ncol - 1:
                    mat[i][j] += 1.0
    return jnp.array(mat, F32)


def _cmp_attention(q_r, k_cmp, v_cmp, *, tq):
    bsz, s, _ = q_r.shape
    ncol = k_cmp.shape[2]
    n_sel = s // SEL_BLOCK
    gw = HPG * HEAD_DIM
    return pl.pallas_call(
        _cmp_attn_kernel,
        grid=(bsz, NSA_KV_HEADS, s // tq),
        in_specs=[pl.BlockSpec((1, tq, gw), lambda b, g, i: (b, i, g)),
                  pl.BlockSpec((1, 1, ncol, HEAD_DIM), lambda b, g, i: (b, g, 0, 0)),
                  pl.BlockSpec((1, 1, ncol, HEAD_DIM), lambda b, g, i: (b, g, 0, 0)),
                  pl.BlockSpec((ncol, ncol), lambda b, g, i: (0, 0))],
        out_specs=[pl.BlockSpec((1, tq, gw), lambda b, g, i: (b, i, g)),
                   pl.BlockSpec((1, 1, tq, n_sel), lambda b, g, i: (b, g, i, 0))],
        out_shape=[jax.ShapeDtypeStruct((bsz, s, NSA_HEADS * HEAD_DIM), F32),
                   jax.ShapeDtypeStruct((bsz, NSA_KV_HEADS, s, n_sel), BF16)],
        compiler_params=_params("parallel", "parallel", "parallel"),
        name="cmp_attention",
    )(q_r, k_cmp, v_cmp, _sel_matrix(ncol, n_sel))


def _online_update(carry, sc, vv):
    m, l, acc = carry
    m_new = jnp.maximum(m, jnp.max(sc, axis=-1, keepdims=True))
    alpha = jnp.exp(m - m_new)
    p = jnp.exp(sc - m_new)
    l = alpha * l + jnp.sum(p, axis=-1, keepdims=True)
    acc = alpha * acc + _dot(p.astype(BF16), vv)
    return m_new, l, acc


def _slc_win_kernel(q_ref, ks_ref, vs_ref, kw_ref, vw_ref, sel_ref, expand_ref, oc_ref, gate_ref, o_ref, selx_s,
                    *, tk):
    tq = q_ref.shape[1]
    g = pl.program_id(1)
    t0 = pl.program_id(2) * tq
    rows = HPG * tq
    s = ks_ref.shape[1]
    qs = jnp.concatenate([q_ref[0, :, hh * HEAD_DIM:(hh + 1) * HEAD_DIM] for hh in range(HPG)], axis=0)
    init = (jnp.full((rows, 1), NEG_INF, F32), jnp.zeros((rows, 1), F32), jnp.zeros((rows, HEAD_DIM), F32))

    selx = _dot(sel_ref[0, 0], expand_ref[...])
    for cc in range(s // tk):
        selx_s[cc] = selx[:, cc * tk:(cc + 1) * tk]
    t_loc = lax.broadcasted_iota(jnp.int32, (tq, tk), 0)
    k_loc = lax.broadcasted_iota(jnp.int32, (tq, tk), 1)

    def slc_step(cc, carry):
        k0 = pl.multiple_of(cc * tk, tk)
        kk = ks_ref[0, pl.ds(k0, tk), :]
        vv = vs_ref[0, pl.ds(k0, tk), :]
        ok = jnp.where((k0 + k_loc) <= (t0 + t_loc), selx_s[cc], 0.0) > 0.5
        bias = jnp.where(ok, 0.0, NEG_INF)
        sc = _dot_nt(qs, kk) + jnp.concatenate([bias] * HPG, axis=0)
        return _online_update(carry, sc, vv)

    _, l_s, acc_s = lax.fori_loop(0, (t0 + tq + tk - 1) // tk, slc_step, init)
    o_slc = acc_s / l_s

    d_loc = (lax.broadcasted_iota(jnp.int32, (tq, tq), 1) - lax.broadcasted_iota(jnp.int32, (tq, tq), 0))

    def win_step(cc, carry):
        k0 = pl.multiple_of(t0 - WINDOW + cc * tq, tq)
        kk = kw_ref[0, pl.ds(k0, tq), :]
        vv = vw_ref[0, pl.ds(k0, tq), :]
        dist = d_loc + (k0 - t0)
        bias = jnp.where((dist <= 0) & (dist > -WINDOW), 0.0, NEG_INF)
        sc = _dot_nt(qs, kk) + jnp.concatenate([bias] * HPG, axis=0)
        return _online_update(carry, sc, vv)

    n_win = WINDOW // tq + 1
    first = jnp.maximum(0, (WINDOW - t0) // tq)
    _, l_w, acc_w = lax.fori_loop(first, n_win, win_step, init)
    o_win = acc_w / l_w

    gt = _sigmoid(gate_ref[0])
    lane = lax.broadcasted_iota(jnp.int32, gt.shape, 1)
    for hh in range(HPG):
        base = (g * HPG + hh) * 3
        gsel = [jnp.sum(jnp.where(lane == base + c, gt, 0.0), axis=1, keepdims=True) for c in range(3)]
        sl = slice(hh * HEAD_DIM, (hh + 1) * HEAD_DIM)
        rs = slice(hh * tq, (hh + 1) * tq)
        o = gsel[0] * oc_ref[0, :, sl] + gsel[1] * o_slc[rs] + gsel[2] * o_win[rs]
        o_ref[0, :, sl] = o.astype(o_ref.dtype)


def _slc_win_attention(q_r, ks, vs, kw, vw, sel, o_cmp, gates, *, tq, tk):
    bsz, s, _ = q_r.shape
    n_sel = s // SEL_BLOCK
    gw = HPG * HEAD_DIM
    expand = (jnp.arange(s)[None, :] // SEL_BLOCK == jnp.arange(n_sel)[:, None]).astype(BF16)
    kv_spec = pl.BlockSpec((1, s, HEAD_DIM), lambda b, g, i: (b, 0, g))
    return pl.pallas_call(
        functools.partial(_slc_win_kernel, tk=tk),
        grid=(bsz, NSA_KV_HEADS, s // tq),
        in_specs=[pl.BlockSpec((1, tq, gw), lambda b, g, i: (b, i, g)),
                  kv_spec, kv_spec, kv_spec, kv_spec,
                  pl.BlockSpec((1, 1, tq, n_sel), lambda b, g, i: (b, g, i, 0)),
                  pl.BlockSpec((n_sel, s), lambda b, g, i: (0, 0)),
                  pl.BlockSpec((1, tq, gw), lambda b, g, i: (b, i, g)),
                  pl.BlockSpec((1, tq, LANES), lambda b, g, i: (b, i, 0))],
        out_specs=pl.BlockSpec((1, tq, gw), lambda b, g, i: (b, i, g)),
        out_shape=jax.ShapeDtypeStruct((bsz, s, NSA_HEADS * HEAD_DIM), BF16),
        scratch_shapes=[pltpu.VMEM((s // tk, tq, tk), F32)],
        compiler_params=_params("parallel", "parallel", "arbitrary"),
        name="slc_win_attention",
    )(q_r, ks, vs, kw, vw, sel, expand, o_cmp, gates)


def _pad_cols(w, n):
    return jnp.pad(w, ((0, 0), (0, n - w.shape[1])))


def _conv_deltanet_mixer(xb, bsz, s, w_in, sc_conv_w, dn_conv_w, a_log, dt_bias, norm_w, w_out):
    sc_w = sc_conv_w.shape[1]
    dn_w = dn_conv_w.shape[1] // 3
    n_heads = dn_w // HEAD_DIM
    main = 3 * sc_w + 4 * dn_w
    proj = _matmul(xb, w_in[:, :main].astype(BF16), tm=1024, tn=1024, out_dtype=F32).reshape(bsz, s, main)
    ba = _matmul(xb, _pad_cols(w_in[:, main:], LANES).astype(BF16), tm=1024, tn=LANES, out_dtype=F32)
    y_sc = _short_conv(proj, sc_conv_w, sc_w, tc=256)
    y_dn = _deltanet(proj, 3 * sc_w, 3 * sc_w + 3 * dn_w, dn_conv_w, ba.reshape(bsz, s, LANES), a_log, dt_bias,
                     norm_w, n_heads)
    wo = w_out.astype(BF16)
    return _matmul2(y_sc.reshape(bsz * s, sc_w), y_dn.reshape(bsz * s, dn_w), wo[:sc_w], wo[sc_w:], tm=1024, tn=1024)


def _nsa_mixer(xb, bsz, s, positions, w_in, cmp_pos_k, cmp_w1_k, cmp_w2_k, cmp_pos_v, cmp_w1_v, cmp_w2_v, w_out):
    qw = NSA_HEADS * HEAD_DIM
    kvw = NSA_KV_HEADS * HEAD_DIM
    main = qw + 6 * kvw
    proj = _matmul(xb, w_in[:, :main].astype(BF16), tm=1024, tn=1024, out_dtype=F32).reshape(bsz, s, main)
    gates = _matmul(xb, _pad_cols(w_in[:, main:], LANES).astype(BF16), tm=1024, tn=LANES, out_dtype=F32)
    half = HEAD_DIM // 2
    inv = jnp.power(ROPE_THETA, -jnp.arange(half, dtype=F32) / half)
    inv = jnp.concatenate([inv, inv])
    ang = positions.astype(F32)[..., None] * inv
    cmp_end = jnp.minimum(jnp.arange(s // CMP_STRIDE) * CMP_STRIDE + CMP_BLOCK - 1, s - 1)
    ang_cmp = positions[:, cmp_end].astype(F32)[..., None] * inv
    q_r, ks, kw, vs, vw = _rope_qkv(proj, ang, ts=512)
    k_cmp = _compress(proj, qw, cmp_pos_k, cmp_w1_k, cmp_w2_k, ang_cmp, rope=True)
    v_cmp = _compress(proj, qw + kvw, cmp_pos_v, cmp_w1_v, cmp_w2_v, ang_cmp, rope=False)
    o_cmp, sel = _cmp_attention(q_r, k_cmp, v_cmp, tq=256)
    o = _slc_win_attention(q_r, ks, vs, kw, vw, sel, o_cmp, gates.reshape(bsz, s, LANES), tq=128, tk=256)
    return _matmul(o.reshape(bsz * s, qw), w_out.astype(BF16), tm=1024, tn=1024, out_dtype=F32)


def _ffn(xb, w_in, w_out):
    hmid = _ffn_in(xb, w_in.astype(BF16), tm=1024, tn=512)
    return _matmul(hmid, w_out.astype(BF16), tm=1024, tn=512, out_dtype=F32)


def kernel(x, positions, ln_mix_g, ln_mix_b, ln_ffn_g, ln_ffn_b, ffn_w_in, ffn_w_out, hy_w_in, sc_conv_w, dn_conv_w, dn_a_log, dn_dt_bias, dn_norm_w, hy_w_out, nsa_w_in, cmp_pos_k, cmp_w1_k, cmp_w2_k, cmp_pos_v, cmp_w1_v, cmp_w2_v, nsa_w_out):
    bsz, s, d = x.shape
    xf = x.reshape(bsz * s, d)
    xb = xf
    for i in range(DEPTH):
        j = i // 2
        if i % 2 == 0:
            y = _conv_deltanet_mixer(xb, bsz, s, hy_w_in[j], sc_conv_w[j], dn_conv_w[j], dn_a_log[j],
                                     dn_dt_bias[j], dn_norm_w[j], hy_w_out[j])
        else:
            y = _nsa_mixer(xb, bsz, s, positions, nsa_w_in[j], cmp_pos_k[j], cmp_w1_k[j], cmp_w2_k[j],
                           cmp_pos_v[j], cmp_w1_v[j], cmp_w2_v[j], nsa_w_out[j])
        xf, xb = _add_ln(xf, y, ln_mix_g[i], ln_mix_b[i], tm=512)
        xf, xb = _add_ln(xf, _ffn(xb, ffn_w_in[i], ffn_w_out[i]), ln_ffn_g[i], ln_ffn_b[i], tm=512)
    return xf.reshape(bsz, s, d)
```

```python
import functools
import math

import jax
import jax.numpy as jnp
from jax import lax
from jax.experimental import pallas as pl
from jax.experimental.pallas import tpu as pltpu

F32 = jnp.float32
BF16 = jnp.bfloat16
HIGHEST = lax.Precision.HIGHEST

LANES = 128
VMEM_LIMIT = 48 * 1024 * 1024

DN_HEADS = 8
DN_CHUNK = 64
DN_CONV = 4
SC_KERNEL = 3
NSA_HEADS = 16
NSA_KV_HEADS = 4
HPG = NSA_HEADS // NSA_KV_HEADS
HEAD_DIM = 128
CMP_BLOCK = 32
CMP_STRIDE = 16
SEL_BLOCK = 64
N_SELECT = 16
WINDOW = 512
ROPE_THETA = 10000.0
LN_EPS = 1e-5
NORM_EPS = 1e-6
NEG_INF = -1e30
DEPTH = 2
ALPHA = (2 * DEPTH) ** 0.25
ATTN_SCALE = HEAD_DIM ** -0.5
Q_SCALE = ATTN_SCALE * math.log2(math.e)


def _params(*sem):
    return pltpu.CompilerParams(dimension_semantics=sem, vmem_limit_bytes=VMEM_LIMIT)


def _sigmoid(x):
    return 1.0 / (1.0 + jnp.exp(-x))


def _silu(x):
    return x * _sigmoid(x)


def _dot(a, b):
    return jnp.dot(a, b, preferred_element_type=F32)


def _dot_nt(a, b):
    return lax.dot_general(a, b, (((1,), (1,)), ((), ())), preferred_element_type=F32)


def _dot_tn(a, b):
    return lax.dot_general(a, b, (((0,), (0,)), ((), ())), preferred_element_type=F32)


def _dot_hi(a, b):
    return jnp.dot(a, b, precision=HIGHEST, preferred_element_type=F32)


def _mm_kernel(x_ref, w_ref, o_ref):
    o_ref[...] = _dot(x_ref[...].astype(BF16), w_ref[...]).astype(o_ref.dtype)


def _matmul(x, w, *, tm, tn, out_dtype):
    m, k = x.shape
    n = w.shape[1]
    return pl.pallas_call(
        _mm_kernel,
        grid=(m // tm, n // tn),
        in_specs=[pl.BlockSpec((tm, k), lambda i, j: (i, 0)),
                  pl.BlockSpec((k, tn), lambda i, j: (0, j))],
        out_specs=pl.BlockSpec((tm, tn), lambda i, j: (i, j)),
        out_shape=jax.ShapeDtypeStruct((m, n), out_dtype),
        compiler_params=_params("parallel", "parallel"),
        name="matmul",
    )(x, w)


def _mm2_kernel(x1_ref, x2_ref, w1_ref, w2_ref, o_ref):
    acc = _dot(x1_ref[...], w1_ref[...]) + _dot(x2_ref[...], w2_ref[...])
    o_ref[...] = acc.astype(o_ref.dtype)


def _matmul2(x1, x2, w1, w2, *, tm, tn):
    m, k1 = x1.shape
    k2 = x2.shape[1]
    n = w1.shape[1]
    return pl.pallas_call(
        _mm2_kernel,
        grid=(m // tm, n // tn),
        in_specs=[pl.BlockSpec((tm, k1), lambda i, j: (i, 0)),
                  pl.BlockSpec((tm, k2), lambda i, j: (i, 0)),
                  pl.BlockSpec((k1, tn), lambda i, j: (0, j)),
                  pl.BlockSpec((k2, tn), lambda i, j: (0, j))],
        out_specs=pl.BlockSpec((tm, tn), lambda i, j: (i, j)),
        out_shape=jax.ShapeDtypeStruct((m, n), F32),
        compiler_params=_params("parallel", "parallel"),
        name="matmul2",
    )(x1, x2, w1, w2)


def _ffn_in_kernel(x_ref, wg_ref, wu_ref, o_ref):
    x = x_ref[...]
    gate = _dot(x, wg_ref[...])
    up = _dot(x, wu_ref[...])
    o_ref[...] = (_silu(gate) * up).astype(o_ref.dtype)


def _ffn_in(xb, w_in, *, tm, tn):
    m, k = xb.shape
    hidden = w_in.shape[1] // 2
    nj = hidden // tn
    return pl.pallas_call(
        _ffn_in_kernel,
        grid=(m // tm, nj),
        in_specs=[pl.BlockSpec((tm, k), lambda i, j: (i, 0)),
                  pl.BlockSpec((k, tn), lambda i, j: (0, j)),
                  pl.BlockSpec((k, tn), lambda i, j: (0, j + nj))],
        out_specs=pl.BlockSpec((tm, tn), lambda i, j: (i, j)),
        out_shape=jax.ShapeDtypeStruct((m, hidden), BF16),
        compiler_params=_params("parallel", "parallel"),
        name="ffn_in",
    )(xb, w_in, w_in)


def _add_ln_kernel(x_ref, y_ref, g_ref, b_ref, o_ref, ob_ref):
    v = ALPHA * x_ref[...] + y_ref[...]
    mu = jnp.mean(v, axis=-1, keepdims=True)
    d = v - mu
    var = jnp.mean(d * d, axis=-1, keepdims=True)
    out = d * lax.rsqrt(var + LN_EPS) * g_ref[...] + b_ref[...]
    o_ref[...] = out
    ob_ref[...] = out.astype(BF16)


def _add_ln(x, y, g, b, *, tm):
    m, d = x.shape
    return pl.pallas_call(
        _add_ln_kernel,
        grid=(m // tm,),
        in_specs=[pl.BlockSpec((tm, d), lambda i: (i, 0)),
                  pl.BlockSpec((tm, d), lambda i: (i, 0)),
                  pl.BlockSpec((1, d), lambda i: (0, 0)),
                  pl.BlockSpec((1, d), lambda i: (0, 0))],
        out_specs=[pl.BlockSpec((tm, d), lambda i: (i, 0)),
                   pl.BlockSpec((tm, d), lambda i: (i, 0))],
        out_shape=[jax.ShapeDtypeStruct((m, d), F32), jax.ShapeDtypeStruct((m, d), BF16)],
        compiler_params=_params("parallel"),
        name="add_ln",
    )(x, y, g.reshape(1, d), b.reshape(1, d))


def _causal_conv(u, w_ref, taps):
    row = lax.broadcasted_iota(jnp.int32, u.shape, 0)
    acc = u * w_ref[taps - 1:taps, :]
    for sh in range(1, taps):
        shifted = jnp.where(row >= sh, pltpu.roll(u, sh, axis=0), 0.0)
        acc = acc + shifted * w_ref[taps - 1 - sh:taps - sh, :]
    return acc


def _sc_kernel(b_ref, c_ref, h_ref, w_ref, o_ref):
    u = c_ref[0] * h_ref[0]
    o_ref[0] = (b_ref[0] * _causal_conv(u, w_ref, SC_KERNEL)).astype(o_ref.dtype)


def _short_conv(proj, conv_w, width, *, tc):
    bsz, s, _ = proj.shape
    nb = width // tc
    w = jnp.zeros((8, width), F32).at[:SC_KERNEL].set(conv_w)
    return pl.pallas_call(
        _sc_kernel,
        grid=(bsz, nb),
        in_specs=[pl.BlockSpec((1, s, tc), lambda b, j: (b, 0, j)),
                  pl.BlockSpec((1, s, tc), lambda b, j: (b, 0, j + nb)),
                  pl.BlockSpec((1, s, tc), lambda b, j: (b, 0, j + 2 * nb)),
                  pl.BlockSpec((8, tc), lambda b, j: (0, j))],
        out_specs=pl.BlockSpec((1, s, tc), lambda b, j: (b, 0, j)),
        out_shape=jax.ShapeDtypeStruct((bsz, s, width), BF16),
        compiler_params=_params("parallel", "parallel"),
        name="short_conv",
    )(proj, proj, proj, w)


def _split(x):
    hi = x.astype(BF16)
    return hi, (x - hi.astype(F32)).astype(BF16)


def _dot3(a, b):
    return _dot(a[0], b[0]) + (_dot(a[0], b[1]) + _dot(a[1], b[0]))


DN_GROUP = 8


def _dn_group_local(base, scr, masks, out):
    q_s, k_s, kb_s, qd_s, kf_s, kbe_s, vb_s, gc_s, gct_s = scr
    incl, strict, m8, m16, eye = masks
    c = DN_CHUNK
    idx = range(DN_GROUP)
    rows = [pl.ds(base + cc * c, c) for cc in idx]
    gc = [gc_s[r, :] for r in rows]
    decay = []
    for cc in idx:
        gc_j = gct_s[pl.ds(base + (cc // 2) * LANES, c), (cc % 2) * c:(cc % 2) * c + c]
        decay.append(jnp.where(incl, jnp.exp(jnp.where(incl, gc[cc][:, :c] - gc_j, 0.0)), 0.0))
    kbf = [k_s[r, :] for r in rows]
    kk = [_dot_nt(kb_s[rows[cc], :], kbf[cc]) for cc in idx]
    qk = [_dot_nt(q_s[rows[cc], :], kbf[cc]) for cc in idx]
    yield
    a = [jnp.where(strict, kk[cc] * decay[cc], 0.0) for cc in idx]
    intra = [(qk[cc] * decay[cc]).astype(BF16) for cc in idx]
    ad = [jnp.where(m8, x, 0.0) for x in a]
    ads = [_split(x) for x in ad]
    a2s = [_split(_dot3(x, x)) for x in ads]
    yield
    p = [eye - x for x in ad]
    p1, a4s = [], []
    for cc in idx:
        p1.append(p[cc] + _dot3(_split(p[cc]), a2s[cc]))
        a4s.append(_split(_dot3(a2s[cc], a2s[cc])))
    yield
    p2 = [p1[cc] + _dot3(_split(p1[cc]), a4s[cc]) for cc in idx]
    yield
    ps = [_split(x) for x in p2]
    t = [_split(_dot3(ps[cc], _split(jnp.where(m16, a[cc] - ad[cc], 0.0)))) for cc in idx]
    yield
    ds = [_split(p2[cc] - _dot3(t[cc], ps[cc])) for cc in idx]
    yield
    db, das = [], []
    for cc in idx:
        rhs = jnp.concatenate([vb_s[rows[cc], :], kbe_s[rows[cc], :]], axis=1)
        db.append(_dot3(ds[cc], _split(rhs)))
        das.append(_split(_dot3(ds[cc], _split(jnp.where(m16, 0.0, a[cc])))))
    yield
    blocks = [[x[0:16]] for x in db]
    for s4 in range(1, c // 16):
        rs = slice(16 * s4, 16 * s4 + 16)
        for cc in idx:
            xprev = jnp.concatenate(blocks[cc] + [jnp.zeros((c - 16 * s4, 2 * HEAD_DIM), F32)], axis=0)
            blocks[cc].append(db[cc][rs] - _dot3((das[cc][0][rs], das[cc][1][rs]), _split(xprev)))
        yield
    for cc in idx:
        sol = jnp.concatenate(blocks[cc], axis=0)
        sol_hi, sol_lo = _split(sol)
        g_last = gc[cc][c - 1:c, :]
        k_dec_t = (kf_s[rows[cc], :] * jnp.exp(g_last - gc[cc])).T.astype(BF16)
        kw = _dot(k_dec_t, sol_hi) + _dot(k_dec_t, sol_lo)
        iw = _dot(intra[cc], sol_hi) + _dot(intra[cc], sol_lo)
        out.append((kw[:, HEAD_DIM:].astype(BF16), kw[:, :HEAD_DIM],
                    (qd_s[rows[cc], :] - iw[:, HEAD_DIM:]).astype(BF16), iw[:, :HEAD_DIM], jnp.exp(g_last)))
    yield


def _dn_chunk_seq(state, loc, z, nw):
    w2, n_mat, qp, op, eg_last = loc
    sb = state.astype(BF16)
    o = _dot(qp, sb) + op
    state = (state * eg_last - _dot(w2, sb)) + n_mat
    o = o * lax.rsqrt(jnp.mean(o * o, axis=-1, keepdims=True) + NORM_EPS) * nw * _silu(z)
    return state, o


def _dn_kernel(qp_ref, kp_ref, vp_ref, z_ref, ba_ref, arow_ref, dtrow_ref, cwq_ref, cwk_ref, cwv_ref, nw_ref, o_ref,
               q_s, k_s, kb_s, qd_s, kf_s, kbe_s, vb_s, gc_s, gct_s, *, n_heads):
    h = pl.program_id(1)
    s = qp_ref.shape[1]
    c = DN_CHUNK
    ba = ba_ref[0]
    lane = lax.broadcasted_iota(jnp.int32, ba.shape, 1)
    xa = ba + dtrow_ref[...]
    softplus = jnp.maximum(xa, 0.0) + jnp.log(1.0 + jnp.exp(-jnp.abs(xa)))
    g_full = -jnp.exp(arow_ref[...]) * softplus
    beta = jnp.sum(jnp.where(lane == h, _sigmoid(ba), 0.0), axis=1, keepdims=True)
    g_col = jnp.sum(jnp.where(lane == h + n_heads, g_full, 0.0), axis=1, keepdims=True)
    gc = jnp.broadcast_to(g_col, (s, LANES))
    pos = lax.broadcasted_iota(jnp.int32, (s, LANES), 0) & (c - 1)
    sh = 1
    while sh < c:
        gc = gc + jnp.where(pos >= sh, pltpu.roll(gc, sh, axis=0), 0.0)
        sh *= 2
    gc_s[...] = gc
    for blk in range(s // LANES):
        rs = slice(blk * LANES, (blk + 1) * LANES)
        gct_s[rs, :] = gc[rs, :].T
    eg = jnp.exp(gc)
    q = _silu(_causal_conv(qp_ref[0], cwq_ref, DN_CONV))
    q = q * (lax.rsqrt(jnp.sum(q * q, axis=-1, keepdims=True) + NORM_EPS) * (HEAD_DIM ** -0.5))
    q_s[...] = q.astype(BF16)
    qd_s[...] = q * eg
    k = _silu(_causal_conv(kp_ref[0], cwk_ref, DN_CONV))
    k = k * lax.rsqrt(jnp.sum(k * k, axis=-1, keepdims=True) + NORM_EPS)
    kb = k * beta
    kf_s[...] = k
    k_s[...] = k.astype(BF16)
    kb_s[...] = kb.astype(BF16)
    kbe_s[...] = kb * eg
    vb_s[...] = _silu(_causal_conv(vp_ref[0], cwv_ref, DN_CONV)) * beta

    row = lax.broadcasted_iota(jnp.int32, (c, c), 0)
    col = lax.broadcasted_iota(jnp.int32, (c, c), 1)
    masks = (row >= col, row > col, (row >> 3) == (col >> 3), (row >> 4) == (col >> 4), (row == col).astype(F32))
    scr = (q_s, k_s, kb_s, qd_s, kf_s, kbe_s, vb_s, gc_s, gct_s)
    nw = nw_ref[...]
    rows_per_group = DN_GROUP * c
    n_groups = s // rows_per_group

    def group_base(gi):
        base = gi * rows_per_group
        return base if isinstance(base, int) else pl.multiple_of(base, rows_per_group)

    def run(gi_local, gi_seq, state, locs):
        nxt = []
        stages = iter(()) if gi_local is None else _dn_group_local(group_base(gi_local), scr, masks, nxt)
        todo = list(range(DN_GROUP)) if gi_seq is not None else []
        done = False
        while todo or not done:
            if not done:
                done = next(stages, "end") == "end"
            if todo:
                cc = todo.pop(0)
                rows = pl.ds(group_base(gi_seq) + cc * c, c)
                state, o = _dn_chunk_seq(state, locs[cc], z_ref[0, rows, :], nw)
                o_ref[0, rows, :] = o.astype(o_ref.dtype)
        return state, tuple(nxt)

    def body(gi, carry):
        return run(gi + 1, gi, *carry)

    carry = run(0, None, jnp.zeros((HEAD_DIM, HEAD_DIM), F32), None)
    carry = lax.fori_loop(0, n_groups - 1, body, carry)
    run(None, n_groups - 1, *carry)


def _deltanet(proj, qkv_col0, z_col0, conv_w, ba, a_log, dt_bias, norm_w, n_heads):
    bsz, s, _ = proj.shape
    qb0 = qkv_col0 // HEAD_DIM
    zb0 = z_col0 // HEAD_DIM
    arow = jnp.zeros((1, LANES), F32).at[0, n_heads:2 * n_heads].set(a_log)
    dtrow = jnp.zeros((1, LANES), F32).at[0, n_heads:2 * n_heads].set(dt_bias)
    cw = jnp.zeros((8, 3 * n_heads * HEAD_DIM), F32).at[:DN_CONV].set(conv_w)
    blk = (1, s, HEAD_DIM)
    col_spec = lambda off: pl.BlockSpec(blk, lambda b, h, off=off: (b, 0, h + off))
    cw_spec = lambda off: pl.BlockSpec((8, HEAD_DIM), lambda b, h, off=off: (0, h + off))
    row_spec = pl.BlockSpec((1, LANES), lambda b, h: (0, 0))
    return pl.pallas_call(
        functools.partial(_dn_kernel, n_heads=n_heads),
        grid=(bsz, n_heads),
        in_specs=[col_spec(qb0), col_spec(qb0 + n_heads), col_spec(qb0 + 2 * n_heads), col_spec(zb0),
                  pl.BlockSpec((1, s, LANES), lambda b, h: (b, 0, 0)), row_spec, row_spec,
                  cw_spec(0), cw_spec(n_heads), cw_spec(2 * n_heads), row_spec],
        out_specs=pl.BlockSpec(blk, lambda b, h: (b, 0, h)),
        out_shape=jax.ShapeDtypeStruct((bsz, s, n_heads * HEAD_DIM), BF16),
        scratch_shapes=[pltpu.VMEM((s, HEAD_DIM), BF16)] * 3 + [pltpu.VMEM((s, HEAD_DIM), F32)] * 6,
        compiler_params=_params("parallel", "arbitrary"),
        name="deltanet",
    )(proj, proj, proj, proj, ba, arow, dtrow, cw, cw, cw, norm_w.reshape(1, HEAD_DIM))


def _rope_tables(ang):
    lane = lax.broadcasted_iota(jnp.int32, ang.shape, 1)
    sin = jnp.sin(ang)
    return jnp.cos(ang), jnp.where(lane < HEAD_DIM // 2, -sin, sin)


def _rope(x, cos, sin_signed):
    return x * cos + pltpu.roll(x, HEAD_DIM // 2, axis=1) * sin_signed


def _rope_kernel(ang_ref, q_ref, ks_ref, kw_ref, vs_ref, vw_ref, qo_ref, kso_ref, kwo_ref, vso_ref, vwo_ref):
    cos, sin = _rope_tables(ang_ref[0])
    for hh in range(NSA_HEADS):
        sl = slice(hh * HEAD_DIM, (hh + 1) * HEAD_DIM)
        qo_ref[0, :, sl] = (_rope(q_ref[0, :, sl], cos, sin) * Q_SCALE).astype(BF16)
    for g in range(NSA_KV_HEADS):
        sl = slice(g * HEAD_DIM, (g + 1) * HEAD_DIM)
        kso_ref[0, :, sl] = _rope(ks_ref[0, :, sl], cos, sin).astype(BF16)
        kwo_ref[0, :, sl] = _rope(kw_ref[0, :, sl], cos, sin).astype(BF16)
    ts = vs_ref.shape[1]
    for v_ref, vo_ref in ((vs_ref, vso_ref), (vw_ref, vwo_ref)):
        ck = vo_ref.shape[4]
        for g in range(NSA_KV_HEADS):
            for cc in range(ts // ck):
                parts = [v_ref[0, cc * ck + r:cc * ck + r + LANES, g * HEAD_DIM:(g + 1) * HEAD_DIM].T
                         for r in range(0, ck, LANES)]
                vo_ref[0, g, cc] = jnp.concatenate(parts, axis=1).astype(BF16)


def _rope_qkv(proj, ang, *, ts, slc_chunk, win_chunk):
    bsz, s, _ = proj.shape
    qw = NSA_HEADS * HEAD_DIM
    kvw = NSA_KV_HEADS * HEAD_DIM
    kv_spec = lambda blk: pl.BlockSpec((1, ts, kvw), lambda b, i, blk=blk: (b, i, blk))
    kv_out = pl.BlockSpec((1, ts, kvw), lambda b, i: (b, i, 0))
    kv_shape = jax.ShapeDtypeStruct((bsz, s, kvw), BF16)
    vt_out = lambda ck: pl.BlockSpec((1, NSA_KV_HEADS, ts // ck, HEAD_DIM, ck), lambda b, i: (b, 0, i, 0, 0))
    vt_shape = lambda ck: jax.ShapeDtypeStruct((bsz, NSA_KV_HEADS, s // ck, HEAD_DIM, ck), BF16)
    base = qw // kvw
    return pl.pallas_call(
        _rope_kernel,
        grid=(bsz, s // ts),
        in_specs=[pl.BlockSpec((1, ts, HEAD_DIM), lambda b, i: (b, i, 0)),
                  pl.BlockSpec((1, ts, qw), lambda b, i: (b, i, 0)),
                  kv_spec(base + 2), kv_spec(base + 4), kv_spec(base + 3), kv_spec(base + 5)],
        out_specs=[pl.BlockSpec((1, ts, qw), lambda b, i: (b, i, 0)), kv_out, kv_out,
                   vt_out(slc_chunk), vt_out(win_chunk)],
        out_shape=[jax.ShapeDtypeStruct((bsz, s, qw), BF16), kv_shape, kv_shape,
                   vt_shape(slc_chunk), vt_shape(win_chunk)],
        compiler_params=_params("parallel", "parallel"),
        name="rope_qkv",
    )(ang, proj, proj, proj, proj, proj)


def _gelu_tanh(x):
    return x * (0.5 * (1.0 + jnp.tanh(math.sqrt(2.0 / math.pi) * (x + 0.044715 * (x * x * x)))))


def _compress_kernel(x_ref, w1_ref, w2_ref, pos_ref, ang_ref, o_ref, *, rope):
    nsub = x_ref.shape[1] // CMP_STRIDE
    hid = w1_ref.shape[1]
    pa = jnp.zeros((nsub, hid), F32)
    pb = jnp.zeros((nsub, hid), F32)
    for l in range(CMP_STRIDE):
        xl = x_ref[0, pl.ds(l, nsub, stride=CMP_STRIDE), :].astype(BF16)
        pa = pa + _dot(xl, w1_ref[l * HEAD_DIM:(l + 1) * HEAD_DIM, :])
        pb = pb + _dot(xl, w1_ref[(CMP_STRIDE + l) * HEAD_DIM:(CMP_STRIDE + l + 1) * HEAD_DIM, :])
    bias = _dot(pos_ref[...], w1_ref[...])[0:1, :]
    hpre = pa + pltpu.roll(pb, nsub - 1, axis=0) + bias
    out = _dot(_gelu_tanh(hpre).astype(BF16), w2_ref[...])
    if rope:
        cos, sin = _rope_tables(ang_ref[0])
        out = _rope(out, cos, sin)
    o_ref[0, 0] = out.astype(o_ref.dtype)


def _compress(proj, col0, pos_emb, w1, w2, ang_cmp, *, rope):
    bsz, s, _ = proj.shape
    nsub = s // CMP_STRIDE
    blk0 = col0 // HEAD_DIM
    hid = w1.shape[1]
    pos = jnp.zeros((8, CMP_BLOCK * HEAD_DIM), BF16).at[0].set(pos_emb.reshape(-1).astype(BF16))
    return pl.pallas_call(
        functools.partial(_compress_kernel, rope=rope),
        grid=(bsz, NSA_KV_HEADS),
        in_specs=[pl.BlockSpec((1, s, HEAD_DIM), lambda b, g: (b, 0, g + blk0)),
                  pl.BlockSpec((CMP_BLOCK * HEAD_DIM, hid), lambda b, g: (0, 0)),
                  pl.BlockSpec((hid, HEAD_DIM), lambda b, g: (0, 0)),
                  pl.BlockSpec((8, CMP_BLOCK * HEAD_DIM), lambda b, g: (0, 0)),
                  pl.BlockSpec((1, nsub, HEAD_DIM), lambda b, g: (b, 0, 0))],
        out_specs=pl.BlockSpec((1, 1, nsub, HEAD_DIM), lambda b, g: (b, g, 0, 0)),
        out_shape=jax.ShapeDtypeStruct((bsz, NSA_KV_HEADS, nsub, HEAD_DIM), BF16),
        compiler_params=_params("parallel", "parallel"),
        name="compress",
    )(proj, w1.astype(BF16), w2.astype(BF16), pos, ang_cmp)


def _cmp_attn_kernel(q_ref, kc_ref, vc_ref, smat_ref, o_ref, sel_ref):
    tq = q_ref.shape[1]
    ncol = kc_ref.shape[2]
    t = pl.program_id(2) * tq + lax.broadcasted_iota(jnp.int32, (tq, ncol), 0)
    n = lax.broadcasted_iota(jnp.int32, (tq, ncol), 1)
    valid = (n * CMP_STRIDE + CMP_BLOCK - 1) <= t
    kc = kc_ref[0, 0]
    vc = vc_ref[0, 0]
    p_grp = jnp.zeros((tq, ncol), F32)
    for hh in range(HPG):
        sl = slice(hh * HEAD_DIM, (hh + 1) * HEAD_DIM)
        sc = jnp.where(valid, _dot_nt(q_ref[0, :, sl], kc), NEG_INF)
        e = jnp.exp2(sc - jnp.max(sc, axis=-1, keepdims=True))
        p = jnp.where(valid, e / jnp.sum(e, axis=-1, keepdims=True), 0.0)
        o_ref[0, :, sl] = _dot(p.astype(BF16), vc)
        p_grp = p_grp + p
    score = _dot_hi(p_grp, smat_ref[...])
    n_sel = sel_ref.shape[2]
    score = jnp.concatenate([score[r:r + LANES].T for r in range(0, tq, LANES)], axis=1)[:n_sel]
    n = lax.broadcasted_iota(jnp.int32, (n_sel, tq), 0)
    t = pl.program_id(2) * tq + lax.broadcasted_iota(jnp.int32, (n_sel, tq), 1)
    cur = t >> int(math.log2(SEL_BLOCK))
    forced = (n == 0) | (n == cur) | (n == cur - 1)
    future = n * SEL_BLOCK > t
    score = jnp.where(forced, jnp.inf, jnp.where(future, -jnp.inf, score))
    rank = jnp.zeros((n_sel, tq), jnp.int32)
    for kk in range(n_sel):
        ck = score[kk:kk + 1, :]
        ahead = (ck > score) | ((ck == score) & (kk < n))
        rank = rank + ahead.astype(jnp.int32)
    sel_ref[0, 0] = (rank < N_SELECT).astype(sel_ref.dtype)


def _sel_matrix(ncol, n_sel):
    rs = SEL_BLOCK // CMP_STRIDE
    rc = CMP_BLOCK // CMP_STRIDE
    mat = [[0.0] * ncol for _ in range(ncol)]
    for j in range(n_sel):
        for m in range(rs):
            for n in range(rc):
                i = rs * j + m + n - (rc - 1)
                if 0 <= i < ncol - 1:
                    mat[i][j] += 1.0
    return jnp.array(mat, F32)


def _cmp_attention(q_r, k_cmp, v_cmp, *, tq):
    bsz, s, _ = q_r.shape
    ncol = k_cmp.shape[2]
    n_sel = s // SEL_BLOCK
    gw = HPG * HEAD_DIM
    return pl.pallas_call(
        _cmp_attn_kernel,
        grid=(bsz, NSA_KV_HEADS, s // tq),
        in_specs=[pl.BlockSpec((1, tq, gw), lambda b, g, i: (b, i, g)),
                  pl.BlockSpec((1, 1, ncol, HEAD_DIM), lambda b, g, i: (b, g, 0, 0)),
                  pl.BlockSpec((1, 1, ncol, HEAD_DIM), lambda b, g, i: (b, g, 0, 0)),
                  pl.BlockSpec((ncol, ncol), lambda b, g, i: (0, 0))],
        out_specs=[pl.BlockSpec((1, tq, gw), lambda b, g, i: (b, i, g)),
                   pl.BlockSpec((1, 1, n_sel, tq), lambda b, g, i: (b, g, 0, i))],
        out_shape=[jax.ShapeDtypeStruct((bsz, s, NSA_HEADS * HEAD_DIM), F32),
                   jax.ShapeDtypeStruct((bsz, NSA_KV_HEADS, n_sel, s), F32)],
        compiler_params=_params("parallel", "parallel", "parallel"),
        name="cmp_attention",
    )(q_r, k_cmp, v_cmp, _sel_matrix(ncol, n_sel))


def _softmax_pv_t(scores, values_t):
    m = None
    for sc in scores:
        cm = jnp.max(sc, axis=0, keepdims=True)
        m = cm if m is None else jnp.maximum(m, cm)
    l = None
    acc = None
    for sc, v_t in zip(scores, values_t):
        p = jnp.exp2(sc - m)
        ps = jnp.sum(p, axis=0, keepdims=True)
        pv = _dot(v_t, p.astype(BF16))
        l = ps if l is None else l + ps
        acc = pv if acc is None else acc + pv
    return acc / l


MASK_BIG = 2.0 ** 100
SLC_VARIANT_CHUNKS = 2


def _slc_win_kernel(q_ref, ks_ref, vst_ref, kw_ref, vwt_ref, selt_ref, blk_ref, oc_ref, gate_ref, o_ref,
                    gt_s, os_s):
    tq = q_ref.shape[1]
    tk = vst_ref.shape[4]
    n_chunks = vst_ref.shape[2]
    n_sel = selt_ref.shape[2]
    g = pl.program_id(1)
    t0 = pl.program_id(2) * tq
    q_t = jnp.concatenate([q_ref[0, :, hh * HEAD_DIM:(hh + 1) * HEAD_DIM].astype(F32).T.astype(BF16)
                           for hh in range(HPG)], axis=1)

    key_loc = lax.broadcasted_iota(jnp.int32, (tq, tq), 0)
    qry_loc = lax.broadcasted_iota(jnp.int32, (tq, tq), 1)
    n_win = WINDOW // tq + 1
    scores, values = [], []
    for cc in range(n_win):
        k0 = t0 - WINDOW + cc * tq
        k0c = pl.multiple_of(jnp.maximum(k0, 0), tq)
        sc = _dot(kw_ref[0, pl.ds(k0c, tq), :], q_t)
        if cc == 0:
            ok = jnp.where(k0 >= 0, key_loc - qry_loc, 0) > 0
            sc = jnp.where(jnp.concatenate([ok] * HPG, axis=1), sc, NEG_INF)
        elif cc == n_win - 1:
            sc = jnp.where(jnp.concatenate([key_loc <= qry_loc] * HPG, axis=1), sc, NEG_INF)
        else:
            sc = sc + jnp.where(k0 >= 0, 0.0, NEG_INF)
        scores.append(sc)
        values.append(vwt_ref[0, 0, k0c // tq])
    o_win_t = _softmax_pv_t(scores, values)

    sel_bias = ((selt_ref[0, 0] - 1.0) * MASK_BIG).astype(BF16)
    q_aug = jnp.concatenate([q_t, jnp.concatenate([sel_bias] * HPG, axis=1),
                             jnp.zeros((HEAD_DIM - n_sel, HPG * tq), BF16)], axis=0)
    k_loc = lax.broadcasted_iota(jnp.int32, (tk, tq), 0)
    t_loc = lax.broadcasted_iota(jnp.int32, (tk, tq), 1)

    def selected(n_used):
        scores, values = [], []
        for cc in range(n_used):
            rows = slice(cc * tk, (cc + 1) * tk)
            k_aug = jnp.concatenate([ks_ref[0, rows, :], blk_ref[rows, :]], axis=1)
            sc = _dot(k_aug, q_aug)
            if cc >= n_used - SLC_VARIANT_CHUNKS:
                causal = (cc * tk + k_loc) <= (t0 + t_loc)
                sc = jnp.where(jnp.concatenate([causal] * HPG, axis=1), sc, NEG_INF)
            scores.append(sc)
            values.append(vst_ref[0, 0, cc])
        os_s[...] = _softmax_pv_t(scores, values)

    variant = t0 // (SLC_VARIANT_CHUNKS * tk)
    for vv in range(n_chunks // SLC_VARIANT_CHUNKS):
        pl.when(variant == vv)(functools.partial(selected, (vv + 1) * SLC_VARIANT_CHUNKS))

    gt = _sigmoid(gate_ref[0])
    gt_s[...] = gt.T
    lane = lax.broadcasted_iota(jnp.int32, gt.shape, 1)
    for hh in range(HPG):
        base = (g * HPG + hh) * 3
        g_cmp = jnp.sum(jnp.where(lane == base, gt, 0.0), axis=1, keepdims=True)
        cs = slice(hh * tq, (hh + 1) * tq)
        mix_t = gt_s[pl.ds(base + 1, 1), :] * os_s[:, cs] + gt_s[pl.ds(base + 2, 1), :] * o_win_t[:, cs]
        sl = slice(hh * HEAD_DIM, (hh + 1) * HEAD_DIM)
        o_ref[0, :, sl] = (g_cmp * oc_ref[0, :, sl] + mix_t.T).astype(o_ref.dtype)


def _slc_win_attention(q_r, ks, vs_t, kw, vw_t, sel_t, o_cmp, gates, *, tq):
    bsz, s, _ = q_r.shape
    n_sel = s // SEL_BLOCK
    gw = HPG * HEAD_DIM
    assert vw_t.shape[4] == tq and tq == LANES and vs_t.shape[2] % SLC_VARIANT_CHUNKS == 0
    kv_spec = pl.BlockSpec((1, s, HEAD_DIM), lambda b, g, i: (b, 0, g))
    vt_spec = lambda a: pl.BlockSpec((1, 1) + a.shape[2:], lambda b, g, i: (b, g, 0, 0, 0))
    block_onehot = (jnp.arange(s)[:, None] // SEL_BLOCK == jnp.arange(HEAD_DIM)[None, :]).astype(BF16)
    return pl.pallas_call(
        _slc_win_kernel,
        grid=(bsz, NSA_KV_HEADS, s // tq),
        in_specs=[pl.BlockSpec((1, tq, gw), lambda b, g, i: (b, i, g)),
                  kv_spec, vt_spec(vs_t), kv_spec, vt_spec(vw_t),
                  pl.BlockSpec((1, 1, n_sel, tq), lambda b, g, i: (b, g, 0, i)),
                  pl.BlockSpec((s, HEAD_DIM), lambda b, g, i: (0, 0)),
                  pl.BlockSpec((1, tq, gw), lambda b, g, i: (b, i, g)),
                  pl.BlockSpec((1, tq, LANES), lambda b, g, i: (b, i, 0))],
        out_specs=pl.BlockSpec((1, tq, gw), lambda b, g, i: (b, i, g)),
        out_shape=jax.ShapeDtypeStruct((bsz, s, NSA_HEADS * HEAD_DIM), BF16),
        scratch_shapes=[pltpu.VMEM((LANES, tq), F32), pltpu.VMEM((HEAD_DIM, HPG * tq), F32)],
        compiler_params=_params("parallel", "parallel", "arbitrary"),
        name="slc_win_attention",
    )(q_r, ks, vs_t, kw, vw_t, sel_t, block_onehot, o_cmp, gates)


def _pad_cols(w, n):
    return jnp.pad(w, ((0, 0), (0, n - w.shape[1])))


def _conv_deltanet_mixer(xb, bsz, s, w_in, sc_conv_w, dn_conv_w, a_log, dt_bias, norm_w, w_out):
    sc_w = sc_conv_w.shape[1]
    dn_w = dn_conv_w.shape[1] // 3
    n_heads = dn_w // HEAD_DIM
    main = 3 * sc_w + 4 * dn_w
    proj = _matmul(xb, w_in[:, :main].astype(BF16), tm=1024, tn=1024, out_dtype=F32).reshape(bsz, s, main)
    ba = _matmul(xb, _pad_cols(w_in[:, main:], LANES).astype(BF16), tm=1024, tn=LANES, out_dtype=F32)
    y_sc = _short_conv(proj, sc_conv_w, sc_w, tc=256)
    y_dn = _deltanet(proj, 3 * sc_w, 3 * sc_w + 3 * dn_w, dn_conv_w, ba.reshape(bsz, s, LANES), a_log, dt_bias,
                     norm_w, n_heads)
    wo = w_out.astype(BF16)
    return _matmul2(y_sc.reshape(bsz * s, sc_w), y_dn.reshape(bsz * s, dn_w), wo[:sc_w], wo[sc_w:], tm=1024, tn=1024)


def _nsa_mixer(xb, bsz, s, positions, w_in, cmp_pos_k, cmp_w1_k, cmp_w2_k, cmp_pos_v, cmp_w1_v, cmp_w2_v, w_out):
    qw = NSA_HEADS * HEAD_DIM
    kvw = NSA_KV_HEADS * HEAD_DIM
    main = qw + 6 * kvw
    proj = _matmul(xb, w_in[:, :main].astype(BF16), tm=1024, tn=1024, out_dtype=F32).reshape(bsz, s, main)
    gates = _matmul(xb, _pad_cols(w_in[:, main:], LANES).astype(BF16), tm=1024, tn=LANES, out_dtype=F32)
    half = HEAD_DIM // 2
    inv = jnp.power(ROPE_THETA, -jnp.arange(half, dtype=F32) / half)
    inv = jnp.concatenate([inv, inv])
    ang = positions.astype(F32)[..., None] * inv
    cmp_end = jnp.minimum(jnp.arange(s // CMP_STRIDE) * CMP_STRIDE + CMP_BLOCK - 1, s - 1)
    ang_cmp = positions[:, cmp_end].astype(F32)[..., None] * inv
    q_r, ks, kw, vs_t, vw_t = _rope_qkv(proj, ang, ts=512, slc_chunk=256, win_chunk=LANES)
    k_cmp = _compress(proj, qw, cmp_pos_k, cmp_w1_k, cmp_w2_k, ang_cmp, rope=True)
    v_cmp = _compress(proj, qw + kvw, cmp_pos_v, cmp_w1_v, cmp_w2_v, ang_cmp, rope=False)
    o_cmp, sel_t = _cmp_attention(q_r, k_cmp, v_cmp, tq=256)
    o = _slc_win_attention(q_r, ks, vs_t, kw, vw_t, sel_t, o_cmp, gates.reshape(bsz, s, LANES), tq=LANES)
    return _matmul(o.reshape(bsz * s, qw), w_out.astype(BF16), tm=1024, tn=1024, out_dtype=F32)


def _ffn(xb, w_in, w_out):
    hmid = _ffn_in(xb, w_in.astype(BF16), tm=1024, tn=512)
    return _matmul(hmid, w_out.astype(BF16), tm=1024, tn=512, out_dtype=F32)


def kernel(x, positions, ln_mix_g, ln_mix_b, ln_ffn_g, ln_ffn_b, ffn_w_in, ffn_w_out, hy_w_in, sc_conv_w, dn_conv_w, dn_a_log, dn_dt_bias, dn_norm_w, hy_w_out, nsa_w_in, cmp_pos_k, cmp_w1_k, cmp_w2_k, cmp_pos_v, cmp_w1_v, cmp_w2_v, nsa_w_out):
    bsz, s, d = x.shape
    xf = x.reshape(bsz * s, d)
    xb = xf
    for i in range(DEPTH):
        j = i // 2
        if i % 2 == 0:
            y = _conv_deltanet_mixer(xb, bsz, s, hy_w_in[j], sc_conv_w[j], dn_conv_w[j], dn_a_log[j],
                                     dn_dt_bias[j], dn_norm_w[j], hy_w_out[j])
        else:
            y = _nsa_mixer(xb, bsz, s, positions, nsa_w_in[j], cmp_pos_k[j], cmp_w1_k[j], cmp_w2_k[j],
                           cmp_pos_v[j], cmp_w1_v[j], cmp_w2_v[j], nsa_w_out[j])
        xf, xb = _add_ln(xf, y, ln_mix_g[i], ln_mix_b[i], tm=512)
        xf, xb = _add_ln(xf, _ffn(xb, ffn_w_in[i], ffn_w_out[i]), ln_ffn_g[i], ln_ffn_b[i], tm=512)
    return xf.reshape(bsz, s, d)
```

```python
import functools
import math

import jax
import jax.numpy as jnp
from jax import lax
from jax.experimental import pallas as pl
from jax.experimental.pallas import tpu as pltpu

F32 = jnp.float32
BF16 = jnp.bfloat16
HIGHEST = lax.Precision.HIGHEST

LANES = 128
VMEM_LIMIT = 48 * 1024 * 1024

DN_HEADS = 8
DN_CHUNK = 64
DN_CONV = 4
SC_KERNEL = 3
NSA_HEADS = 16
NSA_KV_HEADS = 4
HPG = NSA_HEADS // NSA_KV_HEADS
HEAD_DIM = 128
CMP_BLOCK = 32
CMP_STRIDE = 16
SEL_BLOCK = 64
N_SELECT = 16
WINDOW = 512
ROPE_THETA = 10000.0
LN_EPS = 1e-5
NORM_EPS = 1e-6
NEG_INF = -1e30
DEPTH = 2
ALPHA = (2 * DEPTH) ** 0.25
ATTN_SCALE = HEAD_DIM ** -0.5
Q_SCALE = ATTN_SCALE * math.log2(math.e)


def _params(*sem):
    return pltpu.CompilerParams(dimension_semantics=sem, vmem_limit_bytes=VMEM_LIMIT)


def _sigmoid(x):
    return 1.0 / (1.0 + jnp.exp(-x))


def _silu(x):
    return x * _sigmoid(x)


def _dot(a, b):
    return jnp.dot(a, b, preferred_element_type=F32)


def _dot_nt(a, b):
    return lax.dot_general(a, b, (((1,), (1,)), ((), ())), preferred_element_type=F32)


def _dot_tn(a, b):
    return lax.dot_general(a, b, (((0,), (0,)), ((), ())), preferred_element_type=F32)


def _dot_hi(a, b):
    return jnp.dot(a, b, precision=HIGHEST, preferred_element_type=F32)


def _proj_kernel(x_ref, w_ref, ws_ref, o_ref, os_ref):
    xb = x_ref[...].astype(BF16)
    o_ref[...] = _dot(xb, w_ref[...])

    @pl.when(pl.program_id(1) == 0)
    def _():
        os_ref[...] = _dot(xb, ws_ref[...])


def _in_proj(x, w, w_side, *, tm, tn):
    m, k = x.shape
    n = w.shape[1]
    ns = w_side.shape[1]
    return pl.pallas_call(
        _proj_kernel,
        grid=(m // tm, n // tn),
        in_specs=[pl.BlockSpec((tm, k), lambda i, j: (i, 0)),
                  pl.BlockSpec((k, tn), lambda i, j: (0, j)),
                  pl.BlockSpec((k, ns), lambda i, j: (0, 0))],
        out_specs=[pl.BlockSpec((tm, tn), lambda i, j: (i, j)),
                   pl.BlockSpec((tm, ns), lambda i, j: (i, 0))],
        out_shape=[jax.ShapeDtypeStruct((m, n), F32), jax.ShapeDtypeStruct((m, ns), F32)],
        compiler_params=_params("parallel", "arbitrary"),
        name="in_proj",
    )(x, w, w_side)


def _ffn_in_kernel(x_ref, wg_ref, wu_ref, o_ref, wgb_s, wub_s):
    @pl.when(pl.program_id(1) == 0)
    def _():
        wgb_s[...] = wg_ref[...].astype(BF16)
        wub_s[...] = wu_ref[...].astype(BF16)

    x = x_ref[...]
    gate = _dot(x, wgb_s[...])
    up = _dot(x, wub_s[...])
    o_ref[...] = (_silu(gate) * up).astype(o_ref.dtype)


def _ffn_in(xb, w_in, *, tm, tn):
    m, k = xb.shape
    hidden = w_in.shape[1] // 2
    nj = hidden // tn
    return pl.pallas_call(
        _ffn_in_kernel,
        grid=(nj, m // tm),
        in_specs=[pl.BlockSpec((tm, k), lambda j, i: (i, 0)),
                  pl.BlockSpec((k, tn), lambda j, i: (0, j)),
                  pl.BlockSpec((k, tn), lambda j, i: (0, j + nj))],
        out_specs=pl.BlockSpec((tm, tn), lambda j, i: (i, j)),
        out_shape=jax.ShapeDtypeStruct((m, hidden), BF16),
        scratch_shapes=[pltpu.VMEM((k, tn), BF16), pltpu.VMEM((k, tn), BF16)],
        compiler_params=_params("parallel", "arbitrary"),
        name="ffn_in",
    )(xb, w_in, w_in)


def _mm_ln_kernel(*refs, n_pairs, nk):
    xs = refs[:n_pairs]
    ws = refs[n_pairs:2 * n_pairs]
    r_ref, g_ref, b_ref, o_ref, ob_ref = refs[2 * n_pairs:2 * n_pairs + 5]
    part = _dot(xs[0][...], ws[0][...])
    for x_ref, w_ref in zip(xs[1:], ws[1:]):
        part = part + _dot(x_ref[...], w_ref[...])

    def finish(y):
        v = ALPHA * r_ref[...] + y
        mu = jnp.mean(v, axis=-1, keepdims=True)
        d = v - mu
        var = jnp.mean(d * d, axis=-1, keepdims=True)
        out = d * lax.rsqrt(var + LN_EPS) * g_ref[...] + b_ref[...]
        o_ref[...] = out
        ob_ref[...] = out.astype(BF16)

    if nk == 1:
        finish(part)
    else:
        acc_ref = refs[-1]
        kk = pl.program_id(1)

        @pl.when(kk == 0)
        def _():
            acc_ref[...] = part

        @pl.when((kk > 0) & (kk < nk - 1))
        def _():
            acc_ref[...] += part

        @pl.when(kk == nk - 1)
        def _():
            finish(acc_ref[...] + part)


def _matmul_ln(xs, ws, resid, g, b, *, tm, nk):
    m, d = resid.shape
    n_pairs = len(xs)
    tks = [x.shape[1] // nk for x in xs]
    in_specs = ([pl.BlockSpec((tm, tk), lambda i, kk: (i, kk)) for tk in tks]
                + [pl.BlockSpec((tk, d), lambda i, kk: (kk, 0)) for tk in tks]
                + [pl.BlockSpec((tm, d), lambda i, kk: (i, 0)),
                   pl.BlockSpec((1, d), lambda i, kk: (0, 0)),
                   pl.BlockSpec((1, d), lambda i, kk: (0, 0))])
    return pl.pallas_call(
        functools.partial(_mm_ln_kernel, n_pairs=n_pairs, nk=nk),
        grid=(m // tm, nk),
        in_specs=in_specs,
        out_specs=[pl.BlockSpec((tm, d), lambda i, kk: (i, 0)),
                   pl.BlockSpec((tm, d), lambda i, kk: (i, 0))],
        out_shape=[jax.ShapeDtypeStruct((m, d), F32), jax.ShapeDtypeStruct((m, d), BF16)],
        scratch_shapes=[pltpu.VMEM((tm, d), F32)] if nk > 1 else [],
        compiler_params=_params("parallel", "arbitrary"),
        name="matmul_ln",
    )(*xs, *ws, resid, g.reshape(1, d), b.reshape(1, d))


def _causal_conv(u, w_ref, taps):
    row = lax.broadcasted_iota(jnp.int32, u.shape, 0)
    acc = u * w_ref[taps - 1:taps, :]
    for sh in range(1, taps):
        shifted = jnp.where(row >= sh, pltpu.roll(u, sh, axis=0), 0.0)
        acc = acc + shifted * w_ref[taps - 1 - sh:taps - sh, :]
    return acc


def _sc_kernel(b_ref, c_ref, h_ref, w_ref, o_ref):
    u = c_ref[0] * h_ref[0]
    o_ref[0] = (b_ref[0] * _causal_conv(u, w_ref, SC_KERNEL)).astype(o_ref.dtype)


def _short_conv(proj, conv_w, width, *, tc):
    bsz, s, _ = proj.shape
    nb = width // tc
    w = jnp.zeros((8, width), F32).at[:SC_KERNEL].set(conv_w)
    return pl.pallas_call(
        _sc_kernel,
        grid=(bsz, nb),
        in_specs=[pl.BlockSpec((1, s, tc), lambda b, j: (b, 0, j)),
                  pl.BlockSpec((1, s, tc), lambda b, j: (b, 0, j + nb)),
                  pl.BlockSpec((1, s, tc), lambda b, j: (b, 0, j + 2 * nb)),
                  pl.BlockSpec((8, tc), lambda b, j: (0, j))],
        out_specs=pl.BlockSpec((1, s, tc), lambda b, j: (b, 0, j)),
        out_shape=jax.ShapeDtypeStruct((bsz, s, width), BF16),
        compiler_params=_params("parallel", "parallel"),
        name="short_conv",
    )(proj, proj, proj, w)


def _split(x):
    hi = x.astype(BF16)
    return hi, (x - hi.astype(F32)).astype(BF16)


def _dot3(a, b):
    return _dot(a[0], b[0]) + (_dot(a[0], b[1]) + _dot(a[1], b[0]))


DN_GROUP = 8


def _dn_group_local(base, scr, masks, out):
    q_s, k_s, kb_s, qd_s, kf_s, kbe_s, vb_s, gc_s, gct_s = scr
    incl, strict, m8, m16, eye = masks
    c = DN_CHUNK
    idx = range(DN_GROUP)
    rows = [pl.ds(base + cc * c, c) for cc in idx]
    gc = [gc_s[r, :] for r in rows]
    decay = []
    for cc in idx:
        gc_j = gct_s[pl.ds(base + (cc // 2) * LANES, c), (cc % 2) * c:(cc % 2) * c + c]
        decay.append(jnp.where(incl, jnp.exp(jnp.where(incl, gc[cc][:, :c] - gc_j, 0.0)), 0.0))
    kbf = [k_s[r, :] for r in rows]
    kk = [_dot_nt(kb_s[rows[cc], :], kbf[cc]) for cc in idx]
    qk = [_dot_nt(q_s[rows[cc], :], kbf[cc]) for cc in idx]
    yield
    a = [jnp.where(strict, kk[cc] * decay[cc], 0.0) for cc in idx]
    intra = [(qk[cc] * decay[cc]).astype(BF16) for cc in idx]
    ad = [jnp.where(m8, x, 0.0) for x in a]
    ads = [_split(x) for x in ad]
    a2s = [_split(_dot3(x, x)) for x in ads]
    yield
    p = [eye - x for x in ad]
    p1, a4s = [], []
    for cc in idx:
        p1.append(p[cc] + _dot3(_split(p[cc]), a2s[cc]))
        a4s.append(_split(_dot3(a2s[cc], a2s[cc])))
    yield
    p2 = [p1[cc] + _dot3(_split(p1[cc]), a4s[cc]) for cc in idx]
    yield
    ps = [_split(x) for x in p2]
    t = [_split(_dot3(ps[cc], _split(jnp.where(m16, a[cc] - ad[cc], 0.0)))) for cc in idx]
    yield
    ds = [_split(p2[cc] - _dot3(t[cc], ps[cc])) for cc in idx]
    yield
    db, das = [], []
    for cc in idx:
        rhs = jnp.concatenate([vb_s[rows[cc], :], kbe_s[rows[cc], :]], axis=1)
        db.append(_dot3(ds[cc], _split(rhs)))
        das.append(_split(_dot3(ds[cc], _split(jnp.where(m16, 0.0, a[cc])))))
    yield
    blocks = [[x[0:16]] for x in db]
    for s4 in range(1, c // 16):
        rs = slice(16 * s4, 16 * s4 + 16)
        for cc in idx:
            xprev = jnp.concatenate(blocks[cc] + [jnp.zeros((c - 16 * s4, 2 * HEAD_DIM), F32)], axis=0)
            blocks[cc].append(db[cc][rs] - _dot3((das[cc][0][rs], das[cc][1][rs]), _split(xprev)))
        yield
    for cc in idx:
        sol = jnp.concatenate(blocks[cc], axis=0)
        sol_hi, sol_lo = _split(sol)
        g_last = gc[cc][c - 1:c, :]
        k_dec_t = (kf_s[rows[cc], :] * jnp.exp(g_last - gc[cc])).T.astype(BF16)
        kw = _dot(k_dec_t, sol_hi) + _dot(k_dec_t, sol_lo)
        iw = _dot(intra[cc], sol_hi) + _dot(intra[cc], sol_lo)
        out.append((kw[:, HEAD_DIM:].astype(BF16), kw[:, :HEAD_DIM],
                    (qd_s[rows[cc], :] - iw[:, HEAD_DIM:]).astype(BF16), iw[:, :HEAD_DIM], jnp.exp(g_last)))
    yield


def _dn_chunk_seq(state, loc, z, nw):
    w2, n_mat, qp, op, eg_last = loc
    sb = state.astype(BF16)
    o = _dot(qp, sb) + op
    state = (state * eg_last - _dot(w2, sb)) + n_mat
    o = o * lax.rsqrt(jnp.mean(o * o, axis=-1, keepdims=True) + NORM_EPS) * nw * _silu(z)
    return state, o


def _dn_kernel(qp_ref, kp_ref, vp_ref, z_ref, ba_ref, arow_ref, dtrow_ref, cwq_ref, cwk_ref, cwv_ref, nw_ref, o_ref,
               q_s, k_s, kb_s, qd_s, kf_s, kbe_s, vb_s, gc_s, gct_s, *, n_heads):
    h = pl.program_id(1)
    s = qp_ref.shape[1]
    c = DN_CHUNK
    ba = ba_ref[0]
    lane = lax.broadcasted_iota(jnp.int32, ba.shape, 1)
    xa = ba + dtrow_ref[...]
    softplus = jnp.maximum(xa, 0.0) + jnp.log(1.0 + jnp.exp(-jnp.abs(xa)))
    g_full = -jnp.exp(arow_ref[...]) * softplus
    beta = jnp.sum(jnp.where(lane == h, _sigmoid(ba), 0.0), axis=1, keepdims=True)
    g_col = jnp.sum(jnp.where(lane == h + n_heads, g_full, 0.0), axis=1, keepdims=True)
    gc = jnp.broadcast_to(g_col, (s, LANES))
    pos = lax.broadcasted_iota(jnp.int32, (s, LANES), 0) & (c - 1)
    sh = 1
    while sh < c:
        gc = gc + jnp.where(pos >= sh, pltpu.roll(gc, sh, axis=0), 0.0)
        sh *= 2
    gc_s[...] = gc
    for blk in range(s // LANES):
        rs = slice(blk * LANES, (blk + 1) * LANES)
        gct_s[rs, :] = gc[rs, :].T
    eg = jnp.exp(gc)
    q = _silu(_causal_conv(qp_ref[0], cwq_ref, DN_CONV))
    q = q * (lax.rsqrt(jnp.sum(q * q, axis=-1, keepdims=True) + NORM_EPS) * (HEAD_DIM ** -0.5))
    q_s[...] = q.astype(BF16)
    qd_s[...] = q * eg
    k = _silu(_causal_conv(kp_ref[0], cwk_ref, DN_CONV))
    k = k * lax.rsqrt(jnp.sum(k * k, axis=-1, keepdims=True) + NORM_EPS)
    kb = k * beta
    kf_s[...] = k
    k_s[...] = k.astype(BF16)
    kb_s[...] = kb.astype(BF16)
    kbe_s[...] = kb * eg
    vb_s[...] = _silu(_causal_conv(vp_ref[0], cwv_ref, DN_CONV)) * beta

    row = lax.broadcasted_iota(jnp.int32, (c, c), 0)
    col = lax.broadcasted_iota(jnp.int32, (c, c), 1)
    masks = (row >= col, row > col, (row >> 3) == (col >> 3), (row >> 4) == (col >> 4), (row == col).astype(F32))
    scr = (q_s, k_s, kb_s, qd_s, kf_s, kbe_s, vb_s, gc_s, gct_s)
    nw = nw_ref[...]
    rows_per_group = DN_GROUP * c
    n_groups = s // rows_per_group

    def group_base(gi):
        base = gi * rows_per_group
        return base if isinstance(base, int) else pl.multiple_of(base, rows_per_group)

    def run(gi_local, gi_seq, state, locs):
        nxt = []
        stages = iter(()) if gi_local is None else _dn_group_local(group_base(gi_local), scr, masks, nxt)
        todo = list(range(DN_GROUP)) if gi_seq is not None else []
        done = False
        while todo or not done:
            if not done:
                done = next(stages, "end") == "end"
            if todo:
                cc = todo.pop(0)
                rows = pl.ds(group_base(gi_seq) + cc * c, c)
                state, o = _dn_chunk_seq(state, locs[cc], z_ref[0, rows, :], nw)
                o_ref[0, rows, :] = o.astype(o_ref.dtype)
        return state, tuple(nxt)

    def body(gi, carry):
        return run(gi + 1, gi, *carry)

    carry = run(0, None, jnp.zeros((HEAD_DIM, HEAD_DIM), F32), None)
    carry = lax.fori_loop(0, n_groups - 1, body, carry)
    run(None, n_groups - 1, *carry)


def _deltanet(proj, qkv_col0, z_col0, conv_w, ba, a_log, dt_bias, norm_w, n_heads):
    bsz, s, _ = proj.shape
    qb0 = qkv_col0 // HEAD_DIM
    zb0 = z_col0 // HEAD_DIM
    arow = jnp.zeros((1, LANES), F32).at[0, n_heads:2 * n_heads].set(a_log)
    dtrow = jnp.zeros((1, LANES), F32).at[0, n_heads:2 * n_heads].set(dt_bias)
    cw = jnp.zeros((8, 3 * n_heads * HEAD_DIM), F32).at[:DN_CONV].set(conv_w)
    blk = (1, s, HEAD_DIM)
    col_spec = lambda off: pl.BlockSpec(blk, lambda b, h, off=off: (b, 0, h + off))
    cw_spec = lambda off: pl.BlockSpec((8, HEAD_DIM), lambda b, h, off=off: (0, h + off))
    row_spec = pl.BlockSpec((1, LANES), lambda b, h: (0, 0))
    return pl.pallas_call(
        functools.partial(_dn_kernel, n_heads=n_heads),
        grid=(bsz, n_heads),
        in_specs=[col_spec(qb0), col_spec(qb0 + n_heads), col_spec(qb0 + 2 * n_heads), col_spec(zb0),
                  pl.BlockSpec((1, s, LANES), lambda b, h: (b, 0, 0)), row_spec, row_spec,
                  cw_spec(0), cw_spec(n_heads), cw_spec(2 * n_heads), row_spec],
        out_specs=pl.BlockSpec(blk, lambda b, h: (b, 0, h)),
        out_shape=jax.ShapeDtypeStruct((bsz, s, n_heads * HEAD_DIM), BF16),
        scratch_shapes=[pltpu.VMEM((s, HEAD_DIM), BF16)] * 3 + [pltpu.VMEM((s, HEAD_DIM), F32)] * 6,
        compiler_params=_params("parallel", "arbitrary"),
        name="deltanet",
    )(proj, proj, proj, proj, ba, arow, dtrow, cw, cw, cw, norm_w.reshape(1, HEAD_DIM))


def _rope_tables(ang):
    lane = lax.broadcasted_iota(jnp.int32, ang.shape, 1)
    sin = jnp.sin(ang)
    return jnp.cos(ang), jnp.where(lane < HEAD_DIM // 2, -sin, sin)


def _rope(x, cos, sin_signed):
    return x * cos + pltpu.roll(x, HEAD_DIM // 2, axis=1) * sin_signed


def _rope_kernel(ang_ref, q_ref, ks_ref, kw_ref, vs_ref, vw_ref, qo_ref, kso_ref, kwo_ref, vso_ref, vwo_ref):
    cos, sin = _rope_tables(ang_ref[0])
    for hh in range(NSA_HEADS):
        sl = slice(hh * HEAD_DIM, (hh + 1) * HEAD_DIM)
        qo_ref[0, :, sl] = (_rope(q_ref[0, :, sl], cos, sin) * Q_SCALE).astype(BF16)
    for g in range(NSA_KV_HEADS):
        sl = slice(g * HEAD_DIM, (g + 1) * HEAD_DIM)
        kso_ref[0, :, sl] = _rope(ks_ref[0, :, sl], cos, sin).astype(BF16)
        kwo_ref[0, :, sl] = _rope(kw_ref[0, :, sl], cos, sin).astype(BF16)
    ts = vs_ref.shape[1]
    for v_ref, vo_ref in ((vs_ref, vso_ref), (vw_ref, vwo_ref)):
        ck = vo_ref.shape[4]
        for g in range(NSA_KV_HEADS):
            for cc in range(ts // ck):
                parts = [v_ref[0, cc * ck + r:cc * ck + r + LANES, g * HEAD_DIM:(g + 1) * HEAD_DIM].T
                         for r in range(0, ck, LANES)]
                vo_ref[0, g, cc] = jnp.concatenate(parts, axis=1).astype(BF16)


def _rope_qkv(proj, ang, *, ts, slc_chunk, win_chunk):
    bsz, s, _ = proj.shape
    qw = NSA_HEADS * HEAD_DIM
    kvw = NSA_KV_HEADS * HEAD_DIM
    kv_spec = lambda blk: pl.BlockSpec((1, ts, kvw), lambda b, i, blk=blk: (b, i, blk))
    kv_out = pl.BlockSpec((1, ts, kvw), lambda b, i: (b, i, 0))
    kv_shape = jax.ShapeDtypeStruct((bsz, s, kvw), BF16)
    vt_out = lambda ck: pl.BlockSpec((1, NSA_KV_HEADS, ts // ck, HEAD_DIM, ck), lambda b, i: (b, 0, i, 0, 0))
    vt_shape = lambda ck: jax.ShapeDtypeStruct((bsz, NSA_KV_HEADS, s // ck, HEAD_DIM, ck), BF16)
    base = qw // kvw
    return pl.pallas_call(
        _rope_kernel,
        grid=(bsz, s // ts),
        in_specs=[pl.BlockSpec((1, ts, HEAD_DIM), lambda b, i: (b, i, 0)),
                  pl.BlockSpec((1, ts, qw), lambda b, i: (b, i, 0)),
                  kv_spec(base + 2), kv_spec(base + 4), kv_spec(base + 3), kv_spec(base + 5)],
        out_specs=[pl.BlockSpec((1, ts, qw), lambda b, i: (b, i, 0)), kv_out, kv_out,
                   vt_out(slc_chunk), vt_out(win_chunk)],
        out_shape=[jax.ShapeDtypeStruct((bsz, s, qw), BF16), kv_shape, kv_shape,
                   vt_shape(slc_chunk), vt_shape(win_chunk)],
        compiler_params=_params("parallel", "parallel"),
        name="rope_qkv",
    )(ang, proj, proj, proj, proj, proj)


def _gelu_tanh(x):
    return x * (0.5 * (1.0 + jnp.tanh(math.sqrt(2.0 / math.pi) * (x + 0.044715 * (x * x * x)))))


def _compress_kernel(x_ref, w1_ref, w2_ref, pos_ref, ang_ref, o_ref, *, rope):
    nsub = x_ref.shape[1] // CMP_STRIDE
    hid = w1_ref.shape[1]
    pa = jnp.zeros((nsub, hid), F32)
    pb = jnp.zeros((nsub, hid), F32)
    for l in range(CMP_STRIDE):
        xl = x_ref[0, pl.ds(l, nsub, stride=CMP_STRIDE), :].astype(BF16)
        pa = pa + _dot(xl, w1_ref[l * HEAD_DIM:(l + 1) * HEAD_DIM, :])
        pb = pb + _dot(xl, w1_ref[(CMP_STRIDE + l) * HEAD_DIM:(CMP_STRIDE + l + 1) * HEAD_DIM, :])
    bias = _dot(pos_ref[...], w1_ref[...])[0:1, :]
    hpre = pa + pltpu.roll(pb, nsub - 1, axis=0) + bias
    out = _dot(_gelu_tanh(hpre).astype(BF16), w2_ref[...])
    if rope:
        cos, sin = _rope_tables(ang_ref[0])
        out = _rope(out, cos, sin)
    o_ref[0, 0] = out.astype(o_ref.dtype)


def _compress(proj, col0, pos_emb, w1, w2, ang_cmp, *, rope):
    bsz, s, _ = proj.shape
    nsub = s // CMP_STRIDE
    blk0 = col0 // HEAD_DIM
    hid = w1.shape[1]
    pos = jnp.zeros((8, CMP_BLOCK * HEAD_DIM), BF16).at[0].set(pos_emb.reshape(-1).astype(BF16))
    return pl.pallas_call(
        functools.partial(_compress_kernel, rope=rope),
        grid=(bsz, NSA_KV_HEADS),
        in_specs=[pl.BlockSpec((1, s, HEAD_DIM), lambda b, g: (b, 0, g + blk0)),
                  pl.BlockSpec((CMP_BLOCK * HEAD_DIM, hid), lambda b, g: (0, 0)),
                  pl.BlockSpec((hid, HEAD_DIM), lambda b, g: (0, 0)),
                  pl.BlockSpec((8, CMP_BLOCK * HEAD_DIM), lambda b, g: (0, 0)),
                  pl.BlockSpec((1, nsub, HEAD_DIM), lambda b, g: (b, 0, 0))],
        out_specs=pl.BlockSpec((1, 1, nsub, HEAD_DIM), lambda b, g: (b, g, 0, 0)),
        out_shape=jax.ShapeDtypeStruct((bsz, NSA_KV_HEADS, nsub, HEAD_DIM), BF16),
        compiler_params=_params("parallel", "parallel"),
        name="compress",
    )(proj, w1.astype(BF16), w2.astype(BF16), pos, ang_cmp)


def _cmp_attn_kernel(q_ref, kc_ref, vc_ref, smat_ref, o_ref, sel_ref):
    tq = q_ref.shape[1]
    ncol = kc_ref.shape[2]
    t = pl.program_id(2) * tq + lax.broadcasted_iota(jnp.int32, (tq, ncol), 0)
    n = lax.broadcasted_iota(jnp.int32, (tq, ncol), 1)
    valid = (n * CMP_STRIDE + CMP_BLOCK - 1) <= t
    kc = kc_ref[0, 0]
    vc = vc_ref[0, 0]
    p_grp = jnp.zeros((tq, ncol), F32)
    for hh in range(HPG):
        sl = slice(hh * HEAD_DIM, (hh + 1) * HEAD_DIM)
        sc = jnp.where(valid, _dot_nt(q_ref[0, :, sl], kc), NEG_INF)
        e = jnp.exp2(sc - jnp.max(sc, axis=-1, keepdims=True))
        p = jnp.where(valid, e / jnp.sum(e, axis=-1, keepdims=True), 0.0)
        o_ref[0, :, sl] = _dot(p.astype(BF16), vc)
        p_grp = p_grp + p
    score = _dot_hi(p_grp, smat_ref[...])
    n_sel = sel_ref.shape[2]
    score = jnp.concatenate([score[r:r + LANES].T for r in range(0, tq, LANES)], axis=1)[:n_sel]
    n = lax.broadcasted_iota(jnp.int32, (n_sel, tq), 0)
    t = pl.program_id(2) * tq + lax.broadcasted_iota(jnp.int32, (n_sel, tq), 1)
    cur = t >> int(math.log2(SEL_BLOCK))
    forced = (n == 0) | (n == cur) | (n == cur - 1)
    future = n * SEL_BLOCK > t
    score = jnp.where(forced, jnp.inf, jnp.where(future, -jnp.inf, score))
    rank = jnp.zeros((n_sel, tq), jnp.int32)
    for kk in range(n_sel):
        ck = score[kk:kk + 1, :]
        ahead = (ck > score) | ((ck == score) & (kk < n))
        rank = rank + ahead.astype(jnp.int32)
    sel_ref[0, 0] = (rank < N_SELECT).astype(sel_ref.dtype)


def _sel_matrix(ncol, n_sel):
    rs = SEL_BLOCK // CMP_STRIDE
    rc = CMP_BLOCK // CMP_STRIDE
    mat = [[0.0] * ncol for _ in range(ncol)]
    for j in range(n_sel):
        for m in range(rs):
            for n in range(rc):
                i = rs * j + m + n - (rc - 1)
                if 0 <= i < ncol - 1:
                    mat[i][j] += 1.0
    return jnp.array(mat, F32)


def _cmp_attention(q_r, k_cmp, v_cmp, *, tq):
    bsz, s, _ = q_r.shape
    ncol = k_cmp.shape[2]
    n_sel = s // SEL_BLOCK
    gw = HPG * HEAD_DIM
    return pl.pallas_call(
        _cmp_attn_kernel,
        grid=(bsz, NSA_KV_HEADS, s // tq),
        in_specs=[pl.BlockSpec((1, tq, gw), lambda b, g, i: (b, i, g)),
                  pl.BlockSpec((1, 1, ncol, HEAD_DIM), lambda b, g, i: (b, g, 0, 0)),
                  pl.BlockSpec((1, 1, ncol, HEAD_DIM), lambda b, g, i: (b, g, 0, 0)),
                  pl.BlockSpec((ncol, ncol), lambda b, g, i: (0, 0))],
        out_specs=[pl.BlockSpec((1, tq, gw), lambda b, g, i: (b, i, g)),
                   pl.BlockSpec((1, 1, n_sel, tq), lambda b, g, i: (b, g, 0, i))],
        out_shape=[jax.ShapeDtypeStruct((bsz, s, NSA_HEADS * HEAD_DIM), F32),
                   jax.ShapeDtypeStruct((bsz, NSA_KV_HEADS, n_sel, s), F32)],
        compiler_params=_params("parallel", "parallel", "parallel"),
        name="cmp_attention",
    )(q_r, k_cmp, v_cmp, _sel_matrix(ncol, n_sel))


def _softmax_pv_t(scores, values_t):
    m = None
    for sc in scores:
        cm = jnp.max(sc, axis=0, keepdims=True)
        m = cm if m is None else jnp.maximum(m, cm)
    l = None
    acc = None
    for sc, v_t in zip(scores, values_t):
        p = jnp.exp2(sc - m)
        ps = jnp.sum(p, axis=0, keepdims=True)
        pv = _dot(v_t, p.astype(BF16))
        l = ps if l is None else l + ps
        acc = pv if acc is None else acc + pv
    return acc / l


MASK_BIG = 2.0 ** 100
SLC_VARIANT_CHUNKS = 2


def _slc_win_kernel(q_ref, ks_ref, vst_ref, kw_ref, vwt_ref, selt_ref, blk_ref, oc_ref, gate_ref, o_ref,
                    gt_s, os_s):
    tq = q_ref.shape[1]
    tk = vst_ref.shape[4]
    n_chunks = vst_ref.shape[2]
    n_sel = selt_ref.shape[2]
    g = pl.program_id(1)
    t0 = pl.program_id(2) * tq
    q_t = jnp.concatenate([q_ref[0, :, hh * HEAD_DIM:(hh + 1) * HEAD_DIM].astype(F32).T.astype(BF16)
                           for hh in range(HPG)], axis=1)

    key_loc = lax.broadcasted_iota(jnp.int32, (tq, tq), 0)
    qry_loc = lax.broadcasted_iota(jnp.int32, (tq, tq), 1)
    n_win = WINDOW // tq + 1
    scores, values = [], []
    for cc in range(n_win):
        k0 = t0 - WINDOW + cc * tq
        k0c = pl.multiple_of(jnp.maximum(k0, 0), tq)
        sc = _dot(kw_ref[0, pl.ds(k0c, tq), :], q_t)
        if cc == 0:
            ok = jnp.where(k0 >= 0, key_loc - qry_loc, 0) > 0
            sc = jnp.where(jnp.concatenate([ok] * HPG, axis=1), sc, NEG_INF)
        elif cc == n_win - 1:
            sc = jnp.where(jnp.concatenate([key_loc <= qry_loc] * HPG, axis=1), sc, NEG_INF)
        else:
            sc = sc + jnp.where(k0 >= 0, 0.0, NEG_INF)
        scores.append(sc)
        values.append(vwt_ref[0, 0, k0c // tq])
    o_win_t = _softmax_pv_t(scores, values)

    sel_bias = ((selt_ref[0, 0] - 1.0) * MASK_BIG).astype(BF16)
    q_aug = jnp.concatenate([q_t, jnp.concatenate([sel_bias] * HPG, axis=1),
                             jnp.zeros((HEAD_DIM - n_sel, HPG * tq), BF16)], axis=0)
    k_loc = lax.broadcasted_iota(jnp.int32, (tk, tq), 0)
    t_loc = lax.broadcasted_iota(jnp.int32, (tk, tq), 1)

    def selected(n_used):
        scores, values = [], []
        for cc in range(n_used):
            rows = slice(cc * tk, (cc + 1) * tk)
            k_aug = jnp.concatenate([ks_ref[0, rows, :], blk_ref[rows, :]], axis=1)
            sc = _dot(k_aug, q_aug)
            if cc >= n_used - SLC_VARIANT_CHUNKS:
                causal = (cc * tk + k_loc) <= (t0 + t_loc)
                sc = jnp.where(jnp.concatenate([causal] * HPG, axis=1), sc, NEG_INF)
            scores.append(sc)
            values.append(vst_ref[0, 0, cc])
        os_s[...] = _softmax_pv_t(scores, values)

    variant = t0 // (SLC_VARIANT_CHUNKS * tk)
    for vv in range(n_chunks // SLC_VARIANT_CHUNKS):
        pl.when(variant == vv)(functools.partial(selected, (vv + 1) * SLC_VARIANT_CHUNKS))

    gt = _sigmoid(gate_ref[0])
    gt_s[...] = gt.T
    lane = lax.broadcasted_iota(jnp.int32, gt.shape, 1)
    for hh in range(HPG):
        base = (g * HPG + hh) * 3
        g_cmp = jnp.sum(jnp.where(lane == base, gt, 0.0), axis=1, keepdims=True)
        cs = slice(hh * tq, (hh + 1) * tq)
        mix_t = gt_s[pl.ds(base + 1, 1), :] * os_s[:, cs] + gt_s[pl.ds(base + 2, 1), :] * o_win_t[:, cs]
        sl = slice(hh * HEAD_DIM, (hh + 1) * HEAD_DIM)
        o_ref[0, :, sl] = (g_cmp * oc_ref[0, :, sl] + mix_t.T).astype(o_ref.dtype)


def _slc_win_attention(q_r, ks, vs_t, kw, vw_t, sel_t, o_cmp, gates, *, tq):
    bsz, s, _ = q_r.shape
    n_sel = s // SEL_BLOCK
    gw = HPG * HEAD_DIM
    assert vw_t.shape[4] == tq and tq == LANES and vs_t.shape[2] % SLC_VARIANT_CHUNKS == 0
    kv_spec = pl.BlockSpec((1, s, HEAD_DIM), lambda b, g, i: (b, 0, g))
    vt_spec = lambda a: pl.BlockSpec((1, 1) + a.shape[2:], lambda b, g, i: (b, g, 0, 0, 0))
    block_onehot = (jnp.arange(s)[:, None] // SEL_BLOCK == jnp.arange(HEAD_DIM)[None, :]).astype(BF16)
    return pl.pallas_call(
        _slc_win_kernel,
        grid=(bsz, NSA_KV_HEADS, s // tq),
        in_specs=[pl.BlockSpec((1, tq, gw), lambda b, g, i: (b, i, g)),
                  kv_spec, vt_spec(vs_t), kv_spec, vt_spec(vw_t),
                  pl.BlockSpec((1, 1, n_sel, tq), lambda b, g, i: (b, g, 0, i)),
                  pl.BlockSpec((s, HEAD_DIM), lambda b, g, i: (0, 0)),
                  pl.BlockSpec((1, tq, gw), lambda b, g, i: (b, i, g)),
                  pl.BlockSpec((1, tq, LANES), lambda b, g, i: (b, i, 0))],
        out_specs=pl.BlockSpec((1, tq, gw), lambda b, g, i: (b, i, g)),
        out_shape=jax.ShapeDtypeStruct((bsz, s, NSA_HEADS * HEAD_DIM), BF16),
        scratch_shapes=[pltpu.VMEM((LANES, tq), F32), pltpu.VMEM((HEAD_DIM, HPG * tq), F32)],
        compiler_params=_params("parallel", "parallel", "arbitrary"),
        name="slc_win_attention",
    )(q_r, ks, vs_t, kw, vw_t, sel_t, block_onehot, o_cmp, gates)


FFN_OUT_K_STEPS = 4


def _pad_cols(w, n):
    return jnp.pad(w, ((0, 0), (0, n - w.shape[1])))


def _conv_deltanet_mixer(xb, bsz, s, w_in, sc_conv_w, dn_conv_w, a_log, dt_bias, norm_w, w_out):
    sc_w = sc_conv_w.shape[1]
    dn_w = dn_conv_w.shape[1] // 3
    n_heads = dn_w // HEAD_DIM
    main = 3 * sc_w + 4 * dn_w
    proj, ba = _in_proj(xb, w_in[:, :main].astype(BF16), _pad_cols(w_in[:, main:], LANES).astype(BF16),
                        tm=1024, tn=1024)
    proj = proj.reshape(bsz, s, main)
    y_sc = _short_conv(proj, sc_conv_w, sc_w, tc=256)
    y_dn = _deltanet(proj, 3 * sc_w, 3 * sc_w + 3 * dn_w, dn_conv_w, ba.reshape(bsz, s, LANES), a_log, dt_bias,
                     norm_w, n_heads)
    wo = w_out.astype(BF16)
    return [y_sc.reshape(bsz * s, sc_w), y_dn.reshape(bsz * s, dn_w)], [wo[:sc_w], wo[sc_w:]]


def _nsa_mixer(xb, bsz, s, positions, w_in, cmp_pos_k, cmp_w1_k, cmp_w2_k, cmp_pos_v, cmp_w1_v, cmp_w2_v, w_out):
    qw = NSA_HEADS * HEAD_DIM
    kvw = NSA_KV_HEADS * HEAD_DIM
    main = qw + 6 * kvw
    proj, gates = _in_proj(xb, w_in[:, :main].astype(BF16), _pad_cols(w_in[:, main:], LANES).astype(BF16),
                           tm=1024, tn=1024)
    proj = proj.reshape(bsz, s, main)
    half = HEAD_DIM // 2
    inv = jnp.power(ROPE_THETA, -jnp.arange(half, dtype=F32) / half)
    inv = jnp.concatenate([inv, inv])
    ang = positions.astype(F32)[..., None] * inv
    cmp_end = jnp.minimum(jnp.arange(s // CMP_STRIDE) * CMP_STRIDE + CMP_BLOCK - 1, s - 1)
    ang_cmp = positions[:, cmp_end].astype(F32)[..., None] * inv
    q_r, ks, kw, vs_t, vw_t = _rope_qkv(proj, ang, ts=512, slc_chunk=256, win_chunk=LANES)
    k_cmp = _compress(proj, qw, cmp_pos_k, cmp_w1_k, cmp_w2_k, ang_cmp, rope=True)
    v_cmp = _compress(proj, qw + kvw, cmp_pos_v, cmp_w1_v, cmp_w2_v, ang_cmp, rope=False)
    o_cmp, sel_t = _cmp_attention(q_r, k_cmp, v_cmp, tq=256)
    o = _slc_win_attention(q_r, ks, vs_t, kw, vw_t, sel_t, o_cmp, gates.reshape(bsz, s, LANES), tq=LANES)
    return [o.reshape(bsz * s, qw)], [w_out.astype(BF16)]


def kernel(x, positions, ln_mix_g, ln_mix_b, ln_ffn_g, ln_ffn_b, ffn_w_in, ffn_w_out, hy_w_in, sc_conv_w, dn_conv_w, dn_a_log, dn_dt_bias, dn_norm_w, hy_w_out, nsa_w_in, cmp_pos_k, cmp_w1_k, cmp_w2_k, cmp_pos_v, cmp_w1_v, cmp_w2_v, nsa_w_out):
    bsz, s, d = x.shape
    xf = x.reshape(bsz * s, d)
    xb = xf
    for i in range(DEPTH):
        j = i // 2
        if i % 2 == 0:
            ys, wos = _conv_deltanet_mixer(xb, bsz, s, hy_w_in[j], sc_conv_w[j], dn_conv_w[j], dn_a_log[j],
                                           dn_dt_bias[j], dn_norm_w[j], hy_w_out[j])
        else:
            ys, wos = _nsa_mixer(xb, bsz, s, positions, nsa_w_in[j], cmp_pos_k[j], cmp_w1_k[j], cmp_w2_k[j],
                                 cmp_pos_v[j], cmp_w1_v[j], cmp_w2_v[j], nsa_w_out[j])
        xf, xb = _matmul_ln(ys, wos, xf, ln_mix_g[i], ln_mix_b[i], tm=512, nk=1)
        hmid = _ffn_in(xb, ffn_w_in[i], tm=1024, tn=512)
        xf, xb = _matmul_ln([hmid], [ffn_w_out[i].astype(BF16)], xf, ln_ffn_g[i], ln_ffn_b[i], tm=512,
                            nk=FFN_OUT_K_STEPS)
    return xf.reshape(bsz, s, d)
```

```python
import functools
import math

import jax
import jax.numpy as jnp
from jax import lax
from jax.experimental import pallas as pl
from jax.experimental.pallas import tpu as pltpu

F32 = jnp.float32
BF16 = jnp.bfloat16
HIGHEST = lax.Precision.HIGHEST

LANES = 128
VMEM_LIMIT = 48 * 1024 * 1024

DN_HEADS = 8
DN_CHUNK = 64
DN_CONV = 4
SC_KERNEL = 3
NSA_HEADS = 16
NSA_KV_HEADS = 4
HPG = NSA_HEADS // NSA_KV_HEADS
HEAD_DIM = 128
CMP_BLOCK = 32
CMP_STRIDE = 16
SEL_BLOCK = 64
N_SELECT = 16
WINDOW = 512
ROPE_THETA = 10000.0
LN_EPS = 1e-5
NORM_EPS = 1e-6
NEG_INF = -1e30
DEPTH = 2
ALPHA = (2 * DEPTH) ** 0.25
ATTN_SCALE = HEAD_DIM ** -0.5
Q_SCALE = ATTN_SCALE * math.log2(math.e)


def _params(*sem):
    return pltpu.CompilerParams(dimension_semantics=sem, vmem_limit_bytes=VMEM_LIMIT)


def _sigmoid(x):
    return 1.0 / (1.0 + jnp.exp(-x))


def _silu(x):
    return x * _sigmoid(x)


def _dot(a, b):
    return jnp.dot(a, b, preferred_element_type=F32)


def _dot_nt(a, b):
    return lax.dot_general(a, b, (((1,), (1,)), ((), ())), preferred_element_type=F32)


def _dot_tn(a, b):
    return lax.dot_general(a, b, (((0,), (0,)), ((), ())), preferred_element_type=F32)


def _dot_hi(a, b):
    return jnp.dot(a, b, precision=HIGHEST, preferred_element_type=F32)


def _proj_kernel(x_ref, w_ref, ws_ref, o_ref, os_ref):
    xb = x_ref[...].astype(BF16)
    o_ref[...] = _dot(xb, w_ref[...])

    @pl.when(pl.program_id(1) == 0)
    def _():
        os_ref[...] = _dot(xb, ws_ref[...])


def _in_proj(x, w, n, w_side, *, tm, tn):
    m, k = x.shape
    ns = w_side.shape[1]
    return pl.pallas_call(
        _proj_kernel,
        grid=(m // tm, n // tn),
        in_specs=[pl.BlockSpec((tm, k), lambda i, j: (i, 0)),
                  pl.BlockSpec((k, tn), lambda i, j: (0, j)),
                  pl.BlockSpec((k, ns), lambda i, j: (0, 0))],
        out_specs=[pl.BlockSpec((tm, tn), lambda i, j: (i, j)),
                   pl.BlockSpec((tm, ns), lambda i, j: (i, 0))],
        out_shape=[jax.ShapeDtypeStruct((m, n), F32), jax.ShapeDtypeStruct((m, ns), F32)],
        compiler_params=_params("parallel", "arbitrary"),
        name="in_proj",
    )(x, w, w_side)


def _ffn_in_kernel(x_ref, wg_ref, wu_ref, o_ref, wgb_s, wub_s):
    @pl.when(pl.program_id(1) == 0)
    def _():
        wgb_s[...] = wg_ref[...].astype(BF16)
        wub_s[...] = wu_ref[...].astype(BF16)

    x = x_ref[...]
    gate = _dot(x, wgb_s[...])
    up = _dot(x, wub_s[...])
    o_ref[...] = (_silu(gate) * up).astype(o_ref.dtype)


def _ffn_in(xb, w_in, layer, *, tm, tn):
    m, k = xb.shape
    hidden = w_in.shape[2] // 2
    nj = hidden // tn
    return pl.pallas_call(
        _ffn_in_kernel,
        grid=(nj, m // tm),
        in_specs=[pl.BlockSpec((tm, k), lambda j, i: (i, 0)),
                  pl.BlockSpec((None, k, tn), lambda j, i: (layer, 0, j)),
                  pl.BlockSpec((None, k, tn), lambda j, i: (layer, 0, j + nj))],
        out_specs=pl.BlockSpec((tm, tn), lambda j, i: (i, j)),
        out_shape=jax.ShapeDtypeStruct((m, hidden), BF16),
        scratch_shapes=[pltpu.VMEM((k, tn), BF16), pltpu.VMEM((k, tn), BF16)],
        compiler_params=_params("parallel", "arbitrary"),
        name="ffn_in",
    )(xb, w_in, w_in)


def _mm_ln_kernel(*refs, n_pairs, nk):
    xs = refs[:n_pairs]
    ws = refs[n_pairs:2 * n_pairs]
    r_ref, g_ref, b_ref, o_ref, ob_ref = refs[2 * n_pairs:2 * n_pairs + 5]
    part = _dot(xs[0][...], ws[0][...])
    for x_ref, w_ref in zip(xs[1:], ws[1:]):
        part = part + _dot(x_ref[...], w_ref[...])

    def finish(y):
        v = ALPHA * r_ref[...] + y
        mu = jnp.mean(v, axis=-1, keepdims=True)
        d = v - mu
        var = jnp.mean(d * d, axis=-1, keepdims=True)
        out = d * lax.rsqrt(var + LN_EPS) * g_ref[...] + b_ref[...]
        o_ref[...] = out
        ob_ref[...] = out.astype(BF16)

    if nk == 1:
        finish(part)
    else:
        acc_ref = refs[-1]
        kk = pl.program_id(1)

        @pl.when(kk == 0)
        def _():
            acc_ref[...] = part

        @pl.when((kk > 0) & (kk < nk - 1))
        def _():
            acc_ref[...] += part

        @pl.when(kk == nk - 1)
        def _():
            finish(acc_ref[...] + part)


def _matmul_ln(xs, ws, resid, g, b, *, tm, nk, layer=None):
    m, d = resid.shape
    n_pairs = len(xs)
    tks = [x.shape[1] // nk for x in xs]
    if layer is None:
        w_specs = [pl.BlockSpec((tk, d), lambda i, kk: (kk, 0)) for tk in tks]
    else:
        w_specs = [pl.BlockSpec((None, tk, d), lambda i, kk: (layer, kk, 0)) for tk in tks]
    in_specs = ([pl.BlockSpec((tm, tk), lambda i, kk: (i, kk)) for tk in tks]
                + w_specs
                + [pl.BlockSpec((tm, d), lambda i, kk: (i, 0)),
                   pl.BlockSpec((1, d), lambda i, kk: (0, 0)),
                   pl.BlockSpec((1, d), lambda i, kk: (0, 0))])
    return pl.pallas_call(
        functools.partial(_mm_ln_kernel, n_pairs=n_pairs, nk=nk),
        grid=(m // tm, nk),
        in_specs=in_specs,
        out_specs=[pl.BlockSpec((tm, d), lambda i, kk: (i, 0)),
                   pl.BlockSpec((tm, d), lambda i, kk: (i, 0))],
        out_shape=[jax.ShapeDtypeStruct((m, d), F32), jax.ShapeDtypeStruct((m, d), BF16)],
        scratch_shapes=[pltpu.VMEM((tm, d), F32)] if nk > 1 else [],
        compiler_params=_params("parallel", "arbitrary"),
        name="matmul_ln",
    )(*xs, *ws, resid, g.reshape(1, d), b.reshape(1, d))


def _causal_conv(u, w_ref, taps):
    def tap_sum(x, shift):
        acc = x * w_ref[taps - 1:taps, :]
        for sh in range(1, taps):
            acc = acc + shift(x, sh) * w_ref[taps - 1 - sh:taps - sh, :]
        return acc

    body = tap_sum(u, lambda x, sh: pltpu.roll(x, sh, axis=0))
    row = lax.broadcasted_iota(jnp.int32, (8, u.shape[1]), 0)
    head = tap_sum(u[0:8], lambda x, sh: jnp.where(row >= sh, pltpu.roll(x, sh, axis=0), 0.0))
    return jnp.concatenate([head, body[8:]], axis=0)


def _sc_kernel(b_ref, c_ref, h_ref, w_ref, o_ref):
    u = c_ref[0] * h_ref[0]
    o_ref[0] = (b_ref[0] * _causal_conv(u, w_ref, SC_KERNEL)).astype(o_ref.dtype)


def _short_conv(proj, conv_w, width, *, tc):
    bsz, s, _ = proj.shape
    nb = width // tc
    w = jnp.zeros((8, width), F32).at[:SC_KERNEL].set(conv_w)
    return pl.pallas_call(
        _sc_kernel,
        grid=(bsz, nb),
        in_specs=[pl.BlockSpec((1, s, tc), lambda b, j: (b, 0, j)),
                  pl.BlockSpec((1, s, tc), lambda b, j: (b, 0, j + nb)),
                  pl.BlockSpec((1, s, tc), lambda b, j: (b, 0, j + 2 * nb)),
                  pl.BlockSpec((8, tc), lambda b, j: (0, j))],
        out_specs=pl.BlockSpec((1, s, tc), lambda b, j: (b, 0, j)),
        out_shape=jax.ShapeDtypeStruct((bsz, s, width), BF16),
        compiler_params=_params("parallel", "parallel"),
        name="short_conv",
    )(proj, proj, proj, w)


def _split(x):
    hi = x.astype(BF16)
    return hi, (x - hi.astype(F32)).astype(BF16)


def _dotb(a, b):
    return _dot(a.astype(BF16), b.astype(BF16))


DN_GROUP = 8


def _dn_group_local(base, scr, masks, out):
    q_s, k_s, kb_s, qd_s, kf_s, kbe_s, vb_s, gc_s, gcd_s = scr
    incl, strict, m8, m16, eye = masks
    c = DN_CHUNK
    idx = range(DN_GROUP)
    rows = [pl.ds(base + cc * c, c) for cc in idx]
    gc = [gc_s[r, :] for r in rows]
    decay = []
    for cc in idx:
        gc_j = gcd_s[pl.ds(base // LANES + cc // 2, 1), (cc % 2) * c:(cc % 2) * c + c]
        decay.append(jnp.where(incl, jnp.exp(jnp.where(incl, gc[cc][:, :c] - gc_j, 0.0)), 0.0))
    kbf = [k_s[r, :] for r in rows]
    kk = [_dot_nt(kb_s[rows[cc], :], kbf[cc]) for cc in idx]
    qk = [_dot_nt(q_s[rows[cc], :], kbf[cc]) for cc in idx]
    yield
    a = [jnp.where(strict, kk[cc] * decay[cc], 0.0) for cc in idx]
    intra = [(qk[cc] * decay[cc]).astype(BF16) for cc in idx]
    ad = [jnp.where(m8, x, 0.0) for x in a]
    adb = [x.astype(BF16) for x in ad]
    a2 = [_dot(x, x) for x in adb]
    yield
    a2b = [x.astype(BF16) for x in a2]
    p = [eye - x for x in ad]
    p1, a4 = [], []
    for cc in idx:
        p1.append(p[cc] + _dotb(p[cc], a2b[cc]))
        a4.append(_dot(a2b[cc], a2b[cc]))
    yield
    p2 = [p1[cc] + _dotb(p1[cc], a4[cc]) for cc in idx]
    yield
    pb = [x.astype(BF16) for x in p2]
    t = [_dotb(pb[cc], jnp.where(m16, a[cc] - ad[cc], 0.0)) for cc in idx]
    yield
    dinv = [(p2[cc] - _dotb(t[cc], pb[cc])).astype(BF16) for cc in idx]
    yield
    db, da = [], []
    for cc in idx:
        rhs = jnp.concatenate([vb_s[rows[cc], :], kbe_s[rows[cc], :]], axis=1)
        db.append(_dotb(dinv[cc], rhs))
        da.append(_dotb(dinv[cc], jnp.where(m16, 0.0, a[cc])).astype(BF16))
    yield
    blocks = [[x[0:16]] for x in db]
    for s4 in range(1, c // 16):
        rs = slice(16 * s4, 16 * s4 + 16)
        for cc in idx:
            xprev = jnp.concatenate(blocks[cc] + [jnp.zeros((c - 16 * s4, 2 * HEAD_DIM), F32)], axis=0)
            blocks[cc].append(db[cc][rs] - _dotb(da[cc][rs], xprev))
        yield
    for cc in idx:
        sol = jnp.concatenate(blocks[cc], axis=0)
        sol_hi, sol_lo = _split(sol)
        g_last = gc[cc][c - 1:c, :]
        k_dec_t = (kf_s[rows[cc], :] * jnp.exp(g_last - gc[cc])).T.astype(BF16)
        kw = _dot(k_dec_t, sol_hi) + _dot(k_dec_t, sol_lo)
        iw = _dot(intra[cc], sol_hi) + _dot(intra[cc], sol_lo)
        out.append((kw[:, HEAD_DIM:].astype(BF16), kw[:, :HEAD_DIM],
                    (qd_s[rows[cc], :] - iw[:, HEAD_DIM:]).astype(BF16), iw[:, :HEAD_DIM], jnp.exp(g_last)))
    yield


def _dn_chunk_seq(state, loc, z, nw):
    w2, n_mat, qp, op, eg_last = loc
    sb = state.astype(BF16)
    o = _dot(qp, sb) + op
    state = (state * eg_last - _dot(w2, sb)) + n_mat
    o = o * lax.rsqrt(jnp.mean(o * o, axis=-1, keepdims=True) + NORM_EPS) * nw * _silu(z)
    return state, o


def _dn_kernel(qp_ref, kp_ref, vp_ref, z_ref, ba_ref, arow_ref, dtrow_ref, cwq_ref, cwk_ref, cwv_ref, nw_ref, o_ref,
               q_s, k_s, kb_s, qd_s, kf_s, kbe_s, vb_s, gc_s, beta_s, gates_s, gcd_s, *, n_heads):
    h = pl.program_id(1)
    s = qp_ref.shape[1]
    c = DN_CHUNK
    @pl.when(h == 0)
    def _():
        ba = ba_ref[0]
        xa = ba + dtrow_ref[...]
        softplus = jnp.maximum(xa, 0.0) + jnp.log(1.0 + jnp.exp(-jnp.abs(xa)))
        lane0 = lax.broadcasted_iota(jnp.int32, ba.shape, 1)
        gates = jnp.where(lane0 < n_heads, _sigmoid(ba), -jnp.exp(arow_ref[...]) * softplus)
        for blk in range(s // LANES):
            rs = slice(blk * LANES, (blk + 1) * LANES)
            gates_s[rs, :] = gates[rs, :].T

    n_blk = s // LANES
    beta_d = jnp.concatenate([gates_s[pl.ds(blk * LANES + h, 1), :] for blk in range(n_blk)], axis=0)
    gc_d = jnp.concatenate([gates_s[pl.ds(blk * LANES + h + n_heads, 1), :] for blk in range(n_blk)], axis=0)
    pos = lax.broadcasted_iota(jnp.int32, (n_blk, LANES), 1) & (c - 1)
    sh = 1
    while sh < c:
        gc_d = gc_d + jnp.where(pos >= sh, pltpu.roll(gc_d, sh, axis=1), 0.0)
        sh *= 2
    gcd_s[0:n_blk, :] = gc_d
    for blk in range(n_blk):
        rs = slice(blk * LANES, (blk + 1) * LANES)
        gc_s[rs, :] = jnp.broadcast_to(gc_d[blk:blk + 1, :], (LANES, LANES)).T
        beta_s[rs, :] = jnp.broadcast_to(beta_d[blk:blk + 1, :], (LANES, LANES)).T
    gc = gc_s[...]
    beta = beta_s[...]
    eg = jnp.exp(gc)
    q = _silu(_causal_conv(qp_ref[0], cwq_ref, DN_CONV))
    q = q * (lax.rsqrt(jnp.sum(q * q, axis=-1, keepdims=True) + NORM_EPS) * (HEAD_DIM ** -0.5))
    q_s[...] = q.astype(BF16)
    qd_s[...] = q * eg
    k = _silu(_causal_conv(kp_ref[0], cwk_ref, DN_CONV))
    k = k * lax.rsqrt(jnp.sum(k * k, axis=-1, keepdims=True) + NORM_EPS)
    kb = k * beta
    kf_s[...] = k
    k_s[...] = k.astype(BF16)
    kb_s[...] = kb.astype(BF16)
    kbe_s[...] = kb * eg
    vb_s[...] = _silu(_causal_conv(vp_ref[0], cwv_ref, DN_CONV)) * beta

    row = lax.broadcasted_iota(jnp.int32, (c, c), 0)
    col = lax.broadcasted_iota(jnp.int32, (c, c), 1)
    masks = (row >= col, row > col, (row >> 3) == (col >> 3), (row >> 4) == (col >> 4), (row == col).astype(F32))
    scr = (q_s, k_s, kb_s, qd_s, kf_s, kbe_s, vb_s, gc_s, gcd_s)
    nw = nw_ref[...]
    rows_per_group = DN_GROUP * c
    n_groups = s // rows_per_group

    def group_base(gi):
        base = gi * rows_per_group
        return base if isinstance(base, int) else pl.multiple_of(base, rows_per_group)

    def run(gi_local, gi_seq, state, locs):
        nxt = []
        stages = iter(()) if gi_local is None else _dn_group_local(group_base(gi_local), scr, masks, nxt)
        todo = list(range(DN_GROUP)) if gi_seq is not None else []
        done = False
        while todo or not done:
            if not done:
                done = next(stages, "end") == "end"
            if todo:
                cc = todo.pop(0)
                rows = pl.ds(group_base(gi_seq) + cc * c, c)
                state, o = _dn_chunk_seq(state, locs[cc], z_ref[0, rows, :], nw)
                o_ref[0, rows, :] = o.astype(o_ref.dtype)
        return state, tuple(nxt)

    def body(gi, carry):
        return run(gi + 1, gi, *carry)

    carry = run(0, None, jnp.zeros((HEAD_DIM, HEAD_DIM), F32), None)
    carry = lax.fori_loop(0, n_groups - 1, body, carry)
    run(None, n_groups - 1, *carry)


def _deltanet(proj, qkv_col0, z_col0, conv_w, ba, a_log, dt_bias, norm_w, n_heads):
    bsz, s, _ = proj.shape
    qb0 = qkv_col0 // HEAD_DIM
    zb0 = z_col0 // HEAD_DIM
    arow = jnp.zeros((1, LANES), F32).at[0, n_heads:2 * n_heads].set(a_log)
    dtrow = jnp.zeros((1, LANES), F32).at[0, n_heads:2 * n_heads].set(dt_bias)
    cw = jnp.zeros((8, 3 * n_heads * HEAD_DIM), F32).at[:DN_CONV].set(conv_w)
    blk = (1, s, HEAD_DIM)
    col_spec = lambda off: pl.BlockSpec(blk, lambda b, h, off=off: (b, 0, h + off))
    cw_spec = lambda off: pl.BlockSpec((8, HEAD_DIM), lambda b, h, off=off: (0, h + off))
    row_spec = pl.BlockSpec((1, LANES), lambda b, h: (0, 0))
    return pl.pallas_call(
        functools.partial(_dn_kernel, n_heads=n_heads),
        grid=(bsz, n_heads),
        in_specs=[col_spec(qb0), col_spec(qb0 + n_heads), col_spec(qb0 + 2 * n_heads), col_spec(zb0),
                  pl.BlockSpec((1, s, LANES), lambda b, h: (b, 0, 0)), row_spec, row_spec,
                  cw_spec(0), cw_spec(n_heads), cw_spec(2 * n_heads), row_spec],
        out_specs=pl.BlockSpec(blk, lambda b, h: (b, 0, h)),
        out_shape=jax.ShapeDtypeStruct((bsz, s, n_heads * HEAD_DIM), BF16),
        scratch_shapes=([pltpu.VMEM((s, HEAD_DIM), BF16)] * 3 + [pltpu.VMEM((s, HEAD_DIM), F32)] * 7
                        + [pltpu.VMEM((max(8, s // LANES), LANES), F32)]),
        compiler_params=_params("parallel", "arbitrary"),
        name="deltanet",
    )(proj, proj, proj, proj, ba, arow, dtrow, cw, cw, cw, norm_w.reshape(1, HEAD_DIM))


def _rope_tables(ang):
    lane = lax.broadcasted_iota(jnp.int32, ang.shape, 1)
    sin = jnp.sin(ang)
    return jnp.cos(ang), jnp.where(lane < HEAD_DIM // 2, -sin, sin)


def _rope(x, cos, sin_signed):
    return x * cos + pltpu.roll(x, HEAD_DIM // 2, axis=1) * sin_signed


def _rope_kernel(ang_ref, q_ref, ks_ref, kw_ref, vs_ref, vw_ref, qo_ref, kso_ref, kwo_ref, vso_ref, vwo_ref):
    cos, sin = _rope_tables(ang_ref[0])
    for hh in range(NSA_HEADS):
        sl = slice(hh * HEAD_DIM, (hh + 1) * HEAD_DIM)
        qo_ref[0, :, sl] = (_rope(q_ref[0, :, sl], cos, sin) * Q_SCALE).astype(BF16)
    for g in range(NSA_KV_HEADS):
        sl = slice(g * HEAD_DIM, (g + 1) * HEAD_DIM)
        kso_ref[0, :, sl] = _rope(ks_ref[0, :, sl], cos, sin).astype(BF16)
        kwo_ref[0, :, sl] = _rope(kw_ref[0, :, sl], cos, sin).astype(BF16)
    ts = vs_ref.shape[1]
    for v_ref, vo_ref in ((vs_ref, vso_ref), (vw_ref, vwo_ref)):
        ck = vo_ref.shape[4]
        for g in range(NSA_KV_HEADS):
            for cc in range(ts // ck):
                parts = [v_ref[0, cc * ck + r:cc * ck + r + LANES, g * HEAD_DIM:(g + 1) * HEAD_DIM].T
                         for r in range(0, ck, LANES)]
                vo_ref[0, g, cc] = jnp.concatenate(parts, axis=1).astype(BF16)


def _rope_qkv(proj, ang, *, ts, slc_chunk, win_chunk):
    bsz, s, _ = proj.shape
    qw = NSA_HEADS * HEAD_DIM
    kvw = NSA_KV_HEADS * HEAD_DIM
    kv_spec = lambda blk: pl.BlockSpec((1, ts, kvw), lambda b, i, blk=blk: (b, i, blk))
    kv_out = pl.BlockSpec((1, ts, kvw), lambda b, i: (b, i, 0))
    kv_shape = jax.ShapeDtypeStruct((bsz, s, kvw), BF16)
    vt_out = lambda ck: pl.BlockSpec((1, NSA_KV_HEADS, ts // ck, HEAD_DIM, ck), lambda b, i: (b, 0, i, 0, 0))
    vt_shape = lambda ck: jax.ShapeDtypeStruct((bsz, NSA_KV_HEADS, s // ck, HEAD_DIM, ck), BF16)
    base = qw // kvw
    return pl.pallas_call(
        _rope_kernel,
        grid=(bsz, s // ts),
        in_specs=[pl.BlockSpec((1, ts, HEAD_DIM), lambda b, i: (b, i, 0)),
                  pl.BlockSpec((1, ts, qw), lambda b, i: (b, i, 0)),
                  kv_spec(base + 2), kv_spec(base + 4), kv_spec(base + 3), kv_spec(base + 5)],
        out_specs=[pl.BlockSpec((1, ts, qw), lambda b, i: (b, i, 0)), kv_out, kv_out,
                   vt_out(slc_chunk), vt_out(win_chunk)],
        out_shape=[jax.ShapeDtypeStruct((bsz, s, qw), BF16), kv_shape, kv_shape,
                   vt_shape(slc_chunk), vt_shape(win_chunk)],
        compiler_params=_params("parallel", "parallel"),
        name="rope_qkv",
    )(ang, proj, proj, proj, proj, proj)


def _gelu_tanh(x):
    return x * (0.5 * (1.0 + jnp.tanh(math.sqrt(2.0 / math.pi) * (x + 0.044715 * (x * x * x)))))


def _compress_kernel(x_ref, w1_ref, w2_ref, pos_ref, ang_ref, o_ref, *, rope):
    nsub = x_ref.shape[1] // CMP_STRIDE
    hid = w1_ref.shape[1]
    pa = jnp.zeros((nsub, hid), F32)
    pb = jnp.zeros((nsub, hid), F32)
    for l in range(CMP_STRIDE):
        xl = x_ref[0, pl.ds(l, nsub, stride=CMP_STRIDE), :].astype(BF16)
        pa = pa + _dot(xl, w1_ref[l * HEAD_DIM:(l + 1) * HEAD_DIM, :])
        pb = pb + _dot(xl, w1_ref[(CMP_STRIDE + l) * HEAD_DIM:(CMP_STRIDE + l + 1) * HEAD_DIM, :])
    bias = _dot(pos_ref[...], w1_ref[...])[0:1, :]
    hpre = pa + pltpu.roll(pb, nsub - 1, axis=0) + bias
    out = _dot(_gelu_tanh(hpre).astype(BF16), w2_ref[...])
    if rope:
        cos, sin = _rope_tables(ang_ref[0])
        out = _rope(out, cos, sin)
    o_ref[0, 0] = out.astype(o_ref.dtype)


def _compress(proj, col0, pos_emb, w1, w2, ang_cmp, *, rope):
    bsz, s, _ = proj.shape
    nsub = s // CMP_STRIDE
    blk0 = col0 // HEAD_DIM
    hid = w1.shape[1]
    pos = jnp.zeros((8, CMP_BLOCK * HEAD_DIM), BF16).at[0].set(pos_emb.reshape(-1).astype(BF16))
    return pl.pallas_call(
        functools.partial(_compress_kernel, rope=rope),
        grid=(bsz, NSA_KV_HEADS),
        in_specs=[pl.BlockSpec((1, s, HEAD_DIM), lambda b, g: (b, 0, g + blk0)),
                  pl.BlockSpec((CMP_BLOCK * HEAD_DIM, hid), lambda b, g: (0, 0)),
                  pl.BlockSpec((hid, HEAD_DIM), lambda b, g: (0, 0)),
                  pl.BlockSpec((8, CMP_BLOCK * HEAD_DIM), lambda b, g: (0, 0)),
                  pl.BlockSpec((1, nsub, HEAD_DIM), lambda b, g: (b, 0, 0))],
        out_specs=pl.BlockSpec((1, 1, nsub, HEAD_DIM), lambda b, g: (b, g, 0, 0)),
        out_shape=jax.ShapeDtypeStruct((bsz, NSA_KV_HEADS, nsub, HEAD_DIM), BF16),
        compiler_params=_params("parallel", "parallel"),
        name="compress",
    )(proj, w1.astype(BF16), w2.astype(BF16), pos, ang_cmp)


def _cmp_attn_kernel(q_ref, kc_ref, vc_ref, smat_ref, o_ref, sel_ref):
    tq = q_ref.shape[1]
    ncol = kc_ref.shape[2]
    t = pl.program_id(2) * tq + lax.broadcasted_iota(jnp.int32, (tq, ncol), 0)
    n = lax.broadcasted_iota(jnp.int32, (tq, ncol), 1)
    valid = (n * CMP_STRIDE + CMP_BLOCK - 1) <= t
    kc = kc_ref[0, 0]
    vc = vc_ref[0, 0]
    p_grp = jnp.zeros((tq, ncol), F32)
    for hh in range(HPG):
        sl = slice(hh * HEAD_DIM, (hh + 1) * HEAD_DIM)
        sc = jnp.where(valid, _dot_nt(q_ref[0, :, sl], kc), NEG_INF)
        e = jnp.exp2(sc - jnp.max(sc, axis=-1, keepdims=True))
        p = jnp.where(valid, e / jnp.sum(e, axis=-1, keepdims=True), 0.0)
        o_ref[0, :, sl] = _dot(p.astype(BF16), vc)
        p_grp = p_grp + p
    score = _dot_hi(p_grp, smat_ref[...])
    n_sel = sel_ref.shape[2]
    score = jnp.concatenate([score[r:r + LANES].T for r in range(0, tq, LANES)], axis=1)[:n_sel]
    n = lax.broadcasted_iota(jnp.int32, (n_sel, tq), 0)
    t = pl.program_id(2) * tq + lax.broadcasted_iota(jnp.int32, (n_sel, tq), 1)
    cur = t >> int(math.log2(SEL_BLOCK))
    forced = (n == 0) | (n == cur) | (n == cur - 1)
    future = n * SEL_BLOCK > t
    score = jnp.where(forced, jnp.inf, jnp.where(future, -jnp.inf, score))
    rank = jnp.zeros((n_sel, tq), jnp.int32)
    for kk in range(n_sel):
        ck = score[kk:kk + 1, :]
        ahead = (ck > score) | ((ck == score) & (kk < n))
        rank = rank + ahead.astype(jnp.int32)
    sel_ref[0, 0] = (rank < N_SELECT).astype(sel_ref.dtype)


def _sel_matrix(ncol, n_sel):
    rs = SEL_BLOCK // CMP_STRIDE
    rc = CMP_BLOCK // CMP_STRIDE
    mat = [[0.0] * ncol for _ in range(ncol)]
    for j in range(n_sel):
        for m in range(rs):
            for n in range(rc):
                i = rs * j + m + n - (rc - 1)
                if 0 <= i < ncol - 1:
                    mat[i][j] += 1.0
    return jnp.array(mat, F32)


def _cmp_attention(q_r, k_cmp, v_cmp, *, tq):
    bsz, s, _ = q_r.shape
    ncol = k_cmp.shape[2]
    n_sel = s // SEL_BLOCK
    gw = HPG * HEAD_DIM
    return pl.pallas_call(
        _cmp_attn_kernel,
        grid=(bsz, NSA_KV_HEADS, s // tq),
        in_specs=[pl.BlockSpec((1, tq, gw), lambda b, g, i: (b, i, g)),
                  pl.BlockSpec((1, 1, ncol, HEAD_DIM), lambda b, g, i: (b, g, 0, 0)),
                  pl.BlockSpec((1, 1, ncol, HEAD_DIM), lambda b, g, i: (b, g, 0, 0)),
                  pl.BlockSpec((ncol, ncol), lambda b, g, i: (0, 0))],
        out_specs=[pl.BlockSpec((1, tq, gw), lambda b, g, i: (b, i, g)),
                   pl.BlockSpec((1, 1, n_sel, tq), lambda b, g, i: (b, g, 0, i))],
        out_shape=[jax.ShapeDtypeStruct((bsz, s, NSA_HEADS * HEAD_DIM), F32),
                   jax.ShapeDtypeStruct((bsz, NSA_KV_HEADS, n_sel, s), F32)],
        compiler_params=_params("parallel", "parallel", "parallel"),
        name="cmp_attention",
    )(q_r, k_cmp, v_cmp, _sel_matrix(ncol, n_sel))


def _softmax_pv_t(scores, values_t):
    m = None
    for sc in scores:
        cm = jnp.max(sc, axis=0, keepdims=True)
        m = cm if m is None else jnp.maximum(m, cm)
    l = None
    acc = None
    for sc, v_t in zip(scores, values_t):
        p = jnp.exp2(sc - m)
        ps = jnp.sum(p, axis=0, keepdims=True)
        pv = _dot(v_t, p.astype(BF16))
        l = ps if l is None else l + ps
        acc = pv if acc is None else acc + pv
    return acc / l


MASK_BIG = 2.0 ** 100
SLC_VARIANT_CHUNKS = 2


def _slc_win_kernel(q_ref, ks_ref, vst_ref, kw_ref, vwt_ref, selt_ref, blk_ref, oc_ref, gate_ref, o_ref,
                    gt_s, os_s):
    tq = q_ref.shape[1]
    tk = vst_ref.shape[4]
    n_chunks = vst_ref.shape[2]
    n_sel = selt_ref.shape[2]
    g = pl.program_id(1)
    t0 = pl.program_id(2) * tq
    q_t = jnp.concatenate([q_ref[0, :, hh * HEAD_DIM:(hh + 1) * HEAD_DIM].astype(F32).T.astype(BF16)
                           for hh in range(HPG)], axis=1)

    key_loc = lax.broadcasted_iota(jnp.int32, (tq, tq), 0)
    qry_loc = lax.broadcasted_iota(jnp.int32, (tq, tq), 1)
    n_win = WINDOW // tq + 1
    scores, values = [], []
    for cc in range(n_win):
        k0 = t0 - WINDOW + cc * tq
        k0c = pl.multiple_of(jnp.maximum(k0, 0), tq)
        sc = _dot(kw_ref[0, pl.ds(k0c, tq), :], q_t)
        if cc == 0:
            ok = jnp.where(k0 >= 0, key_loc - qry_loc, 0) > 0
            sc = jnp.where(jnp.concatenate([ok] * HPG, axis=1), sc, NEG_INF)
        elif cc == n_win - 1:
            sc = jnp.where(jnp.concatenate([key_loc <= qry_loc] * HPG, axis=1), sc, NEG_INF)
        else:
            sc = sc + jnp.where(k0 >= 0, 0.0, NEG_INF)
        scores.append(sc)
        values.append(vwt_ref[0, 0, k0c // tq])
    o_win_t = _softmax_pv_t(scores, values)

    sel_bias = ((selt_ref[0, 0] - 1.0) * MASK_BIG).astype(BF16)
    q_aug = jnp.concatenate([q_t, jnp.concatenate([sel_bias] * HPG, axis=1),
                             jnp.zeros((HEAD_DIM - n_sel, HPG * tq), BF16)], axis=0)
    k_loc = lax.broadcasted_iota(jnp.int32, (tk, tq), 0)
    t_loc = lax.broadcasted_iota(jnp.int32, (tk, tq), 1)

    def selected(n_used):
        scores, values = [], []
        for cc in range(n_used):
            rows = slice(cc * tk, (cc + 1) * tk)
            k_aug = jnp.concatenate([ks_ref[0, rows, :], blk_ref[rows, :]], axis=1)
            sc = _dot(k_aug, q_aug)
            if cc >= n_used - SLC_VARIANT_CHUNKS:
                causal = (cc * tk + k_loc) <= (t0 + t_loc)
                sc = jnp.where(jnp.concatenate([causal] * HPG, axis=1), sc, NEG_INF)
            scores.append(sc)
            values.append(vst_ref[0, 0, cc])
        os_s[...] = _softmax_pv_t(scores, values)

    variant = t0 // (SLC_VARIANT_CHUNKS * tk)
    for vv in range(n_chunks // SLC_VARIANT_CHUNKS):
        pl.when(variant == vv)(functools.partial(selected, (vv + 1) * SLC_VARIANT_CHUNKS))

    gt = _sigmoid(gate_ref[0])
    gt_s[...] = gt.T
    lane = lax.broadcasted_iota(jnp.int32, gt.shape, 1)
    for hh in range(HPG):
        base = (g * HPG + hh) * 3
        g_cmp = jnp.sum(jnp.where(lane == base, gt, 0.0), axis=1, keepdims=True)
        cs = slice(hh * tq, (hh + 1) * tq)
        mix_t = gt_s[pl.ds(base + 1, 1), :] * os_s[:, cs] + gt_s[pl.ds(base + 2, 1), :] * o_win_t[:, cs]
        sl = slice(hh * HEAD_DIM, (hh + 1) * HEAD_DIM)
        o_ref[0, :, sl] = (g_cmp * oc_ref[0, :, sl] + mix_t.T).astype(o_ref.dtype)


def _slc_win_attention(q_r, ks, vs_t, kw, vw_t, sel_t, o_cmp, gates, *, tq):
    bsz, s, _ = q_r.shape
    n_sel = s // SEL_BLOCK
    gw = HPG * HEAD_DIM
    assert vw_t.shape[4] == tq and tq == LANES and vs_t.shape[2] % SLC_VARIANT_CHUNKS == 0
    kv_spec = pl.BlockSpec((1, s, HEAD_DIM), lambda b, g, i: (b, 0, g))
    vt_spec = lambda a: pl.BlockSpec((1, 1) + a.shape[2:], lambda b, g, i: (b, g, 0, 0, 0))
    block_onehot = (jnp.arange(s)[:, None] // SEL_BLOCK == jnp.arange(HEAD_DIM)[None, :]).astype(BF16)
    return pl.pallas_call(
        _slc_win_kernel,
        grid=(bsz, NSA_KV_HEADS, s // tq),
        in_specs=[pl.BlockSpec((1, tq, gw), lambda b, g, i: (b, i, g)),
                  kv_spec, vt_spec(vs_t), kv_spec, vt_spec(vw_t),
                  pl.BlockSpec((1, 1, n_sel, tq), lambda b, g, i: (b, g, 0, i)),
                  pl.BlockSpec((s, HEAD_DIM), lambda b, g, i: (0, 0)),
                  pl.BlockSpec((1, tq, gw), lambda b, g, i: (b, i, g)),
                  pl.BlockSpec((1, tq, LANES), lambda b, g, i: (b, i, 0))],
        out_specs=pl.BlockSpec((1, tq, gw), lambda b, g, i: (b, i, g)),
        out_shape=jax.ShapeDtypeStruct((bsz, s, NSA_HEADS * HEAD_DIM), BF16),
        scratch_shapes=[pltpu.VMEM((LANES, tq), F32), pltpu.VMEM((HEAD_DIM, HPG * tq), F32)],
        compiler_params=_params("parallel", "parallel", "arbitrary"),
        name="slc_win_attention",
    )(q_r, ks, vs_t, kw, vw_t, sel_t, block_onehot, o_cmp, gates)


FFN_OUT_K_STEPS = 4


def _pad_cols(w, n):
    return jnp.pad(w, ((0, 0), (0, n - w.shape[1])))


def _conv_deltanet_mixer(xb, bsz, s, w_in, sc_conv_w, dn_conv_w, a_log, dt_bias, norm_w, w_out):
    sc_w = sc_conv_w.shape[1]
    dn_w = dn_conv_w.shape[1] // 3
    n_heads = dn_w // HEAD_DIM
    main = 3 * sc_w + 4 * dn_w
    proj, ba = _in_proj(xb, w_in.astype(BF16), main, _pad_cols(w_in[:, main:], LANES).astype(BF16),
                        tm=1024, tn=1024)
    proj = proj.reshape(bsz, s, main)
    y_sc = _short_conv(proj, sc_conv_w, sc_w, tc=256)
    y_dn = _deltanet(proj, 3 * sc_w, 3 * sc_w + 3 * dn_w, dn_conv_w, ba.reshape(bsz, s, LANES), a_log, dt_bias,
                     norm_w, n_heads)
    wo = w_out.astype(BF16)
    return [y_sc.reshape(bsz * s, sc_w), y_dn.reshape(bsz * s, dn_w)], [wo[:sc_w], wo[sc_w:]]


def _nsa_mixer(xb, bsz, s, positions, w_in, cmp_pos_k, cmp_w1_k, cmp_w2_k, cmp_pos_v, cmp_w1_v, cmp_w2_v, w_out):
    qw = NSA_HEADS * HEAD_DIM
    kvw = NSA_KV_HEADS * HEAD_DIM
    main = qw + 6 * kvw
    proj, gates = _in_proj(xb, w_in.astype(BF16), main, _pad_cols(w_in[:, main:], LANES).astype(BF16),
                           tm=1024, tn=1024)
    proj = proj.reshape(bsz, s, main)
    half = HEAD_DIM // 2
    inv = jnp.power(ROPE_THETA, -jnp.arange(half, dtype=F32) / half)
    inv = jnp.concatenate([inv, inv])
    ang = positions.astype(F32)[..., None] * inv
    cmp_end = jnp.minimum(jnp.arange(s // CMP_STRIDE) * CMP_STRIDE + CMP_BLOCK - 1, s - 1)
    ang_cmp = positions[:, cmp_end].astype(F32)[..., None] * inv
    q_r, ks, kw, vs_t, vw_t = _rope_qkv(proj, ang, ts=512, slc_chunk=256, win_chunk=LANES)
    k_cmp = _compress(proj, qw, cmp_pos_k, cmp_w1_k, cmp_w2_k, ang_cmp, rope=True)
    v_cmp = _compress(proj, qw + kvw, cmp_pos_v, cmp_w1_v, cmp_w2_v, ang_cmp, rope=False)
    o_cmp, sel_t = _cmp_attention(q_r, k_cmp, v_cmp, tq=256)
    o = _slc_win_attention(q_r, ks, vs_t, kw, vw_t, sel_t, o_cmp, gates.reshape(bsz, s, LANES), tq=LANES)
    return [o.reshape(bsz * s, qw)], [w_out.astype(BF16)]


def kernel(x, positions, ln_mix_g, ln_mix_b, ln_ffn_g, ln_ffn_b, ffn_w_in, ffn_w_out, hy_w_in, sc_conv_w, dn_conv_w, dn_a_log, dn_dt_bias, dn_norm_w, hy_w_out, nsa_w_in, cmp_pos_k, cmp_w1_k, cmp_w2_k, cmp_pos_v, cmp_w1_v, cmp_w2_v, nsa_w_out):
    bsz, s, d = x.shape
    xf = x.reshape(bsz * s, d)
    xb = xf
    ffn_w_out_b = ffn_w_out.astype(BF16)
    for i in range(DEPTH):
        j = i // 2
        if i % 2 == 0:
            ys, wos = _conv_deltanet_mixer(xb, bsz, s, hy_w_in[j], sc_conv_w[j], dn_conv_w[j], dn_a_log[j],
                                           dn_dt_bias[j], dn_norm_w[j], hy_w_out[j])
        else:
            ys, wos = _nsa_mixer(xb, bsz, s, positions, nsa_w_in[j], cmp_pos_k[j], cmp_w1_k[j], cmp_w2_k[j],
                                 cmp_pos_v[j], cmp_w1_v[j], cmp_w2_v[j], nsa_w_out[j])
        xf, xb = _matmul_ln(ys, wos, xf, ln_mix_g[i], ln_mix_b[i], tm=512, nk=1)
        hmid = _ffn_in(xb, ffn_w_in, i, tm=1024, tn=512)
        xf, xb = _matmul_ln([hmid], [ffn_w_out_b], xf, ln_ffn_g[i], ln_ffn_b[i], tm=512, nk=FFN_OUT_K_STEPS,
                            layer=i)
    return xf.reshape(bsz, s, d)
```

```python
import functools
import math

import jax
import jax.numpy as jnp
from jax import lax
from jax.experimental import pallas as pl
from jax.experimental.pallas import tpu as pltpu

F32 = jnp.float32
BF16 = jnp.bfloat16
HIGHEST = lax.Precision.HIGHEST

LANES = 128
VMEM_LIMIT = 48 * 1024 * 1024
VMEM_LIMIT_BIG = 56 * 1024 * 1024

DN_HEADS = 8
DN_CHUNK = 64
DN_CONV = 4
SC_KERNEL = 3
NSA_HEADS = 16
NSA_KV_HEADS = 4
HPG = NSA_HEADS // NSA_KV_HEADS
HEAD_DIM = 128
CMP_BLOCK = 32
CMP_STRIDE = 16
SEL_BLOCK = 64
N_SELECT = 16
WINDOW = 512
ROPE_THETA = 10000.0
LN_EPS = 1e-5
NORM_EPS = 1e-6
NEG_INF = -1e30
DEPTH = 2
ALPHA = (2 * DEPTH) ** 0.25
ATTN_SCALE = HEAD_DIM ** -0.5
Q_SCALE = ATTN_SCALE * math.log2(math.e)


def _params(*sem, vmem=VMEM_LIMIT):
    return pltpu.CompilerParams(dimension_semantics=sem, vmem_limit_bytes=vmem)


def _sigmoid(x):
    return 1.0 / (1.0 + jnp.exp(-x))


def _silu(x):
    return x * _sigmoid(x)


def _dot(a, b):
    return jnp.dot(a, b, preferred_element_type=F32)


def _dot_nt(a, b):
    return lax.dot_general(a, b, (((1,), (1,)), ((), ())), preferred_element_type=F32)


def _dot_tn(a, b):
    return lax.dot_general(a, b, (((0,), (0,)), ((), ())), preferred_element_type=F32)


def _dot_hi(a, b):
    return jnp.dot(a, b, precision=HIGHEST, preferred_element_type=F32)


def _proj_kernel(x_ref, w_ref, ws_ref, o_ref, os_ref):
    xb = x_ref[...].astype(BF16)
    o_ref[...] = _dot(xb, w_ref[...])

    @pl.when(pl.program_id(1) == 0)
    def _():
        os_ref[...] = _dot(xb, ws_ref[...])


def _in_proj(x, w, n, w_side, *, tm, tn):
    m, k = x.shape
    ns = w_side.shape[1]
    return pl.pallas_call(
        _proj_kernel,
        grid=(m // tm, n // tn),
        in_specs=[pl.BlockSpec((tm, k), lambda i, j: (i, 0)),
                  pl.BlockSpec((k, tn), lambda i, j: (0, j)),
                  pl.BlockSpec((k, ns), lambda i, j: (0, 0))],
        out_specs=[pl.BlockSpec((tm, tn), lambda i, j: (i, j)),
                   pl.BlockSpec((tm, ns), lambda i, j: (i, 0))],
        out_shape=[jax.ShapeDtypeStruct((m, n), F32), jax.ShapeDtypeStruct((m, ns), F32)],
        compiler_params=_params("parallel", "arbitrary"),
        name="in_proj",
    )(x, w, w_side)


def _ffn_in_kernel(x_ref, wg_ref, wu_ref, o_ref, wgb_s, wub_s):
    @pl.when(pl.program_id(1) == 0)
    def _():
        wgb_s[...] = wg_ref[...].astype(BF16)
        wub_s[...] = wu_ref[...].astype(BF16)

    x = x_ref[...]
    gate = _dot(x, wgb_s[...])
    up = _dot(x, wub_s[...])
    o_ref[...] = (_silu(gate) * up).astype(o_ref.dtype)


def _ffn_in(xb, w_in, layer, *, tm, tn):
    m, k = xb.shape
    hidden = w_in.shape[2] // 2
    nj = hidden // tn
    return pl.pallas_call(
        _ffn_in_kernel,
        grid=(nj, m // tm),
        in_specs=[pl.BlockSpec((tm, k), lambda j, i: (i, 0)),
                  pl.BlockSpec((None, k, tn), lambda j, i: (layer, 0, j)),
                  pl.BlockSpec((None, k, tn), lambda j, i: (layer, 0, j + nj))],
        out_specs=pl.BlockSpec((tm, tn), lambda j, i: (i, j)),
        out_shape=jax.ShapeDtypeStruct((m, hidden), BF16),
        scratch_shapes=[pltpu.VMEM((k, tn), BF16), pltpu.VMEM((k, tn), BF16)],
        compiler_params=_params("parallel", "arbitrary"),
        name="ffn_in",
    )(xb, w_in, w_in)


MM_LN_COLS = 512
MM_LN_ROWS = 128


def _mm_ln_kernel(*refs, n_pairs, nk):
    xs = refs[:n_pairs]
    ws = refs[n_pairs:2 * n_pairs]
    r_ref, g_ref, b_ref, o_ref, ob_ref = refs[2 * n_pairs:2 * n_pairs + 5]
    tm, d = o_ref.shape

    def accumulate(first):
        for n0 in range(0, d, MM_LN_COLS):
            cols = slice(n0, n0 + MM_LN_COLS)
            part = _dot(xs[0][...], ws[0][:, cols])
            for x_ref, w_ref in zip(xs[1:], ws[1:]):
                part = part + _dot(x_ref[...], w_ref[:, cols])
            if first:
                o_ref[:, cols] = part
            else:
                o_ref[:, cols] += part

    def finish():
        def rows_body(r, carry):
            rows = pl.ds(pl.multiple_of(r * MM_LN_ROWS, MM_LN_ROWS), MM_LN_ROWS)
            v = ALPHA * r_ref[rows, :] + o_ref[rows, :]
            mu = jnp.mean(v, axis=-1, keepdims=True)
            dv = v - mu
            var = jnp.mean(dv * dv, axis=-1, keepdims=True)
            out = dv * lax.rsqrt(var + LN_EPS) * g_ref[...] + b_ref[...]
            o_ref[rows, :] = out
            ob_ref[rows, :] = out.astype(BF16)
            return carry

        lax.fori_loop(0, tm // MM_LN_ROWS, rows_body, 0)

    if nk == 1:
        accumulate(True)
        finish()
    else:
        kk = pl.program_id(1)

        @pl.when(kk == 0)
        def _():
            accumulate(True)

        @pl.when(kk > 0)
        def _():
            accumulate(False)

        @pl.when(kk == nk - 1)
        def _():
            finish()


def _matmul_ln(xs, ws, resid, g, b, *, tm, nk, layer=None):
    m, d = resid.shape
    resid_mode = {} if nk == 1 else {"pipeline_mode": pl.Buffered(1)}
    n_pairs = len(xs)
    tks = [x.shape[1] // nk for x in xs]
    if layer is None:
        w_specs = [pl.BlockSpec((tk, d), lambda i, kk: (kk, 0)) for tk in tks]
    else:
        w_specs = [pl.BlockSpec((None, tk, d), lambda i, kk: (layer, kk, 0)) for tk in tks]
    in_specs = ([pl.BlockSpec((tm, tk), lambda i, kk: (i, kk)) for tk in tks]
                + w_specs
                + [pl.BlockSpec((tm, d), lambda i, kk: (i, 0), **resid_mode),
                   pl.BlockSpec((1, d), lambda i, kk: (0, 0)),
                   pl.BlockSpec((1, d), lambda i, kk: (0, 0))])
    return pl.pallas_call(
        functools.partial(_mm_ln_kernel, n_pairs=n_pairs, nk=nk),
        grid=(m // tm, nk),
        in_specs=in_specs,
        out_specs=[pl.BlockSpec((tm, d), lambda i, kk: (i, 0)),
                   pl.BlockSpec((tm, d), lambda i, kk: (i, 0))],
        out_shape=[jax.ShapeDtypeStruct((m, d), F32), jax.ShapeDtypeStruct((m, d), BF16)],
        compiler_params=_params("parallel", "arbitrary", vmem=VMEM_LIMIT_BIG if nk > 1 else VMEM_LIMIT),
        name="matmul_ln",
    )(*xs, *ws, resid, g.reshape(1, d), b.reshape(1, d))


def _causal_conv(u, w_ref, taps):
    def tap_sum(x, shift):
        acc = x * w_ref[taps - 1:taps, :]
        for sh in range(1, taps):
            acc = acc + shift(x, sh) * w_ref[taps - 1 - sh:taps - sh, :]
        return acc

    body = tap_sum(u, lambda x, sh: pltpu.roll(x, sh, axis=0))
    row = lax.broadcasted_iota(jnp.int32, (8, u.shape[1]), 0)
    head = tap_sum(u[0:8], lambda x, sh: jnp.where(row >= sh, pltpu.roll(x, sh, axis=0), 0.0))
    return jnp.concatenate([head, body[8:]], axis=0)


def _sc_kernel(b_ref, c_ref, h_ref, w_ref, o_ref):
    u = c_ref[0] * h_ref[0]
    o_ref[0] = (b_ref[0] * _causal_conv(u, w_ref, SC_KERNEL)).astype(o_ref.dtype)


def _short_conv(proj, conv_w, width, *, tc):
    bsz, s, _ = proj.shape
    nb = width // tc
    w = jnp.zeros((8, width), F32).at[:SC_KERNEL].set(conv_w)
    return pl.pallas_call(
        _sc_kernel,
        grid=(bsz, nb),
        in_specs=[pl.BlockSpec((1, s, tc), lambda b, j: (b, 0, j)),
                  pl.BlockSpec((1, s, tc), lambda b, j: (b, 0, j + nb)),
                  pl.BlockSpec((1, s, tc), lambda b, j: (b, 0, j + 2 * nb)),
                  pl.BlockSpec((8, tc), lambda b, j: (0, j))],
        out_specs=pl.BlockSpec((1, s, tc), lambda b, j: (b, 0, j)),
        out_shape=jax.ShapeDtypeStruct((bsz, s, width), BF16),
        compiler_params=_params("parallel", "parallel"),
        name="short_conv",
    )(proj, proj, proj, w)


def _split(x):
    hi = x.astype(BF16)
    return hi, (x - hi.astype(F32)).astype(BF16)


def _dotb(a, b):
    return _dot(a.astype(BF16), b.astype(BF16))


DN_GROUP = 8


def _dn_group_local(base, scr, masks, out):
    q_s, k_s, kb_s, qd_s, kf_s, kbe_s, vb_s, gc_s, gcd_s = scr
    incl, strict, m8, m16, eye = masks
    c = DN_CHUNK
    idx = range(DN_GROUP)
    rows = [pl.ds(base + cc * c, c) for cc in idx]
    gc = [gc_s[r, :] for r in rows]
    decay = []
    for cc in idx:
        gc_j = gcd_s[pl.ds(base // LANES + cc // 2, 1), (cc % 2) * c:(cc % 2) * c + c]
        decay.append(jnp.where(incl, jnp.exp(jnp.where(incl, gc[cc][:, :c] - gc_j, 0.0)), 0.0))
    kbf = [k_s[r, :] for r in rows]
    kk = [_dot_nt(kb_s[rows[cc], :], kbf[cc]) for cc in idx]
    qk = [_dot_nt(q_s[rows[cc], :], kbf[cc]) for cc in idx]
    yield
    a = [jnp.where(strict, kk[cc] * decay[cc], 0.0) for cc in idx]
    intra = [(qk[cc] * decay[cc]).astype(BF16) for cc in idx]
    ad = [jnp.where(m8, x, 0.0) for x in a]
    adb = [x.astype(BF16) for x in ad]
    a2 = [_dot(x, x) for x in adb]
    yield
    a2b = [x.astype(BF16) for x in a2]
    p = [eye - x for x in ad]
    p1, a4 = [], []
    for cc in idx:
        p1.append(p[cc] + _dotb(p[cc], a2b[cc]))
        a4.append(_dot(a2b[cc], a2b[cc]))
    yield
    p2 = [p1[cc] + _dotb(p1[cc], a4[cc]) for cc in idx]
    yield
    pb = [x.astype(BF16) for x in p2]
    t = [_dotb(pb[cc], jnp.where(m16, a[cc] - ad[cc], 0.0)) for cc in idx]
    yield
    dinv = [(p2[cc] - _dotb(t[cc], pb[cc])).astype(BF16) for cc in idx]
    yield
    db, da = [], []
    for cc in idx:
        rhs = jnp.concatenate([vb_s[rows[cc], :], kbe_s[rows[cc], :]], axis=1)
        db.append(_dotb(dinv[cc], rhs))
        da.append(_dotb(dinv[cc], jnp.where(m16, 0.0, a[cc])).astype(BF16))
    yield
    blocks = [[x[0:16]] for x in db]
    for s4 in range(1, c // 16):
        rs = slice(16 * s4, 16 * s4 + 16)
        for cc in idx:
            xprev = jnp.concatenate(blocks[cc] + [jnp.zeros((c - 16 * s4, 2 * HEAD_DIM), F32)], axis=0)
            blocks[cc].append(db[cc][rs] - _dotb(da[cc][rs], xprev))
        yield
    for cc in idx:
        sol = jnp.concatenate(blocks[cc], axis=0)
        sol_hi, sol_lo = _split(sol)
        g_last = gc[cc][c - 1:c, :]
        k_dec_t = (kf_s[rows[cc], :] * jnp.exp(g_last - gc[cc])).T.astype(BF16)
        kw = _dot(k_dec_t, sol_hi) + _dot(k_dec_t, sol_lo)
        iw = _dot(intra[cc], sol_hi) + _dot(intra[cc], sol_lo)
        out.append((kw[:, HEAD_DIM:].astype(BF16), kw[:, :HEAD_DIM],
                    (qd_s[rows[cc], :] - iw[:, HEAD_DIM:]).astype(BF16), iw[:, :HEAD_DIM], jnp.exp(g_last)))
    yield


def _dn_chunk_seq(state, loc, z, nw):
    w2, n_mat, qp, op, eg_last = loc
    sb = state.astype(BF16)
    o = _dot(qp, sb) + op
    state = (state * eg_last - _dot(w2, sb)) + n_mat
    o = o * lax.rsqrt(jnp.mean(o * o, axis=-1, keepdims=True) + NORM_EPS) * nw * _silu(z)
    return state, o


def _dn_kernel(qp_ref, kp_ref, vp_ref, z_ref, ba_ref, arow_ref, dtrow_ref, cwq_ref, cwk_ref, cwv_ref, nw_ref, o_ref,
               q_s, k_s, kb_s, qd_s, kf_s, kbe_s, vb_s, gc_s, beta_s, gates_s, gcd_s, *, n_heads):
    h = pl.program_id(1)
    s = qp_ref.shape[1]
    c = DN_CHUNK
    @pl.when(h == 0)
    def _():
        ba = ba_ref[0]
        xa = ba + dtrow_ref[...]
        softplus = jnp.maximum(xa, 0.0) + jnp.log(1.0 + jnp.exp(-jnp.abs(xa)))
        lane0 = lax.broadcasted_iota(jnp.int32, ba.shape, 1)
        gates = jnp.where(lane0 < n_heads, _sigmoid(ba), -jnp.exp(arow_ref[...]) * softplus)
        for blk in range(s // LANES):
            rs = slice(blk * LANES, (blk + 1) * LANES)
            gates_s[rs, :] = gates[rs, :].T

    n_blk = s // LANES
    beta_d = jnp.concatenate([gates_s[pl.ds(blk * LANES + h, 1), :] for blk in range(n_blk)], axis=0)
    gc_d = jnp.concatenate([gates_s[pl.ds(blk * LANES + h + n_heads, 1), :] for blk in range(n_blk)], axis=0)
    pos = lax.broadcasted_iota(jnp.int32, (n_blk, LANES), 1) & (c - 1)
    sh = 1
    while sh < c:
        gc_d = gc_d + jnp.where(pos >= sh, pltpu.roll(gc_d, sh, axis=1), 0.0)
        sh *= 2
    gcd_s[0:n_blk, :] = gc_d
    for blk in range(n_blk):
        rs = slice(blk * LANES, (blk + 1) * LANES)
        gc_s[rs, :] = jnp.broadcast_to(gc_d[blk:blk + 1, :], (LANES, LANES)).T
        beta_s[rs, :] = jnp.broadcast_to(beta_d[blk:blk + 1, :], (LANES, LANES)).T
    gc = gc_s[...]
    beta = beta_s[...]
    eg = jnp.exp(gc)
    q = _silu(_causal_conv(qp_ref[0], cwq_ref, DN_CONV))
    q = q * (lax.rsqrt(jnp.sum(q * q, axis=-1, keepdims=True) + NORM_EPS) * (HEAD_DIM ** -0.5))
    q_s[...] = q.astype(BF16)
    qd_s[...] = q * eg
    k = _silu(_causal_conv(kp_ref[0], cwk_ref, DN_CONV))
    k = k * lax.rsqrt(jnp.sum(k * k, axis=-1, keepdims=True) + NORM_EPS)
    kb = k * beta
    kf_s[...] = k
    k_s[...] = k.astype(BF16)
    kb_s[...] = kb.astype(BF16)
    kbe_s[...] = kb * eg
    vb_s[...] = _silu(_causal_conv(vp_ref[0], cwv_ref, DN_CONV)) * beta

    row = lax.broadcasted_iota(jnp.int32, (c, c), 0)
    col = lax.broadcasted_iota(jnp.int32, (c, c), 1)
    masks = (row >= col, row > col, (row >> 3) == (col >> 3), (row >> 4) == (col >> 4), (row == col).astype(F32))
    scr = (q_s, k_s, kb_s, qd_s, kf_s, kbe_s, vb_s, gc_s, gcd_s)
    nw = nw_ref[...]
    rows_per_group = DN_GROUP * c
    n_groups = s // rows_per_group

    def group_base(gi):
        base = gi * rows_per_group
        return base if isinstance(base, int) else pl.multiple_of(base, rows_per_group)

    def run(gi_local, gi_seq, state, locs):
        nxt = []
        stages = iter(()) if gi_local is None else _dn_group_local(group_base(gi_local), scr, masks, nxt)
        todo = list(range(DN_GROUP)) if gi_seq is not None else []
        done = False
        while todo or not done:
            if not done:
                done = next(stages, "end") == "end"
            if todo:
                cc = todo.pop(0)
                rows = pl.ds(group_base(gi_seq) + cc * c, c)
                state, o = _dn_chunk_seq(state, locs[cc], z_ref[0, rows, :], nw)
                o_ref[0, rows, :] = o.astype(o_ref.dtype)
        return state, tuple(nxt)

    def body(gi, carry):
        return run(gi + 1, gi, *carry)

    carry = run(0, None, jnp.zeros((HEAD_DIM, HEAD_DIM), F32), None)
    carry = lax.fori_loop(0, n_groups - 1, body, carry)
    run(None, n_groups - 1, *carry)


def _deltanet(proj, qkv_col0, z_col0, conv_w, ba, a_log, dt_bias, norm_w, n_heads):
    bsz, s, _ = proj.shape
    qb0 = qkv_col0 // HEAD_DIM
    zb0 = z_col0 // HEAD_DIM
    arow = jnp.zeros((1, LANES), F32).at[0, n_heads:2 * n_heads].set(a_log)
    dtrow = jnp.zeros((1, LANES), F32).at[0, n_heads:2 * n_heads].set(dt_bias)
    cw = jnp.zeros((8, 3 * n_heads * HEAD_DIM), F32).at[:DN_CONV].set(conv_w)
    blk = (1, s, HEAD_DIM)
    col_spec = lambda off: pl.BlockSpec(blk, lambda b, h, off=off: (b, 0, h + off))
    cw_spec = lambda off: pl.BlockSpec((8, HEAD_DIM), lambda b, h, off=off: (0, h + off))
    row_spec = pl.BlockSpec((1, LANES), lambda b, h: (0, 0))
    return pl.pallas_call(
        functools.partial(_dn_kernel, n_heads=n_heads),
        grid=(bsz, n_heads),
        in_specs=[col_spec(qb0), col_spec(qb0 + n_heads), col_spec(qb0 + 2 * n_heads), col_spec(zb0),
                  pl.BlockSpec((1, s, LANES), lambda b, h: (b, 0, 0)), row_spec, row_spec,
                  cw_spec(0), cw_spec(n_heads), cw_spec(2 * n_heads), row_spec],
        out_specs=pl.BlockSpec(blk, lambda b, h: (b, 0, h)),
        out_shape=jax.ShapeDtypeStruct((bsz, s, n_heads * HEAD_DIM), BF16),
        scratch_shapes=([pltpu.VMEM((s, HEAD_DIM), BF16)] * 3 + [pltpu.VMEM((s, HEAD_DIM), F32)] * 7
                        + [pltpu.VMEM((max(8, s // LANES), LANES), F32)]),
        compiler_params=_params("parallel", "arbitrary"),
        name="deltanet",
    )(proj, proj, proj, proj, ba, arow, dtrow, cw, cw, cw, norm_w.reshape(1, HEAD_DIM))


def _rope_tables(ang):
    lane = lax.broadcasted_iota(jnp.int32, ang.shape, 1)
    sin = jnp.sin(ang)
    return jnp.cos(ang), jnp.where(lane < HEAD_DIM // 2, -sin, sin)


def _rope(x, cos, sin_signed):
    return x * cos + pltpu.roll(x, HEAD_DIM // 2, axis=1) * sin_signed


def _rope_kernel(ang_ref, q_ref, ks_ref, kw_ref, vs_ref, vw_ref, qo_ref, kso_ref, kwo_ref, vso_ref, vwo_ref):
    cos, sin = _rope_tables(ang_ref[0])
    for hh in range(NSA_HEADS):
        sl = slice(hh * HEAD_DIM, (hh + 1) * HEAD_DIM)
        qo_ref[0, :, sl] = (_rope(q_ref[0, :, sl], cos, sin) * Q_SCALE).astype(BF16)
    for g in range(NSA_KV_HEADS):
        sl = slice(g * HEAD_DIM, (g + 1) * HEAD_DIM)
        kso_ref[0, :, sl] = _rope(ks_ref[0, :, sl], cos, sin).astype(BF16)
        kwo_ref[0, :, sl] = _rope(kw_ref[0, :, sl], cos, sin).astype(BF16)
    ts = vs_ref.shape[1]
    for v_ref, vo_ref in ((vs_ref, vso_ref), (vw_ref, vwo_ref)):
        ck = vo_ref.shape[4]
        for g in range(NSA_KV_HEADS):
            for cc in range(ts // ck):
                parts = [v_ref[0, cc * ck + r:cc * ck + r + LANES, g * HEAD_DIM:(g + 1) * HEAD_DIM].T
                         for r in range(0, ck, LANES)]
                vo_ref[0, g, cc] = jnp.concatenate(parts, axis=1).astype(BF16)


def _rope_qkv(proj, ang, *, ts, slc_chunk, win_chunk):
    bsz, s, _ = proj.shape
    qw = NSA_HEADS * HEAD_DIM
    kvw = NSA_KV_HEADS * HEAD_DIM
    kv_spec = lambda blk: pl.BlockSpec((1, ts, kvw), lambda b, i, blk=blk: (b, i, blk))
    kv_out = pl.BlockSpec((1, ts, kvw), lambda b, i: (b, i, 0))
    kv_shape = jax.ShapeDtypeStruct((bsz, s, kvw), BF16)
    vt_out = lambda ck: pl.BlockSpec((1, NSA_KV_HEADS, ts // ck, HEAD_DIM, ck), lambda b, i: (b, 0, i, 0, 0))
    vt_shape = lambda ck: jax.ShapeDtypeStruct((bsz, NSA_KV_HEADS, s // ck, HEAD_DIM, ck), BF16)
    base = qw // kvw
    return pl.pallas_call(
        _rope_kernel,
        grid=(bsz, s // ts),
        in_specs=[pl.BlockSpec((1, ts, HEAD_DIM), lambda b, i: (b, i, 0)),
                  pl.BlockSpec((1, ts, qw), lambda b, i: (b, i, 0)),
                  kv_spec(base + 2), kv_spec(base + 4), kv_spec(base + 3), kv_spec(base + 5)],
        out_specs=[pl.BlockSpec((1, ts, qw), lambda b, i: (b, i, 0)), kv_out, kv_out,
                   vt_out(slc_chunk), vt_out(win_chunk)],
        out_shape=[jax.ShapeDtypeStruct((bsz, s, qw), BF16), kv_shape, kv_shape,
                   vt_shape(slc_chunk), vt_shape(win_chunk)],
        compiler_params=_params("parallel", "parallel"),
        name="rope_qkv",
    )(ang, proj, proj, proj, proj, proj)


def _gelu_tanh(x):
    return x * (0.5 * (1.0 + jnp.tanh(math.sqrt(2.0 / math.pi) * (x + 0.044715 * (x * x * x)))))


def _compress_kernel(x_ref, w1_ref, w2_ref, pos_ref, ang_ref, o_ref, *, rope):
    nsub = x_ref.shape[1] // CMP_STRIDE
    hid = w1_ref.shape[1]
    pa = jnp.zeros((nsub, hid), F32)
    pb = jnp.zeros((nsub, hid), F32)
    for l in range(CMP_STRIDE):
        xl = x_ref[0, pl.ds(l, nsub, stride=CMP_STRIDE), :].astype(BF16)
        pa = pa + _dot(xl, w1_ref[l * HEAD_DIM:(l + 1) * HEAD_DIM, :])
        pb = pb + _dot(xl, w1_ref[(CMP_STRIDE + l) * HEAD_DIM:(CMP_STRIDE + l + 1) * HEAD_DIM, :])
    bias = _dot(pos_ref[...], w1_ref[...])[0:1, :]
    hpre = pa + pltpu.roll(pb, nsub - 1, axis=0) + bias
    out = _dot(_gelu_tanh(hpre).astype(BF16), w2_ref[...])
    if rope:
        cos, sin = _rope_tables(ang_ref[0])
        out = _rope(out, cos, sin)
    o_ref[0, 0] = out.astype(o_ref.dtype)


def _compress(proj, col0, pos_emb, w1, w2, ang_cmp, *, rope):
    bsz, s, _ = proj.shape
    nsub = s // CMP_STRIDE
    blk0 = col0 // HEAD_DIM
    hid = w1.shape[1]
    pos = jnp.zeros((8, CMP_BLOCK * HEAD_DIM), BF16).at[0].set(pos_emb.reshape(-1).astype(BF16))
    return pl.pallas_call(
        functools.partial(_compress_kernel, rope=rope),
        grid=(bsz, NSA_KV_HEADS),
        in_specs=[pl.BlockSpec((1, s, HEAD_DIM), lambda b, g: (b, 0, g + blk0)),
                  pl.BlockSpec((CMP_BLOCK * HEAD_DIM, hid), lambda b, g: (0, 0)),
                  pl.BlockSpec((hid, HEAD_DIM), lambda b, g: (0, 0)),
                  pl.BlockSpec((8, CMP_BLOCK * HEAD_DIM), lambda b, g: (0, 0)),
                  pl.BlockSpec((1, nsub, HEAD_DIM), lambda b, g: (b, 0, 0))],
        out_specs=pl.BlockSpec((1, 1, nsub, HEAD_DIM), lambda b, g: (b, g, 0, 0)),
        out_shape=jax.ShapeDtypeStruct((bsz, NSA_KV_HEADS, nsub, HEAD_DIM), BF16),
        compiler_params=_params("parallel", "parallel"),
        name="compress",
    )(proj, w1.astype(BF16), w2.astype(BF16), pos, ang_cmp)


def _cmp_attn_kernel(q_ref, kc_ref, vc_ref, smat_ref, o_ref, sel_ref):
    tq = q_ref.shape[1]
    ncol = kc_ref.shape[2]
    t = pl.program_id(2) * tq + lax.broadcasted_iota(jnp.int32, (tq, ncol), 0)
    n = lax.broadcasted_iota(jnp.int32, (tq, ncol), 1)
    valid = (n * CMP_STRIDE + CMP_BLOCK - 1) <= t
    kc = kc_ref[0, 0]
    vc = vc_ref[0, 0]
    p_grp = jnp.zeros((tq, ncol), F32)
    for hh in range(HPG):
        sl = slice(hh * HEAD_DIM, (hh + 1) * HEAD_DIM)
        sc = jnp.where(valid, _dot_nt(q_ref[0, :, sl], kc), NEG_INF)
        e = jnp.exp2(sc - jnp.max(sc, axis=-1, keepdims=True))
        p = jnp.where(valid, e / jnp.sum(e, axis=-1, keepdims=True), 0.0)
        o_ref[0, :, sl] = _dot(p.astype(BF16), vc)
        p_grp = p_grp + p
    score = _dot_hi(p_grp, smat_ref[...])
    n_sel = sel_ref.shape[2]
    score = jnp.concatenate([score[r:r + LANES].T for r in range(0, tq, LANES)], axis=1)[:n_sel]
    n = lax.broadcasted_iota(jnp.int32, (n_sel, tq), 0)
    t = pl.program_id(2) * tq + lax.broadcasted_iota(jnp.int32, (n_sel, tq), 1)
    cur = t >> int(math.log2(SEL_BLOCK))
    forced = (n == 0) | (n == cur) | (n == cur - 1)
    future = n * SEL_BLOCK > t
    score = jnp.where(forced, jnp.inf, jnp.where(future, -jnp.inf, score))
    rank = jnp.zeros((n_sel, tq), jnp.int32)
    for kk in range(n_sel):
        ck = score[kk:kk + 1, :]
        ahead = (ck > score) | ((ck == score) & (kk < n))
        rank = rank + ahead.astype(jnp.int32)
    sel_ref[0, 0] = (rank < N_SELECT).astype(sel_ref.dtype)


def _sel_matrix(ncol, n_sel):
    rs = SEL_BLOCK // CMP_STRIDE
    rc = CMP_BLOCK // CMP_STRIDE
    mat = [[0.0] * ncol for _ in range(ncol)]
    for j in range(n_sel):
        for m in range(rs):
            for n in range(rc):
                i = rs * j + m + n - (rc - 1)
                if 0 <= i < ncol - 1:
                    mat[i][j] += 1.0
    return jnp.array(mat, F32)


def _cmp_attention(q_r, k_cmp, v_cmp, *, tq):
    bsz, s, _ = q_r.shape
    ncol = k_cmp.shape[2]
    n_sel = s // SEL_BLOCK
    gw = HPG * HEAD_DIM
    return pl.pallas_call(
        _cmp_attn_kernel,
        grid=(bsz, NSA_KV_HEADS, s // tq),
        in_specs=[pl.BlockSpec((1, tq, gw), lambda b, g, i: (b, i, g)),
                  pl.BlockSpec((1, 1, ncol, HEAD_DIM), lambda b, g, i: (b, g, 0, 0)),
                  pl.BlockSpec((1, 1, ncol, HEAD_DIM), lambda b, g, i: (b, g, 0, 0)),
                  pl.BlockSpec((ncol, ncol), lambda b, g, i: (0, 0))],
        out_specs=[pl.BlockSpec((1, tq, gw), lambda b, g, i: (b, i, g)),
                   pl.BlockSpec((1, 1, n_sel, tq), lambda b, g, i: (b, g, 0, i))],
        out_shape=[jax.ShapeDtypeStruct((bsz, s, NSA_HEADS * HEAD_DIM), F32),
                   jax.ShapeDtypeStruct((bsz, NSA_KV_HEADS, n_sel, s), F32)],
        compiler_params=_params("parallel", "parallel", "parallel"),
        name="cmp_attention",
    )(q_r, k_cmp, v_cmp, _sel_matrix(ncol, n_sel))


def _softmax_pv_t(scores, values_t):
    m = None
    for sc in scores:
        cm = jnp.max(sc, axis=0, keepdims=True)
        m = cm if m is None else jnp.maximum(m, cm)
    l = None
    probs = []
    for sc in scores:
        p = jnp.exp2(sc - m)
        ps = jnp.sum(p, axis=0, keepdims=True)
        l = ps if l is None else l + ps
        probs.append(p.astype(BF16))
    acc = _dot(jnp.concatenate(values_t, axis=1), jnp.concatenate(probs, axis=0))
    return acc / l


MASK_BIG = 2.0 ** 100
def _slc_win_kernel(q_ref, ks_ref, vst_ref, kw_ref, vwt_ref, selt_ref, blk_ref, oc_ref, gate_ref, o_ref,
                    gt_s, os_s, ow_s):
    tq = q_ref.shape[1]
    tk = vst_ref.shape[4]
    n_chunks = vst_ref.shape[2]
    n_sel = selt_ref.shape[2]
    g = pl.program_id(1)
    t0 = pl.program_id(2) * tq
    q_t = jnp.concatenate([q_ref[0, :, hh * HEAD_DIM:(hh + 1) * HEAD_DIM].astype(F32).T.astype(BF16)
                           for hh in range(HPG)], axis=1)

    def tile_heads(mask):
        return jnp.concatenate([mask] * HPG, axis=1)

    key_loc = lax.broadcasted_iota(jnp.int32, (tq, tq), 0)
    qry_loc = lax.broadcasted_iota(jnp.int32, (tq, tq), 1)
    n_win = WINDOW // tq + 1

    @pl.when(t0 >= WINDOW)
    def _():
        k0 = pl.multiple_of(t0 - WINDOW, tq)
        sc = _dot(kw_ref[0, pl.ds(k0, n_win * tq), :], q_t)
        scores = [sc[cc * tq:(cc + 1) * tq] for cc in range(n_win)]
        scores[0] = jnp.where(tile_heads(key_loc > qry_loc), scores[0], NEG_INF)
        scores[-1] = jnp.where(tile_heads(key_loc <= qry_loc), scores[-1], NEG_INF)
        ow_s[...] = _softmax_pv_t(scores, [vwt_ref[0, 0, k0 // tq + cc] for cc in range(n_win)])

    @pl.when(t0 < WINDOW)
    def _():
        sc = _dot(kw_ref[0, 0:WINDOW, :], q_t)
        scores = [jnp.where(tile_heads(cc * tq + key_loc <= t0 + qry_loc), sc[cc * tq:(cc + 1) * tq], NEG_INF)
                  for cc in range(WINDOW // tq)]
        ow_s[...] = _softmax_pv_t(scores, [vwt_ref[0, 0, cc] for cc in range(WINDOW // tq)])

    sel_bias = ((selt_ref[0, 0] - 1.0) * MASK_BIG).astype(BF16)
    q_aug = jnp.concatenate([q_t, tile_heads(sel_bias),
                             jnp.zeros((HEAD_DIM - n_sel, HPG * tq), BF16)], axis=0)
    k_loc = lax.broadcasted_iota(jnp.int32, (tk, tq), 0)
    t_loc = lax.broadcasted_iota(jnp.int32, (tk, tq), 1)

    def selected(n_used):
        rows = slice(0, n_used * tk)
        sc = _dot(jnp.concatenate([ks_ref[0, rows, :], blk_ref[rows, :]], axis=1), q_aug)
        scores = [sc[cc * tk:(cc + 1) * tk] for cc in range(n_used)]
        causal = ((n_used - 1) * tk + k_loc) <= (t0 + t_loc)
        scores[-1] = jnp.where(tile_heads(causal), scores[-1], NEG_INF)
        os_s[...] = _softmax_pv_t(scores, [vst_ref[0, 0, cc] for cc in range(n_used)])

    for vv in range(n_chunks):
        pl.when(t0 // tk == vv)(functools.partial(selected, vv + 1))

    gt = _sigmoid(gate_ref[0])
    gt_s[...] = gt.T
    lane = lax.broadcasted_iota(jnp.int32, gt.shape, 1)
    for hh in range(HPG):
        base = (g * HPG + hh) * 3
        g_cmp = jnp.sum(jnp.where(lane == base, gt, 0.0), axis=1, keepdims=True)
        cs = slice(hh * tq, (hh + 1) * tq)
        mix_t = gt_s[pl.ds(base + 1, 1), :] * os_s[:, cs] + gt_s[pl.ds(base + 2, 1), :] * ow_s[:, cs]
        sl = slice(hh * HEAD_DIM, (hh + 1) * HEAD_DIM)
        o_ref[0, :, sl] = (g_cmp * oc_ref[0, :, sl] + mix_t.T).astype(o_ref.dtype)


def _slc_win_attention(q_r, ks, vs_t, kw, vw_t, sel_t, o_cmp, gates, *, tq):
    bsz, s, _ = q_r.shape
    n_sel = s // SEL_BLOCK
    gw = HPG * HEAD_DIM
    assert vw_t.shape[4] == tq and tq == LANES and vs_t.shape[4] % tq == 0
    kv_spec = pl.BlockSpec((1, s, HEAD_DIM), lambda b, g, i: (b, 0, g))
    vt_spec = lambda a: pl.BlockSpec((1, 1) + a.shape[2:], lambda b, g, i: (b, g, 0, 0, 0))
    block_onehot = (jnp.arange(s)[:, None] // SEL_BLOCK == jnp.arange(HEAD_DIM)[None, :]).astype(BF16)
    return pl.pallas_call(
        _slc_win_kernel,
        grid=(bsz, NSA_KV_HEADS, s // tq),
        in_specs=[pl.BlockSpec((1, tq, gw), lambda b, g, i: (b, i, g)),
                  kv_spec, vt_spec(vs_t), kv_spec, vt_spec(vw_t),
                  pl.BlockSpec((1, 1, n_sel, tq), lambda b, g, i: (b, g, 0, i)),
                  pl.BlockSpec((s, HEAD_DIM), lambda b, g, i: (0, 0)),
                  pl.BlockSpec((1, tq, gw), lambda b, g, i: (b, i, g)),
                  pl.BlockSpec((1, tq, LANES), lambda b, g, i: (b, i, 0))],
        out_specs=pl.BlockSpec((1, tq, gw), lambda b, g, i: (b, i, g)),
        out_shape=jax.ShapeDtypeStruct((bsz, s, NSA_HEADS * HEAD_DIM), BF16),
        scratch_shapes=[pltpu.VMEM((LANES, tq), F32), pltpu.VMEM((HEAD_DIM, HPG * tq), F32),
                        pltpu.VMEM((HEAD_DIM, HPG * tq), F32)],
        compiler_params=_params("parallel", "parallel", "arbitrary"),
        name="slc_win_attention",
    )(q_r, ks, vs_t, kw, vw_t, sel_t, block_onehot, o_cmp, gates)


FFN_OUT_K_STEPS = 4


def _pad_cols(w, n):
    return jnp.pad(w, ((0, 0), (0, n - w.shape[1])))


def _conv_deltanet_mixer(xb, bsz, s, w_in, sc_conv_w, dn_conv_w, a_log, dt_bias, norm_w, w_out):
    sc_w = sc_conv_w.shape[1]
    dn_w = dn_conv_w.shape[1] // 3
    n_heads = dn_w // HEAD_DIM
    main = 3 * sc_w + 4 * dn_w
    proj, ba = _in_proj(xb, w_in.astype(BF16), main, _pad_cols(w_in[:, main:], LANES).astype(BF16),
                        tm=1024, tn=1024)
    proj = proj.reshape(bsz, s, main)
    y_sc = _short_conv(proj, sc_conv_w, sc_w, tc=256)
    y_dn = _deltanet(proj, 3 * sc_w, 3 * sc_w + 3 * dn_w, dn_conv_w, ba.reshape(bsz, s, LANES), a_log, dt_bias,
                     norm_w, n_heads)
    wo = w_out.astype(BF16)
    return [y_sc.reshape(bsz * s, sc_w), y_dn.reshape(bsz * s, dn_w)], [wo[:sc_w], wo[sc_w:]]


def _nsa_mixer(xb, bsz, s, positions, w_in, cmp_pos_k, cmp_w1_k, cmp_w2_k, cmp_pos_v, cmp_w1_v, cmp_w2_v, w_out):
    qw = NSA_HEADS * HEAD_DIM
    kvw = NSA_KV_HEADS * HEAD_DIM
    main = qw + 6 * kvw
    proj, gates = _in_proj(xb, w_in.astype(BF16), main, _pad_cols(w_in[:, main:], LANES).astype(BF16),
                           tm=1024, tn=1024)
    proj = proj.reshape(bsz, s, main)
    half = HEAD_DIM // 2
    inv = jnp.power(ROPE_THETA, -jnp.arange(half, dtype=F32) / half)
    inv = jnp.concatenate([inv, inv])
    ang = positions.astype(F32)[..., None] * inv
    cmp_end = jnp.minimum(jnp.arange(s // CMP_STRIDE) * CMP_STRIDE + CMP_BLOCK - 1, s - 1)
    ang_cmp = positions[:, cmp_end].astype(F32)[..., None] * inv
    q_r, ks, kw, vs_t, vw_t = _rope_qkv(proj, ang, ts=512, slc_chunk=256, win_chunk=LANES)
    k_cmp = _compress(proj, qw, cmp_pos_k, cmp_w1_k, cmp_w2_k, ang_cmp, rope=True)
    v_cmp = _compress(proj, qw + kvw, cmp_pos_v, cmp_w1_v, cmp_w2_v, ang_cmp, rope=False)
    o_cmp, sel_t = _cmp_attention(q_r, k_cmp, v_cmp, tq=256)
    o = _slc_win_attention(q_r, ks, vs_t, kw, vw_t, sel_t, o_cmp, gates.reshape(bsz, s, LANES), tq=LANES)
    return [o.reshape(bsz * s, qw)], [w_out.astype(BF16)]


def kernel(x, positions, ln_mix_g, ln_mix_b, ln_ffn_g, ln_ffn_b, ffn_w_in, ffn_w_out, hy_w_in, sc_conv_w, dn_conv_w, dn_a_log, dn_dt_bias, dn_norm_w, hy_w_out, nsa_w_in, cmp_pos_k, cmp_w1_k, cmp_w2_k, cmp_pos_v, cmp_w1_v, cmp_w2_v, nsa_w_out):
    bsz, s, d = x.shape
    xf = x.reshape(bsz * s, d)
    xb = xf
    ffn_w_out_b = ffn_w_out.astype(BF16)
    for i in range(DEPTH):
        j = i // 2
        if i % 2 == 0:
            ys, wos = _conv_deltanet_mixer(xb, bsz, s, hy_w_in[j], sc_conv_w[j], dn_conv_w[j], dn_a_log[j],
                                           dn_dt_bias[j], dn_norm_w[j], hy_w_out[j])
        else:
            ys, wos = _nsa_mixer(xb, bsz, s, positions, nsa_w_in[j], cmp_pos_k[j], cmp_w1_k[j], cmp_w2_k[j],
                                 cmp_pos_v[j], cmp_w1_v[j], cmp_w2_v[j], nsa_w_out[j])
        xf, xb = _matmul_ln(ys, wos, xf, ln_mix_g[i], ln_mix_b[i], tm=512, nk=1)
        hmid = _ffn_in(xb, ffn_w_in, i, tm=1024, tn=512)
        xf, xb = _matmul_ln([hmid], [ffn_w_out_b], xf, ln_ffn_g[i], ln_ffn_b[i], tm=1024, nk=FFN_OUT_K_STEPS,
                            layer=i)
    return xf.reshape(bsz, s, d)
```

```python
import functools
import math

import jax
import jax.numpy as jnp
from jax import lax
from jax.experimental import pallas as pl
from jax.experimental.pallas import tpu as pltpu

F32 = jnp.float32
BF16 = jnp.bfloat16
HIGHEST = lax.Precision.HIGHEST

LANES = 128
VMEM_LIMIT = 48 * 1024 * 1024
VMEM_LIMIT_BIG = 56 * 1024 * 1024

DN_HEADS = 8
DN_CHUNK = 64
DN_CONV = 4
SC_KERNEL = 3
NSA_HEADS = 16
NSA_KV_HEADS = 4
HPG = NSA_HEADS // NSA_KV_HEADS
HEAD_DIM = 128
CMP_BLOCK = 32
CMP_STRIDE = 16
SEL_BLOCK = 64
N_SELECT = 16
WINDOW = 512
ROPE_THETA = 10000.0
LN_EPS = 1e-5
NORM_EPS = 1e-6
NEG_INF = -1e30
DEPTH = 2
ALPHA = (2 * DEPTH) ** 0.25
ATTN_SCALE = HEAD_DIM ** -0.5
Q_SCALE = ATTN_SCALE * math.log2(math.e)


def _params(*sem, vmem=VMEM_LIMIT):
    return pltpu.CompilerParams(dimension_semantics=sem, vmem_limit_bytes=vmem)


def _sigmoid(x):
    return 1.0 / (1.0 + jnp.exp(-x))


def _silu(x):
    return x * _sigmoid(x)


def _dot(a, b):
    return jnp.dot(a, b, preferred_element_type=F32)


def _dot_nt(a, b):
    return lax.dot_general(a, b, (((1,), (1,)), ((), ())), preferred_element_type=F32)


def _dot_tn(a, b):
    return lax.dot_general(a, b, (((0,), (0,)), ((), ())), preferred_element_type=F32)


def _dot_hi(a, b):
    return jnp.dot(a, b, precision=HIGHEST, preferred_element_type=F32)


def _proj_kernel(x_ref, w_ref, ws_ref, o_ref, os_ref):
    xb = x_ref[...].astype(BF16)
    o_ref[...] = _dot(xb, w_ref[...])

    @pl.when(pl.program_id(1) == 0)
    def _():
        os_ref[...] = _dot(xb, ws_ref[...])


def _in_proj(x, w, n, w_side, *, tm, tn):
    m, k = x.shape
    ns = w_side.shape[1]
    return pl.pallas_call(
        _proj_kernel,
        grid=(m // tm, n // tn),
        in_specs=[pl.BlockSpec((tm, k), lambda i, j: (i, 0)),
                  pl.BlockSpec((k, tn), lambda i, j: (0, j)),
                  pl.BlockSpec((k, ns), lambda i, j: (0, 0))],
        out_specs=[pl.BlockSpec((tm, tn), lambda i, j: (i, j)),
                   pl.BlockSpec((tm, ns), lambda i, j: (i, 0))],
        out_shape=[jax.ShapeDtypeStruct((m, n), F32), jax.ShapeDtypeStruct((m, ns), F32)],
        compiler_params=_params("parallel", "arbitrary"),
        name="in_proj",
    )(x, w, w_side)


def _ffn_in_kernel(x_ref, wg_ref, wu_ref, o_ref, wgb_s, wub_s):
    @pl.when(pl.program_id(1) == 0)
    def _():
        wgb_s[...] = wg_ref[...].astype(BF16)
        wub_s[...] = wu_ref[...].astype(BF16)

    x = x_ref[...]
    gate = _dot(x, wgb_s[...])
    up = _dot(x, wub_s[...])
    o_ref[...] = (_silu(gate) * up).astype(o_ref.dtype)


def _ffn_in(xb, w_in, layer, *, tm, tn):
    m, k = xb.shape
    hidden = w_in.shape[2] // 2
    nj = hidden // tn
    return pl.pallas_call(
        _ffn_in_kernel,
        grid=(nj, m // tm),
        in_specs=[pl.BlockSpec((tm, k), lambda j, i: (i, 0)),
                  pl.BlockSpec((None, k, tn), lambda j, i: (layer, 0, j)),
                  pl.BlockSpec((None, k, tn), lambda j, i: (layer, 0, j + nj))],
        out_specs=pl.BlockSpec((tm, tn), lambda j, i: (i, j)),
        out_shape=jax.ShapeDtypeStruct((m, hidden), BF16),
        scratch_shapes=[pltpu.VMEM((k, tn), BF16), pltpu.VMEM((k, tn), BF16)],
        compiler_params=_params("parallel", "arbitrary"),
        name="ffn_in",
    )(xb, w_in, w_in)


MM_LN_ROWS = 128


def _mm_ln_kernel(*refs, n_pairs, nj):
    xs = refs[:n_pairs]
    ws = refs[n_pairs:2 * n_pairs]
    r_ref, g_ref, b_ref, o_ref, ob_ref, y_s = refs[2 * n_pairs:2 * n_pairs + 6]
    tm = o_ref.shape[0]
    j = pl.program_id(1)
    part = _dot(xs[0][...], ws[0][...])
    for x_ref, w_ref in zip(xs[1:], ws[1:]):
        part = part + _dot(x_ref[...], w_ref[...])
    y_s[j] = part

    @pl.when(j == nj - 1)
    def _():
        for r0 in range(0, tm, MM_LN_ROWS):
            rows = slice(r0, r0 + MM_LN_ROWS)
            y = jnp.concatenate([y_s[jj, rows, :] for jj in range(nj)], axis=1)
            v = ALPHA * r_ref[rows, :] + y
            mu = jnp.mean(v, axis=-1, keepdims=True)
            dv = v - mu
            var = jnp.mean(dv * dv, axis=-1, keepdims=True)
            out = dv * lax.rsqrt(var + LN_EPS) * g_ref[...] + b_ref[...]
            o_ref[rows, :] = out
            ob_ref[rows, :] = out.astype(BF16)


def _matmul_ln(xs, ws, resid, g, b, *, tm, tn, layer=None):
    m, d = resid.shape
    n_pairs = len(xs)
    nj = d // tn
    ks = [x.shape[1] for x in xs]
    if layer is None:
        w_specs = [pl.BlockSpec((k, tn), lambda i, j: (0, j)) for k in ks]
    else:
        w_specs = [pl.BlockSpec((None, k, tn), lambda i, j: (layer, 0, j)) for k in ks]
    in_specs = ([pl.BlockSpec((tm, k), lambda i, j: (i, 0)) for k in ks]
                + w_specs
                + [pl.BlockSpec((tm, d), lambda i, j: (i, 0)),
                   pl.BlockSpec((1, d), lambda i, j: (0, 0)),
                   pl.BlockSpec((1, d), lambda i, j: (0, 0))])
    return pl.pallas_call(
        functools.partial(_mm_ln_kernel, n_pairs=n_pairs, nj=nj),
        grid=(m // tm, nj),
        in_specs=in_specs,
        out_specs=[pl.BlockSpec((tm, d), lambda i, j: (i, 0)),
                   pl.BlockSpec((tm, d), lambda i, j: (i, 0))],
        out_shape=[jax.ShapeDtypeStruct((m, d), F32), jax.ShapeDtypeStruct((m, d), BF16)],
        scratch_shapes=[pltpu.VMEM((nj, tm, tn), F32)],
        compiler_params=_params("parallel", "arbitrary", vmem=VMEM_LIMIT_BIG),
        name="matmul_ln",
    )(*xs, *ws, resid, g.reshape(1, d), b.reshape(1, d))


def _causal_conv(u, w_ref, taps):
    def tap_sum(x, shift):
        acc = x * w_ref[taps - 1:taps, :]
        for sh in range(1, taps):
            acc = acc + shift(x, sh) * w_ref[taps - 1 - sh:taps - sh, :]
        return acc

    body = tap_sum(u, lambda x, sh: pltpu.roll(x, sh, axis=0))
    row = lax.broadcasted_iota(jnp.int32, (8, u.shape[1]), 0)
    head = tap_sum(u[0:8], lambda x, sh: jnp.where(row >= sh, pltpu.roll(x, sh, axis=0), 0.0))
    return jnp.concatenate([head, body[8:]], axis=0)


def _sc_kernel(b_ref, c_ref, h_ref, w_ref, o_ref):
    u = c_ref[0] * h_ref[0]
    o_ref[0] = (b_ref[0] * _causal_conv(u, w_ref, SC_KERNEL)).astype(o_ref.dtype)


def _short_conv(proj, conv_w, width, *, tc):
    bsz, s, _ = proj.shape
    nb = width // tc
    w = jnp.zeros((8, width), F32).at[:SC_KERNEL].set(conv_w)
    return pl.pallas_call(
        _sc_kernel,
        grid=(bsz, nb),
        in_specs=[pl.BlockSpec((1, s, tc), lambda b, j: (b, 0, j)),
                  pl.BlockSpec((1, s, tc), lambda b, j: (b, 0, j + nb)),
                  pl.BlockSpec((1, s, tc), lambda b, j: (b, 0, j + 2 * nb)),
                  pl.BlockSpec((8, tc), lambda b, j: (0, j))],
        out_specs=pl.BlockSpec((1, s, tc), lambda b, j: (b, 0, j)),
        out_shape=jax.ShapeDtypeStruct((bsz, s, width), BF16),
        compiler_params=_params("parallel", "parallel"),
        name="short_conv",
    )(proj, proj, proj, w)


def _split(x):
    hi = x.astype(BF16)
    return hi, (x - hi.astype(F32)).astype(BF16)


def _dotb(a, b):
    return _dot(a.astype(BF16), b.astype(BF16))


DN_GROUP = 8


def _dn_group_local(base, scr, masks, out):
    q_s, k_s, kb_s, qd_s, kf_s, kbe_s, vb_s, gc_s, gcd_s = scr
    incl, strict, m8, m16, eye = masks
    c = DN_CHUNK
    idx = range(DN_GROUP)
    rows = [pl.ds(base + cc * c, c) for cc in idx]
    gc = [gc_s[r, :] for r in rows]
    decay = []
    for cc in idx:
        gc_j = gcd_s[pl.ds(base // LANES + cc // 2, 1), (cc % 2) * c:(cc % 2) * c + c]
        decay.append(jnp.where(incl, jnp.exp(jnp.where(incl, gc[cc][:, :c] - gc_j, 0.0)), 0.0))
    kbf = [k_s[r, :] for r in rows]
    kk = [_dot_nt(kb_s[rows[cc], :], kbf[cc]) for cc in idx]
    qk = [_dot_nt(q_s[rows[cc], :], kbf[cc]) for cc in idx]
    yield
    a = [jnp.where(strict, kk[cc] * decay[cc], 0.0) for cc in idx]
    intra = [(qk[cc] * decay[cc]).astype(BF16) for cc in idx]
    ad = [jnp.where(m8, x, 0.0) for x in a]
    adb = [x.astype(BF16) for x in ad]
    a2 = [_dot(x, x) for x in adb]
    yield
    a2b = [x.astype(BF16) for x in a2]
    p = [eye - x for x in ad]
    p1, a4 = [], []
    for cc in idx:
        p1.append(p[cc] + _dotb(p[cc], a2b[cc]))
        a4.append(_dot(a2b[cc], a2b[cc]))
    yield
    p2 = [p1[cc] + _dotb(p1[cc], a4[cc]) for cc in idx]
    yield
    pb = [x.astype(BF16) for x in p2]
    t = [_dotb(pb[cc], jnp.where(m16, a[cc] - ad[cc], 0.0)) for cc in idx]
    yield
    dinv = [(p2[cc] - _dotb(t[cc], pb[cc])).astype(BF16) for cc in idx]
    yield
    db, da = [], []
    for cc in idx:
        rhs = jnp.concatenate([vb_s[rows[cc], :], kbe_s[rows[cc], :]], axis=1)
        db.append(_dotb(dinv[cc], rhs))
        da.append(_dotb(dinv[cc], jnp.where(m16, 0.0, a[cc])).astype(BF16))
    yield
    blocks = [[x[0:16]] for x in db]
    for s4 in range(1, c // 16):
        rs = slice(16 * s4, 16 * s4 + 16)
        for cc in idx:
            xprev = jnp.concatenate(blocks[cc] + [jnp.zeros((c - 16 * s4, 2 * HEAD_DIM), F32)], axis=0)
            blocks[cc].append(db[cc][rs] - _dotb(da[cc][rs], xprev))
        yield
    for cc in idx:
        sol = jnp.concatenate(blocks[cc], axis=0)
        sol_hi, sol_lo = _split(sol)
        g_last = gc[cc][c - 1:c, :]
        k_dec_t = (kf_s[rows[cc], :] * jnp.exp(g_last - gc[cc])).T.astype(BF16)
        kw = _dot(k_dec_t, sol_hi) + _dot(k_dec_t, sol_lo)
        iw = _dot(intra[cc], sol_hi) + _dot(intra[cc], sol_lo)
        out.append((kw[:, HEAD_DIM:].astype(BF16), kw[:, :HEAD_DIM],
                    (qd_s[rows[cc], :] - iw[:, HEAD_DIM:]).astype(BF16), iw[:, :HEAD_DIM], jnp.exp(g_last)))
    yield


def _dn_chunk_seq(state, loc, z, nw):
    w2, n_mat, qp, op, eg_last = loc
    sb = state.astype(BF16)
    o = _dot(qp, sb) + op
    state = (state * eg_last - _dot(w2, sb)) + n_mat
    o = o * lax.rsqrt(jnp.mean(o * o, axis=-1, keepdims=True) + NORM_EPS) * nw * _silu(z)
    return state, o


def _dn_kernel(qp_ref, kp_ref, vp_ref, z_ref, ba_ref, arow_ref, dtrow_ref, cwq_ref, cwk_ref, cwv_ref, nw_ref, o_ref,
               q_s, k_s, kb_s, qd_s, kf_s, kbe_s, vb_s, gc_s, beta_s, gates_s, gcd_s, *, n_heads):
    h = pl.program_id(1)
    s = qp_ref.shape[1]
    c = DN_CHUNK
    @pl.when(h == 0)
    def _():
        ba = ba_ref[0]
        xa = ba + dtrow_ref[...]
        softplus = jnp.maximum(xa, 0.0) + jnp.log(1.0 + jnp.exp(-jnp.abs(xa)))
        lane0 = lax.broadcasted_iota(jnp.int32, ba.shape, 1)
        gates = jnp.where(lane0 < n_heads, _sigmoid(ba), -jnp.exp(arow_ref[...]) * softplus)
        for blk in range(s // LANES):
            rs = slice(blk * LANES, (blk + 1) * LANES)
            gates_s[rs, :] = gates[rs, :].T

    n_blk = s // LANES
    beta_d = jnp.concatenate([gates_s[pl.ds(blk * LANES + h, 1), :] for blk in range(n_blk)], axis=0)
    gc_d = jnp.concatenate([gates_s[pl.ds(blk * LANES + h + n_heads, 1), :] for blk in range(n_blk)], axis=0)
    pos = lax.broadcasted_iota(jnp.int32, (n_blk, LANES), 1) & (c - 1)
    sh = 1
    while sh < c:
        gc_d = gc_d + jnp.where(pos >= sh, pltpu.roll(gc_d, sh, axis=1), 0.0)
        sh *= 2
    gcd_s[0:n_blk, :] = gc_d
    for blk in range(n_blk):
        rs = slice(blk * LANES, (blk + 1) * LANES)
        gc_s[rs, :] = jnp.broadcast_to(gc_d[blk:blk + 1, :], (LANES, LANES)).T
        beta_s[rs, :] = jnp.broadcast_to(beta_d[blk:blk + 1, :], (LANES, LANES)).T
    gc = gc_s[...]
    beta = beta_s[...]
    eg = jnp.exp(gc)
    q = _silu(_causal_conv(qp_ref[0], cwq_ref, DN_CONV))
    q = q * (lax.rsqrt(jnp.sum(q * q, axis=-1, keepdims=True) + NORM_EPS) * (HEAD_DIM ** -0.5))
    q_s[...] = q.astype(BF16)
    qd_s[...] = q * eg
    k = _silu(_causal_conv(kp_ref[0], cwk_ref, DN_CONV))
    k = k * lax.rsqrt(jnp.sum(k * k, axis=-1, keepdims=True) + NORM_EPS)
    kb = k * beta
    kf_s[...] = k
    k_s[...] = k.astype(BF16)
    kb_s[...] = kb.astype(BF16)
    kbe_s[...] = kb * eg
    vb_s[...] = _silu(_causal_conv(vp_ref[0], cwv_ref, DN_CONV)) * beta

    row = lax.broadcasted_iota(jnp.int32, (c, c), 0)
    col = lax.broadcasted_iota(jnp.int32, (c, c), 1)
    masks = (row >= col, row > col, (row >> 3) == (col >> 3), (row >> 4) == (col >> 4), (row == col).astype(F32))
    scr = (q_s, k_s, kb_s, qd_s, kf_s, kbe_s, vb_s, gc_s, gcd_s)
    nw = nw_ref[...]
    rows_per_group = DN_GROUP * c
    n_groups = s // rows_per_group

    def group_base(gi):
        base = gi * rows_per_group
        return base if isinstance(base, int) else pl.multiple_of(base, rows_per_group)

    def run(gi_local, gi_seq, state, locs):
        nxt = []
        stages = iter(()) if gi_local is None else _dn_group_local(group_base(gi_local), scr, masks, nxt)
        todo = list(range(DN_GROUP)) if gi_seq is not None else []
        done = False
        while todo or not done:
            if not done:
                done = next(stages, "end") == "end"
            if todo:
                cc = todo.pop(0)
                rows = pl.ds(group_base(gi_seq) + cc * c, c)
                state, o = _dn_chunk_seq(state, locs[cc], z_ref[0, rows, :], nw)
                o_ref[0, rows, :] = o.astype(o_ref.dtype)
        return state, tuple(nxt)

    def body(gi, carry):
        return run(gi + 1, gi, *carry)

    carry = run(0, None, jnp.zeros((HEAD_DIM, HEAD_DIM), F32), None)
    carry = lax.fori_loop(0, n_groups - 1, body, carry)
    run(None, n_groups - 1, *carry)


def _deltanet(proj, qkv_col0, z_col0, conv_w, ba, a_log, dt_bias, norm_w, n_heads):
    bsz, s, _ = proj.shape
    qb0 = qkv_col0 // HEAD_DIM
    zb0 = z_col0 // HEAD_DIM
    arow = jnp.zeros((1, LANES), F32).at[0, n_heads:2 * n_heads].set(a_log)
    dtrow = jnp.zeros((1, LANES), F32).at[0, n_heads:2 * n_heads].set(dt_bias)
    cw = jnp.zeros((8, 3 * n_heads * HEAD_DIM), F32).at[:DN_CONV].set(conv_w)
    blk = (1, s, HEAD_DIM)
    col_spec = lambda off: pl.BlockSpec(blk, lambda b, h, off=off: (b, 0, h + off))
    cw_spec = lambda off: pl.BlockSpec((8, HEAD_DIM), lambda b, h, off=off: (0, h + off))
    row_spec = pl.BlockSpec((1, LANES), lambda b, h: (0, 0))
    return pl.pallas_call(
        functools.partial(_dn_kernel, n_heads=n_heads),
        grid=(bsz, n_heads),
        in_specs=[col_spec(qb0), col_spec(qb0 + n_heads), col_spec(qb0 + 2 * n_heads), col_spec(zb0),
                  pl.BlockSpec((1, s, LANES), lambda b, h: (b, 0, 0)), row_spec, row_spec,
                  cw_spec(0), cw_spec(n_heads), cw_spec(2 * n_heads), row_spec],
        out_specs=pl.BlockSpec(blk, lambda b, h: (b, 0, h)),
        out_shape=jax.ShapeDtypeStruct((bsz, s, n_heads * HEAD_DIM), BF16),
        scratch_shapes=([pltpu.VMEM((s, HEAD_DIM), BF16)] * 3 + [pltpu.VMEM((s, HEAD_DIM), F32)] * 7
                        + [pltpu.VMEM((max(8, s // LANES), LANES), F32)]),
        compiler_params=_params("parallel", "arbitrary"),
        name="deltanet",
    )(proj, proj, proj, proj, ba, arow, dtrow, cw, cw, cw, norm_w.reshape(1, HEAD_DIM))


def _rope_tables(ang):
    lane = lax.broadcasted_iota(jnp.int32, ang.shape, 1)
    sin = jnp.sin(ang)
    return jnp.cos(ang), jnp.where(lane < HEAD_DIM // 2, -sin, sin)


def _rope(x, cos, sin_signed):
    return x * cos + pltpu.roll(x, HEAD_DIM // 2, axis=1) * sin_signed


def _rope_kernel(ang_ref, q_ref, ks_ref, kw_ref, vs_ref, vw_ref, qo_ref, kso_ref, kwo_ref, vso_ref, vwo_ref):
    cos, sin = _rope_tables(ang_ref[0])
    for hh in range(NSA_HEADS):
        sl = slice(hh * HEAD_DIM, (hh + 1) * HEAD_DIM)
        qo_ref[0, :, sl] = (_rope(q_ref[0, :, sl], cos, sin) * Q_SCALE).astype(BF16)
    for g in range(NSA_KV_HEADS):
        sl = slice(g * HEAD_DIM, (g + 1) * HEAD_DIM)
        kso_ref[0, :, sl] = _rope(ks_ref[0, :, sl], cos, sin).astype(BF16)
        kwo_ref[0, :, sl] = _rope(kw_ref[0, :, sl], cos, sin).astype(BF16)
    ts = vs_ref.shape[1]
    for v_ref, vo_ref in ((vs_ref, vso_ref), (vw_ref, vwo_ref)):
        ck = vo_ref.shape[4]
        for g in range(NSA_KV_HEADS):
            for cc in range(ts // ck):
                parts = [v_ref[0, cc * ck + r:cc * ck + r + LANES, g * HEAD_DIM:(g + 1) * HEAD_DIM].T
                         for r in range(0, ck, LANES)]
                vo_ref[0, g, cc] = jnp.concatenate(parts, axis=1).astype(BF16)


def _rope_qkv(proj, ang, *, ts, slc_chunk, win_chunk):
    bsz, s, _ = proj.shape
    qw = NSA_HEADS * HEAD_DIM
    kvw = NSA_KV_HEADS * HEAD_DIM
    kv_spec = lambda blk: pl.BlockSpec((1, ts, kvw), lambda b, i, blk=blk: (b, i, blk))
    kv_out = pl.BlockSpec((1, ts, kvw), lambda b, i: (b, i, 0))
    kv_shape = jax.ShapeDtypeStruct((bsz, s, kvw), BF16)
    vt_out = lambda ck: pl.BlockSpec((1, NSA_KV_HEADS, ts // ck, HEAD_DIM, ck), lambda b, i: (b, 0, i, 0, 0))
    vt_shape = lambda ck: jax.ShapeDtypeStruct((bsz, NSA_KV_HEADS, s // ck, HEAD_DIM, ck), BF16)
    base = qw // kvw
    return pl.pallas_call(
        _rope_kernel,
        grid=(bsz, s // ts),
        in_specs=[pl.BlockSpec((1, ts, HEAD_DIM), lambda b, i: (b, i, 0)),
                  pl.BlockSpec((1, ts, qw), lambda b, i: (b, i, 0)),
                  kv_spec(base + 2), kv_spec(base + 4), kv_spec(base + 3), kv_spec(base + 5)],
        out_specs=[pl.BlockSpec((1, ts, qw), lambda b, i: (b, i, 0)), kv_out, kv_out,
                   vt_out(slc_chunk), vt_out(win_chunk)],
        out_shape=[jax.ShapeDtypeStruct((bsz, s, qw), BF16), kv_shape, kv_shape,
                   vt_shape(slc_chunk), vt_shape(win_chunk)],
        compiler_params=_params("parallel", "parallel"),
        name="rope_qkv",
    )(ang, proj, proj, proj, proj, proj)


def _gelu_tanh(x):
    return x * (0.5 * (1.0 + jnp.tanh(math.sqrt(2.0 / math.pi) * (x + 0.044715 * (x * x * x)))))


def _compress_kernel(x_ref, w1_ref, w2_ref, pos_ref, ang_ref, o_ref, *, rope):
    nsub = x_ref.shape[1] // CMP_STRIDE
    hid = w1_ref.shape[1]
    pa = jnp.zeros((nsub, hid), F32)
    pb = jnp.zeros((nsub, hid), F32)
    for l in range(CMP_STRIDE):
        xl = x_ref[0, pl.ds(l, nsub, stride=CMP_STRIDE), :].astype(BF16)
        pa = pa + _dot(xl, w1_ref[l * HEAD_DIM:(l + 1) * HEAD_DIM, :])
        pb = pb + _dot(xl, w1_ref[(CMP_STRIDE + l) * HEAD_DIM:(CMP_STRIDE + l + 1) * HEAD_DIM, :])
    bias = _dot(pos_ref[...], w1_ref[...])[0:1, :]
    hpre = pa + pltpu.roll(pb, nsub - 1, axis=0) + bias
    out = _dot(_gelu_tanh(hpre).astype(BF16), w2_ref[...])
    if rope:
        cos, sin = _rope_tables(ang_ref[0])
        out = _rope(out, cos, sin)
    o_ref[0, 0] = out.astype(o_ref.dtype)


def _compress(proj, col0, pos_emb, w1, w2, ang_cmp, *, rope):
    bsz, s, _ = proj.shape
    nsub = s // CMP_STRIDE
    blk0 = col0 // HEAD_DIM
    hid = w1.shape[1]
    pos = jnp.zeros((8, CMP_BLOCK * HEAD_DIM), BF16).at[0].set(pos_emb.reshape(-1).astype(BF16))
    return pl.pallas_call(
        functools.partial(_compress_kernel, rope=rope),
        grid=(bsz, NSA_KV_HEADS),
        in_specs=[pl.BlockSpec((1, s, HEAD_DIM), lambda b, g: (b, 0, g + blk0)),
                  pl.BlockSpec((CMP_BLOCK * HEAD_DIM, hid), lambda b, g: (0, 0)),
                  pl.BlockSpec((hid, HEAD_DIM), lambda b, g: (0, 0)),
                  pl.BlockSpec((8, CMP_BLOCK * HEAD_DIM), lambda b, g: (0, 0)),
                  pl.BlockSpec((1, nsub, HEAD_DIM), lambda b, g: (b, 0, 0))],
        out_specs=pl.BlockSpec((1, 1, nsub, HEAD_DIM), lambda b, g: (b, g, 0, 0)),
        out_shape=jax.ShapeDtypeStruct((bsz, NSA_KV_HEADS, nsub, HEAD_DIM), BF16),
        compiler_params=_params("parallel", "parallel"),
        name="compress",
    )(proj, w1.astype(BF16), w2.astype(BF16), pos, ang_cmp)


def _cmp_attn_kernel(q_ref, kc_ref, vc_ref, smat_ref, o_ref, sel_ref):
    tq = q_ref.shape[1]
    ncol = kc_ref.shape[2]
    t = pl.program_id(2) * tq + lax.broadcasted_iota(jnp.int32, (tq, ncol), 0)
    n = lax.broadcasted_iota(jnp.int32, (tq, ncol), 1)
    valid = (n * CMP_STRIDE + CMP_BLOCK - 1) <= t
    kc = kc_ref[0, 0]
    vc = vc_ref[0, 0]
    p_grp = jnp.zeros((tq, ncol), F32)
    for hh in range(HPG):
        sl = slice(hh * HEAD_DIM, (hh + 1) * HEAD_DIM)
        sc = jnp.where(valid, _dot_nt(q_ref[0, :, sl], kc), NEG_INF)
        e = jnp.exp2(sc - jnp.max(sc, axis=-1, keepdims=True))
        p = jnp.where(valid, e / jnp.sum(e, axis=-1, keepdims=True), 0.0)
        o_ref[0, :, sl] = _dot(p.astype(BF16), vc)
        p_grp = p_grp + p
    score = _dot_hi(p_grp, smat_ref[...])
    n_sel = sel_ref.shape[2]
    score = jnp.concatenate([score[r:r + LANES].T for r in range(0, tq, LANES)], axis=1)[:n_sel]
    n = lax.broadcasted_iota(jnp.int32, (n_sel, tq), 0)
    t = pl.program_id(2) * tq + lax.broadcasted_iota(jnp.int32, (n_sel, tq), 1)
    cur = t >> int(math.log2(SEL_BLOCK))
    forced = (n == 0) | (n == cur) | (n == cur - 1)
    future = n * SEL_BLOCK > t
    score = jnp.where(forced, jnp.inf, jnp.where(future, -jnp.inf, score))
    rank = jnp.zeros((n_sel, tq), jnp.int32)
    for kk in range(n_sel):
        ck = score[kk:kk + 1, :]
        ahead = (ck > score) | ((ck == score) & (kk < n))
        rank = rank + ahead.astype(jnp.int32)
    sel_ref[0, 0] = (rank < N_SELECT).astype(sel_ref.dtype)


def _sel_matrix(ncol, n_sel):
    rs = SEL_BLOCK // CMP_STRIDE
    rc = CMP_BLOCK // CMP_STRIDE
    mat = [[0.0] * ncol for _ in range(ncol)]
    for j in range(n_sel):
        for m in range(rs):
            for n in range(rc):
                i = rs * j + m + n - (rc - 1)
                if 0 <= i < ncol - 1:
                    mat[i][j] += 1.0
    return jnp.array(mat, F32)


def _cmp_attention(q_r, k_cmp, v_cmp, *, tq):
    bsz, s, _ = q_r.shape
    ncol = k_cmp.shape[2]
    n_sel = s // SEL_BLOCK
    gw = HPG * HEAD_DIM
    return pl.pallas_call(
        _cmp_attn_kernel,
        grid=(bsz, NSA_KV_HEADS, s // tq),
        in_specs=[pl.BlockSpec((1, tq, gw), lambda b, g, i: (b, i, g)),
                  pl.BlockSpec((1, 1, ncol, HEAD_DIM), lambda b, g, i: (b, g, 0, 0)),
                  pl.BlockSpec((1, 1, ncol, HEAD_DIM), lambda b, g, i: (b, g, 0, 0)),
                  pl.BlockSpec((ncol, ncol), lambda b, g, i: (0, 0))],
        out_specs=[pl.BlockSpec((1, tq, gw), lambda b, g, i: (b, i, g)),
                   pl.BlockSpec((1, 1, n_sel, tq), lambda b, g, i: (b, g, 0, i))],
        out_shape=[jax.ShapeDtypeStruct((bsz, s, NSA_HEADS * HEAD_DIM), F32),
                   jax.ShapeDtypeStruct((bsz, NSA_KV_HEADS, n_sel, s), F32)],
        compiler_params=_params("parallel", "parallel", "parallel"),
        name="cmp_attention",
    )(q_r, k_cmp, v_cmp, _sel_matrix(ncol, n_sel))


def _softmax_pv_t(scores, values_t):
    m = None
    for sc in scores:
        cm = jnp.max(sc, axis=0, keepdims=True)
        m = cm if m is None else jnp.maximum(m, cm)
    l = None
    probs = []
    for sc in scores:
        p = jnp.exp2(sc - m)
        ps = jnp.sum(p, axis=0, keepdims=True)
        l = ps if l is None else l + ps
        probs.append(p.astype(BF16))
    acc = _dot(jnp.concatenate(values_t, axis=1), jnp.concatenate(probs, axis=0))
    return acc / l


MASK_BIG = 2.0 ** 100
def _slc_win_kernel(q_ref, ks_ref, vst_ref, kw_ref, vwt_ref, selt_ref, blk_ref, oc_ref, gate_ref, o_ref,
                    gt_s, os_s, ow_s):
    tq = q_ref.shape[1]
    tk = vst_ref.shape[4]
    n_chunks = vst_ref.shape[2]
    n_sel = selt_ref.shape[2]
    g = pl.program_id(1)
    t0 = pl.program_id(2) * tq
    q_t = jnp.concatenate([q_ref[0, :, hh * HEAD_DIM:(hh + 1) * HEAD_DIM].astype(F32).T.astype(BF16)
                           for hh in range(HPG)], axis=1)

    def tile_heads(mask):
        return jnp.concatenate([mask] * HPG, axis=1)

    key_loc = lax.broadcasted_iota(jnp.int32, (tq, tq), 0)
    qry_loc = lax.broadcasted_iota(jnp.int32, (tq, tq), 1)
    n_win = WINDOW // tq + 1

    @pl.when(t0 >= WINDOW)
    def _():
        k0 = pl.multiple_of(t0 - WINDOW, tq)
        sc = _dot(kw_ref[0, pl.ds(k0, n_win * tq), :], q_t)
        scores = [sc[cc * tq:(cc + 1) * tq] for cc in range(n_win)]
        scores[0] = jnp.where(tile_heads(key_loc > qry_loc), scores[0], NEG_INF)
        scores[-1] = jnp.where(tile_heads(key_loc <= qry_loc), scores[-1], NEG_INF)
        ow_s[...] = _softmax_pv_t(scores, [vwt_ref[0, 0, k0 // tq + cc] for cc in range(n_win)])

    @pl.when(t0 < WINDOW)
    def _():
        sc = _dot(kw_ref[0, 0:WINDOW, :], q_t)
        scores = [jnp.where(tile_heads(cc * tq + key_loc <= t0 + qry_loc), sc[cc * tq:(cc + 1) * tq], NEG_INF)
                  for cc in range(WINDOW // tq)]
        ow_s[...] = _softmax_pv_t(scores, [vwt_ref[0, 0, cc] for cc in range(WINDOW // tq)])

    sel_bias = ((selt_ref[0, 0] - 1.0) * MASK_BIG).astype(BF16)
    q_aug = jnp.concatenate([q_t, tile_heads(sel_bias),
                             jnp.zeros((HEAD_DIM - n_sel, HPG * tq), BF16)], axis=0)
    k_loc = lax.broadcasted_iota(jnp.int32, (tk, tq), 0)
    t_loc = lax.broadcasted_iota(jnp.int32, (tk, tq), 1)

    def selected(n_used):
        rows = slice(0, n_used * tk)
        sc = _dot(jnp.concatenate([ks_ref[0, rows, :], blk_ref[rows, :]], axis=1), q_aug)
        scores = [sc[cc * tk:(cc + 1) * tk] for cc in range(n_used)]
        causal = ((n_used - 1) * tk + k_loc) <= (t0 + t_loc)
        scores[-1] = jnp.where(tile_heads(causal), scores[-1], NEG_INF)
        os_s[...] = _softmax_pv_t(scores, [vst_ref[0, 0, cc] for cc in range(n_used)])

    for vv in range(n_chunks):
        pl.when(t0 // tk == vv)(functools.partial(selected, vv + 1))

    gt = _sigmoid(gate_ref[0])
    gt_s[...] = gt.T
    lane = lax.broadcasted_iota(jnp.int32, gt.shape, 1)
    for hh in range(HPG):
        base = (g * HPG + hh) * 3
        g_cmp = jnp.sum(jnp.where(lane == base, gt, 0.0), axis=1, keepdims=True)
        cs = slice(hh * tq, (hh + 1) * tq)
        mix_t = gt_s[pl.ds(base + 1, 1), :] * os_s[:, cs] + gt_s[pl.ds(base + 2, 1), :] * ow_s[:, cs]
        sl = slice(hh * HEAD_DIM, (hh + 1) * HEAD_DIM)
        o_ref[0, :, sl] = (g_cmp * oc_ref[0, :, sl] + mix_t.T).astype(o_ref.dtype)


def _slc_win_attention(q_r, ks, vs_t, kw, vw_t, sel_t, o_cmp, gates, *, tq):
    bsz, s, _ = q_r.shape
    n_sel = s // SEL_BLOCK
    gw = HPG * HEAD_DIM
    assert vw_t.shape[4] == tq and tq == LANES and vs_t.shape[4] % tq == 0
    kv_spec = pl.BlockSpec((1, s, HEAD_DIM), lambda b, g, i: (b, 0, g))
    vt_spec = lambda a: pl.BlockSpec((1, 1) + a.shape[2:], lambda b, g, i: (b, g, 0, 0, 0))
    block_onehot = (jnp.arange(s)[:, None] // SEL_BLOCK == jnp.arange(HEAD_DIM)[None, :]).astype(BF16)
    return pl.pallas_call(
        _slc_win_kernel,
        grid=(bsz, NSA_KV_HEADS, s // tq),
        in_specs=[pl.BlockSpec((1, tq, gw), lambda b, g, i: (b, i, g)),
                  kv_spec, vt_spec(vs_t), kv_spec, vt_spec(vw_t),
                  pl.BlockSpec((1, 1, n_sel, tq), lambda b, g, i: (b, g, 0, i)),
                  pl.BlockSpec((s, HEAD_DIM), lambda b, g, i: (0, 0)),
                  pl.BlockSpec((1, tq, gw), lambda b, g, i: (b, i, g)),
                  pl.BlockSpec((1, tq, LANES), lambda b, g, i: (b, i, 0))],
        out_specs=pl.BlockSpec((1, tq, gw), lambda b, g, i: (b, i, g)),
        out_shape=jax.ShapeDtypeStruct((bsz, s, NSA_HEADS * HEAD_DIM), BF16),
        scratch_shapes=[pltpu.VMEM((LANES, tq), F32), pltpu.VMEM((HEAD_DIM, HPG * tq), F32),
                        pltpu.VMEM((HEAD_DIM, HPG * tq), F32)],
        compiler_params=_params("parallel", "parallel", "arbitrary"),
        name="slc_win_attention",
    )(q_r, ks, vs_t, kw, vw_t, sel_t, block_onehot, o_cmp, gates)


def _pad_cols(w, n):
    return jnp.pad(w, ((0, 0), (0, n - w.shape[1])))


def _conv_deltanet_mixer(xb, bsz, s, w_in, sc_conv_w, dn_conv_w, a_log, dt_bias, norm_w, w_out):
    sc_w = sc_conv_w.shape[1]
    dn_w = dn_conv_w.shape[1] // 3
    n_heads = dn_w // HEAD_DIM
    main = 3 * sc_w + 4 * dn_w
    proj, ba = _in_proj(xb, w_in.astype(BF16), main, _pad_cols(w_in[:, main:], LANES).astype(BF16),
                        tm=1024, tn=1024)
    proj = proj.reshape(bsz, s, main)
    y_sc = _short_conv(proj, sc_conv_w, sc_w, tc=256)
    y_dn = _deltanet(proj, 3 * sc_w, 3 * sc_w + 3 * dn_w, dn_conv_w, ba.reshape(bsz, s, LANES), a_log, dt_bias,
                     norm_w, n_heads)
    wo = w_out.astype(BF16)
    return [y_sc.reshape(bsz * s, sc_w), y_dn.reshape(bsz * s, dn_w)], [wo[:sc_w], wo[sc_w:]]


def _nsa_mixer(xb, bsz, s, positions, w_in, cmp_pos_k, cmp_w1_k, cmp_w2_k, cmp_pos_v, cmp_w1_v, cmp_w2_v, w_out):
    qw = NSA_HEADS * HEAD_DIM
    kvw = NSA_KV_HEADS * HEAD_DIM
    main = qw + 6 * kvw
    proj, gates = _in_proj(xb, w_in.astype(BF16), main, _pad_cols(w_in[:, main:], LANES).astype(BF16),
                           tm=1024, tn=1024)
    proj = proj.reshape(bsz, s, main)
    half = HEAD_DIM // 2
    inv = jnp.power(ROPE_THETA, -jnp.arange(half, dtype=F32) / half)
    inv = jnp.concatenate([inv, inv])
    ang = positions.astype(F32)[..., None] * inv
    cmp_end = jnp.minimum(jnp.arange(s // CMP_STRIDE) * CMP_STRIDE + CMP_BLOCK - 1, s - 1)
    ang_cmp = positions[:, cmp_end].astype(F32)[..., None] * inv
    q_r, ks, kw, vs_t, vw_t = _rope_qkv(proj, ang, ts=512, slc_chunk=256, win_chunk=LANES)
    k_cmp = _compress(proj, qw, cmp_pos_k, cmp_w1_k, cmp_w2_k, ang_cmp, rope=True)
    v_cmp = _compress(proj, qw + kvw, cmp_pos_v, cmp_w1_v, cmp_w2_v, ang_cmp, rope=False)
    o_cmp, sel_t = _cmp_attention(q_r, k_cmp, v_cmp, tq=256)
    o = _slc_win_attention(q_r, ks, vs_t, kw, vw_t, sel_t, o_cmp, gates.reshape(bsz, s, LANES), tq=LANES)
    return [o.reshape(bsz * s, qw)], [w_out.astype(BF16)]


def kernel(x, positions, ln_mix_g, ln_mix_b, ln_ffn_g, ln_ffn_b, ffn_w_in, ffn_w_out, hy_w_in, sc_conv_w, dn_conv_w, dn_a_log, dn_dt_bias, dn_norm_w, hy_w_out, nsa_w_in, cmp_pos_k, cmp_w1_k, cmp_w2_k, cmp_pos_v, cmp_w1_v, cmp_w2_v, nsa_w_out):
    bsz, s, d = x.shape
    xf = x.reshape(bsz * s, d)
    xb = xf
    ffn_w_out_b = ffn_w_out.astype(BF16)
    for i in range(DEPTH):
        j = i // 2
        if i % 2 == 0:
            ys, wos = _conv_deltanet_mixer(xb, bsz, s, hy_w_in[j], sc_conv_w[j], dn_conv_w[j], dn_a_log[j],
                                           dn_dt_bias[j], dn_norm_w[j], hy_w_out[j])
        else:
            ys, wos = _nsa_mixer(xb, bsz, s, positions, nsa_w_in[j], cmp_pos_k[j], cmp_w1_k[j], cmp_w2_k[j],
                                 cmp_pos_v[j], cmp_w1_v[j], cmp_w2_v[j], nsa_w_out[j])
        xf, xb = _matmul_ln(ys, wos, xf, ln_mix_g[i], ln_mix_b[i], tm=512, tn=1024)
        hmid = _ffn_in(xb, ffn_w_in, i, tm=1024, tn=512)
        xf, xb = _matmul_ln([hmid], [ffn_w_out_b], xf, ln_ffn_g[i], ln_ffn_b[i], tm=512, tn=512, layer=i)
    return xf.reshape(bsz, s, d)
```

```python
import functools
import math

import jax
import jax.numpy as jnp
from jax import lax
from jax.experimental import pallas as pl
from jax.experimental.pallas import tpu as pltpu

F32 = jnp.float32
BF16 = jnp.bfloat16
HIGHEST = lax.Precision.HIGHEST

LANES = 128
VMEM_LIMIT = 48 * 1024 * 1024
VMEM_LIMIT_BIG = 56 * 1024 * 1024

DN_HEADS = 8
DN_CHUNK = 64
DN_CONV = 4
SC_KERNEL = 3
NSA_HEADS = 16
NSA_KV_HEADS = 4
HPG = NSA_HEADS // NSA_KV_HEADS
HEAD_DIM = 128
CMP_BLOCK = 32
CMP_STRIDE = 16
SEL_BLOCK = 64
N_SELECT = 16
WINDOW = 512
ROPE_THETA = 10000.0
LN_EPS = 1e-5
NORM_EPS = 1e-6
NEG_INF = -1e30
DEPTH = 2
ALPHA = (2 * DEPTH) ** 0.25
ATTN_SCALE = HEAD_DIM ** -0.5
Q_SCALE = ATTN_SCALE * math.log2(math.e)


def _params(*sem, vmem=VMEM_LIMIT):
    return pltpu.CompilerParams(dimension_semantics=sem, vmem_limit_bytes=vmem)


def _sigmoid(x):
    return 1.0 / (1.0 + jnp.exp(-x))


def _silu(x):
    return x * _sigmoid(x)


def _dot(a, b):
    return jnp.dot(a, b, preferred_element_type=F32)


def _dot_nt(a, b):
    return lax.dot_general(a, b, (((1,), (1,)), ((), ())), preferred_element_type=F32)


def _dot_tn(a, b):
    return lax.dot_general(a, b, (((0,), (0,)), ((), ())), preferred_element_type=F32)


def _dot_hi(a, b):
    return jnp.dot(a, b, precision=HIGHEST, preferred_element_type=F32)


def _proj_kernel(x_ref, w_ref, ws_ref, o_ref, os_ref):
    xb = x_ref[...].astype(BF16)
    o_ref[...] = _dot(xb, w_ref[...])

    @pl.when(pl.program_id(1) == 0)
    def _():
        os_ref[...] = _dot(xb, ws_ref[...])


def _in_proj(x, w, n, w_side, *, tm, tn):
    m, k = x.shape
    ns = w_side.shape[1]
    return pl.pallas_call(
        _proj_kernel,
        grid=(m // tm, n // tn),
        in_specs=[pl.BlockSpec((tm, k), lambda i, j: (i, 0)),
                  pl.BlockSpec((k, tn), lambda i, j: (0, j)),
                  pl.BlockSpec((k, ns), lambda i, j: (0, 0))],
        out_specs=[pl.BlockSpec((tm, tn), lambda i, j: (i, j)),
                   pl.BlockSpec((tm, ns), lambda i, j: (i, 0))],
        out_shape=[jax.ShapeDtypeStruct((m, n), F32), jax.ShapeDtypeStruct((m, ns), F32)],
        compiler_params=_params("parallel", "arbitrary"),
        name="in_proj",
    )(x, w, w_side)


def _ffn_in_kernel(x_ref, wg_ref, wu_ref, o_ref, wgb_s, wub_s):
    @pl.when(pl.program_id(1) == 0)
    def _():
        wgb_s[...] = wg_ref[...].astype(BF16)
        wub_s[...] = wu_ref[...].astype(BF16)

    x = x_ref[...]
    gate = _dot(x, wgb_s[...])
    up = _dot(x, wub_s[...])
    o_ref[...] = (_silu(gate) * up).astype(o_ref.dtype)


def _ffn_in(xb, w_in, layer, *, tm, tn):
    m, k = xb.shape
    hidden = w_in.shape[2] // 2
    nj = hidden // tn
    return pl.pallas_call(
        _ffn_in_kernel,
        grid=(nj, m // tm),
        in_specs=[pl.BlockSpec((tm, k), lambda j, i: (i, 0)),
                  pl.BlockSpec((None, k, tn), lambda j, i: (layer, 0, j)),
                  pl.BlockSpec((None, k, tn), lambda j, i: (layer, 0, j + nj))],
        out_specs=pl.BlockSpec((tm, tn), lambda j, i: (i, j)),
        out_shape=jax.ShapeDtypeStruct((m, hidden), BF16),
        scratch_shapes=[pltpu.VMEM((k, tn), BF16), pltpu.VMEM((k, tn), BF16)],
        compiler_params=_params("parallel", "arbitrary"),
        name="ffn_in",
    )(xb, w_in, w_in)


MM_LN_ROWS = 128


def _mm_ln_kernel(*refs, n_pairs, nj):
    xs = refs[:n_pairs]
    ws = refs[n_pairs:2 * n_pairs]
    r_ref, g_ref, b_ref, o_ref, ob_ref, y_s = refs[2 * n_pairs:2 * n_pairs + 6]
    tm = o_ref.shape[0]
    j = pl.program_id(1)
    part = _dot(xs[0][...], ws[0][...])
    for x_ref, w_ref in zip(xs[1:], ws[1:]):
        part = part + _dot(x_ref[...], w_ref[...])
    y_s[j] = part

    @pl.when(j == nj - 1)
    def _():
        for r0 in range(0, tm, MM_LN_ROWS):
            rows = slice(r0, r0 + MM_LN_ROWS)
            y = jnp.concatenate([y_s[jj, rows, :] for jj in range(nj)], axis=1)
            v = ALPHA * r_ref[rows, :] + y
            mu = jnp.mean(v, axis=-1, keepdims=True)
            dv = v - mu
            var = jnp.mean(dv * dv, axis=-1, keepdims=True)
            out = dv * lax.rsqrt(var + LN_EPS) * g_ref[...] + b_ref[...]
            o_ref[rows, :] = out
            ob_ref[rows, :] = out.astype(BF16)


def _matmul_ln(xs, ws, resid, g, b, *, tm, tn, layer=None):
    m, d = resid.shape
    n_pairs = len(xs)
    nj = d // tn
    ks = [x.shape[1] for x in xs]
    if layer is None:
        w_specs = [pl.BlockSpec((k, tn), lambda i, j: (0, j)) for k in ks]
    else:
        w_specs = [pl.BlockSpec((None, k, tn), lambda i, j: (layer, 0, j)) for k in ks]
    in_specs = ([pl.BlockSpec((tm, k), lambda i, j: (i, 0)) for k in ks]
                + w_specs
                + [pl.BlockSpec((tm, d), lambda i, j: (i, 0)),
                   pl.BlockSpec((1, d), lambda i, j: (0, 0)),
                   pl.BlockSpec((1, d), lambda i, j: (0, 0))])
    return pl.pallas_call(
        functools.partial(_mm_ln_kernel, n_pairs=n_pairs, nj=nj),
        grid=(m // tm, nj),
        in_specs=in_specs,
        out_specs=[pl.BlockSpec((tm, d), lambda i, j: (i, 0)),
                   pl.BlockSpec((tm, d), lambda i, j: (i, 0))],
        out_shape=[jax.ShapeDtypeStruct((m, d), F32), jax.ShapeDtypeStruct((m, d), BF16)],
        scratch_shapes=[pltpu.VMEM((nj, tm, tn), F32)],
        compiler_params=_params("parallel", "arbitrary", vmem=VMEM_LIMIT_BIG),
        name="matmul_ln",
    )(*xs, *ws, resid, g.reshape(1, d), b.reshape(1, d))


def _causal_conv(u, w_ref, taps):
    def tap_sum(x, shift):
        acc = x * w_ref[taps - 1:taps, :]
        for sh in range(1, taps):
            acc = acc + shift(x, sh) * w_ref[taps - 1 - sh:taps - sh, :]
        return acc

    body = tap_sum(u, lambda x, sh: pltpu.roll(x, sh, axis=0))
    row = lax.broadcasted_iota(jnp.int32, (8, u.shape[1]), 0)
    head = tap_sum(u[0:8], lambda x, sh: jnp.where(row >= sh, pltpu.roll(x, sh, axis=0), 0.0))
    return jnp.concatenate([head, body[8:]], axis=0)


def _sc_kernel(b_ref, c_ref, h_ref, w_ref, o_ref):
    u = c_ref[0] * h_ref[0]
    o_ref[0] = (b_ref[0] * _causal_conv(u, w_ref, SC_KERNEL)).astype(o_ref.dtype)


def _short_conv(proj, conv_w, width, *, tc):
    bsz, s, _ = proj.shape
    nb = width // tc
    w = jnp.zeros((8, width), F32).at[:SC_KERNEL].set(conv_w)
    return pl.pallas_call(
        _sc_kernel,
        grid=(bsz, nb),
        in_specs=[pl.BlockSpec((1, s, tc), lambda b, j: (b, 0, j)),
                  pl.BlockSpec((1, s, tc), lambda b, j: (b, 0, j + nb)),
                  pl.BlockSpec((1, s, tc), lambda b, j: (b, 0, j + 2 * nb)),
                  pl.BlockSpec((8, tc), lambda b, j: (0, j))],
        out_specs=pl.BlockSpec((1, s, tc), lambda b, j: (b, 0, j)),
        out_shape=jax.ShapeDtypeStruct((bsz, s, width), BF16),
        compiler_params=_params("parallel", "parallel"),
        name="short_conv",
    )(proj, proj, proj, w)


def _split(x):
    hi = x.astype(BF16)
    return hi, (x - hi.astype(F32)).astype(BF16)


def _dotb(a, b):
    return _dot(a.astype(BF16), b.astype(BF16))


DN_GROUP = 8


def _dn_group_local(base, scr, masks, out):
    q_s, k_s, kb_s, qd_s, kf_s, kbe_s, vb_s, gc_s, gcd_s = scr
    incl, strict, m8, m16, eye = masks
    c = DN_CHUNK
    idx = range(DN_GROUP)
    rows = [pl.ds(base + cc * c, c) for cc in idx]
    gc = [gc_s[r, :] for r in rows]
    decay = []
    for cc in idx:
        gc_j = gcd_s[pl.ds(base // LANES + cc // 2, 1), (cc % 2) * c:(cc % 2) * c + c]
        decay.append(jnp.where(incl, jnp.exp(jnp.where(incl, gc[cc][:, :c] - gc_j, 0.0)), 0.0))
    kbf = [k_s[r, :] for r in rows]
    kk = [_dot_nt(kb_s[rows[cc], :], kbf[cc]) for cc in idx]
    qk = [_dot_nt(q_s[rows[cc], :], kbf[cc]) for cc in idx]
    yield
    a = [jnp.where(strict, kk[cc] * decay[cc], 0.0) for cc in idx]
    intra = [(qk[cc] * decay[cc]).astype(BF16) for cc in idx]
    ad = [jnp.where(m8, x, 0.0) for x in a]
    adb = [x.astype(BF16) for x in ad]
    a2 = [_dot(x, x) for x in adb]
    yield
    a2b = [x.astype(BF16) for x in a2]
    p = [eye - x for x in ad]
    p1, a4 = [], []
    for cc in idx:
        p1.append(p[cc] + _dotb(p[cc], a2b[cc]))
        a4.append(_dot(a2b[cc], a2b[cc]))
    yield
    p2 = [p1[cc] + _dotb(p1[cc], a4[cc]) for cc in idx]
    yield
    pb = [x.astype(BF16) for x in p2]
    t = [_dotb(pb[cc], jnp.where(m16, a[cc] - ad[cc], 0.0)) for cc in idx]
    yield
    dinv = [(p2[cc] - _dotb(t[cc], pb[cc])).astype(BF16) for cc in idx]
    yield
    db, da = [], []
    for cc in idx:
        rhs = jnp.concatenate([vb_s[rows[cc], :], kbe_s[rows[cc], :]], axis=1)
        db.append(_dotb(dinv[cc], rhs))
        da.append(_dotb(dinv[cc], jnp.where(m16, 0.0, a[cc])).astype(BF16))
    yield
    blocks = [[x[0:16]] for x in db]
    for s4 in range(1, c // 16):
        rs = slice(16 * s4, 16 * s4 + 16)
        for cc in idx:
            xprev = jnp.concatenate(blocks[cc] + [jnp.zeros((c - 16 * s4, 2 * HEAD_DIM), F32)], axis=0)
            blocks[cc].append(db[cc][rs] - _dotb(da[cc][rs], xprev))
        yield
    for cc in idx:
        sol = jnp.concatenate(blocks[cc], axis=0)
        sol_hi, sol_lo = _split(sol)
        g_last = gc[cc][c - 1:c, :]
        k_dec_t = (kf_s[rows[cc], :] * jnp.exp(g_last - gc[cc])).T.astype(BF16)
        kw = _dot(k_dec_t, sol_hi) + _dot(k_dec_t, sol_lo)
        iw = _dot(intra[cc], sol_hi) + _dot(intra[cc], sol_lo)
        out.append((kw[:, HEAD_DIM:].astype(BF16), kw[:, :HEAD_DIM],
                    (qd_s[rows[cc], :] - iw[:, HEAD_DIM:]).astype(BF16), iw[:, :HEAD_DIM], jnp.exp(g_last)))
    yield


def _dn_chunk_seq(state, loc, z, nw):
    w2, n_mat, qp, op, eg_last = loc
    sb = state.astype(BF16)
    o = _dot(qp, sb) + op
    state = (state * eg_last - _dot(w2, sb)) + n_mat
    o = o * lax.rsqrt(jnp.mean(o * o, axis=-1, keepdims=True) + NORM_EPS) * nw * _silu(z)
    return state, o


def _dn_kernel(qp_ref, kp_ref, vp_ref, z_ref, ba_ref, arow_ref, dtrow_ref, cwq_ref, cwk_ref, cwv_ref, nw_ref, o_ref,
               q_s, k_s, kb_s, qd_s, kf_s, kbe_s, vb_s, gc_s, beta_s, gates_s, gcd_s, *, n_heads):
    h = pl.program_id(1)
    s = qp_ref.shape[1]
    c = DN_CHUNK
    @pl.when(h == 0)
    def _():
        ba = ba_ref[0]
        xa = ba + dtrow_ref[...]
        softplus = jnp.maximum(xa, 0.0) + jnp.log(1.0 + jnp.exp(-jnp.abs(xa)))
        lane0 = lax.broadcasted_iota(jnp.int32, ba.shape, 1)
        gates = jnp.where(lane0 < n_heads, _sigmoid(ba), -jnp.exp(arow_ref[...]) * softplus)
        for blk in range(s // LANES):
            rs = slice(blk * LANES, (blk + 1) * LANES)
            gates_s[rs, :] = gates[rs, :].T

    n_blk = s // LANES
    beta_d = jnp.concatenate([gates_s[pl.ds(blk * LANES + h, 1), :] for blk in range(n_blk)], axis=0)
    gc_d = jnp.concatenate([gates_s[pl.ds(blk * LANES + h + n_heads, 1), :] for blk in range(n_blk)], axis=0)
    pos = lax.broadcasted_iota(jnp.int32, (n_blk, LANES), 1) & (c - 1)
    sh = 1
    while sh < c:
        gc_d = gc_d + jnp.where(pos >= sh, pltpu.roll(gc_d, sh, axis=1), 0.0)
        sh *= 2
    gcd_s[0:n_blk, :] = gc_d
    for blk in range(n_blk):
        rs = slice(blk * LANES, (blk + 1) * LANES)
        gc_s[rs, :] = jnp.broadcast_to(gc_d[blk:blk + 1, :], (LANES, LANES)).T
        beta_s[rs, :] = jnp.broadcast_to(beta_d[blk:blk + 1, :], (LANES, LANES)).T
    gc = gc_s[...]
    beta = beta_s[...]
    eg = jnp.exp(gc)
    q = _silu(_causal_conv(qp_ref[0], cwq_ref, DN_CONV))
    q = q * (lax.rsqrt(jnp.sum(q * q, axis=-1, keepdims=True) + NORM_EPS) * (HEAD_DIM ** -0.5))
    q_s[...] = q.astype(BF16)
    qd_s[...] = q * eg
    k = _silu(_causal_conv(kp_ref[0], cwk_ref, DN_CONV))
    k = k * lax.rsqrt(jnp.sum(k * k, axis=-1, keepdims=True) + NORM_EPS)
    kb = k * beta
    kf_s[...] = k
    k_s[...] = k.astype(BF16)
    kb_s[...] = kb.astype(BF16)
    kbe_s[...] = kb * eg
    vb_s[...] = _silu(_causal_conv(vp_ref[0], cwv_ref, DN_CONV)) * beta

    row = lax.broadcasted_iota(jnp.int32, (c, c), 0)
    col = lax.broadcasted_iota(jnp.int32, (c, c), 1)
    masks = (row >= col, row > col, (row >> 3) == (col >> 3), (row >> 4) == (col >> 4), (row == col).astype(F32))
    scr = (q_s, k_s, kb_s, qd_s, kf_s, kbe_s, vb_s, gc_s, gcd_s)
    nw = nw_ref[...]
    rows_per_group = DN_GROUP * c
    n_groups = s // rows_per_group

    def group_base(gi):
        base = gi * rows_per_group
        return base if isinstance(base, int) else pl.multiple_of(base, rows_per_group)

    def run(gi_local, gi_seq, state, locs):
        nxt = []
        stages = iter(()) if gi_local is None else _dn_group_local(group_base(gi_local), scr, masks, nxt)
        todo = list(range(DN_GROUP)) if gi_seq is not None else []
        done = False
        while todo or not done:
            if not done:
                done = next(stages, "end") == "end"
            if todo:
                cc = todo.pop(0)
                rows = pl.ds(group_base(gi_seq) + cc * c, c)
                state, o = _dn_chunk_seq(state, locs[cc], z_ref[0, rows, :], nw)
                o_ref[0, rows, :] = o.astype(o_ref.dtype)
        return state, tuple(nxt)

    def body(gi, carry):
        return run(gi + 1, gi, *carry)

    carry = run(0, None, jnp.zeros((HEAD_DIM, HEAD_DIM), F32), None)
    carry = lax.fori_loop(0, n_groups - 1, body, carry)
    run(None, n_groups - 1, *carry)


def _deltanet(proj, qkv_col0, z_col0, conv_w, ba, a_log, dt_bias, norm_w, n_heads):
    bsz, s, _ = proj.shape
    qb0 = qkv_col0 // HEAD_DIM
    zb0 = z_col0 // HEAD_DIM
    arow = jnp.zeros((1, LANES), F32).at[0, n_heads:2 * n_heads].set(a_log)
    dtrow = jnp.zeros((1, LANES), F32).at[0, n_heads:2 * n_heads].set(dt_bias)
    cw = jnp.zeros((8, 3 * n_heads * HEAD_DIM), F32).at[:DN_CONV].set(conv_w)
    blk = (1, s, HEAD_DIM)
    col_spec = lambda off: pl.BlockSpec(blk, lambda b, h, off=off: (b, 0, h + off))
    cw_spec = lambda off: pl.BlockSpec((8, HEAD_DIM), lambda b, h, off=off: (0, h + off))
    row_spec = pl.BlockSpec((1, LANES), lambda b, h: (0, 0))
    return pl.pallas_call(
        functools.partial(_dn_kernel, n_heads=n_heads),
        grid=(bsz, n_heads),
        in_specs=[col_spec(qb0), col_spec(qb0 + n_heads), col_spec(qb0 + 2 * n_heads), col_spec(zb0),
                  pl.BlockSpec((1, s, LANES), lambda b, h: (b, 0, 0)), row_spec, row_spec,
                  cw_spec(0), cw_spec(n_heads), cw_spec(2 * n_heads), row_spec],
        out_specs=pl.BlockSpec(blk, lambda b, h: (b, 0, h)),
        out_shape=jax.ShapeDtypeStruct((bsz, s, n_heads * HEAD_DIM), BF16),
        scratch_shapes=([pltpu.VMEM((s, HEAD_DIM), BF16)] * 3 + [pltpu.VMEM((s, HEAD_DIM), F32)] * 7
                        + [pltpu.VMEM((max(8, s // LANES), LANES), F32)]),
        compiler_params=_params("parallel", "arbitrary"),
        name="deltanet",
    )(proj, proj, proj, proj, ba, arow, dtrow, cw, cw, cw, norm_w.reshape(1, HEAD_DIM))


def _rope_tables(ang):
    lane = lax.broadcasted_iota(jnp.int32, ang.shape, 1)
    sin = jnp.sin(ang)
    return jnp.cos(ang), jnp.where(lane < HEAD_DIM // 2, -sin, sin)


def _rope(x, cos, sin_signed):
    return x * cos + pltpu.roll(x, HEAD_DIM // 2, axis=1) * sin_signed


def _rope_kernel(ang_ref, q_ref, ks_ref, kw_ref, vs_ref, vw_ref, qo_ref, kso_ref, kwo_ref, vso_ref, vwo_ref):
    cos, sin = _rope_tables(ang_ref[0])
    for hh in range(NSA_HEADS):
        sl = slice(hh * HEAD_DIM, (hh + 1) * HEAD_DIM)
        qo_ref[0, :, sl] = (_rope(q_ref[0, :, sl], cos, sin) * Q_SCALE).astype(BF16)
    for g in range(NSA_KV_HEADS):
        sl = slice(g * HEAD_DIM, (g + 1) * HEAD_DIM)
        kso_ref[0, :, sl] = _rope(ks_ref[0, :, sl], cos, sin).astype(BF16)
        kwo_ref[0, :, sl] = _rope(kw_ref[0, :, sl], cos, sin).astype(BF16)
    ts = vs_ref.shape[1]
    for v_ref, vo_ref in ((vs_ref, vso_ref), (vw_ref, vwo_ref)):
        ck = vo_ref.shape[4]
        for g in range(NSA_KV_HEADS):
            for cc in range(ts // ck):
                parts = [v_ref[0, cc * ck + r:cc * ck + r + LANES, g * HEAD_DIM:(g + 1) * HEAD_DIM].T
                         for r in range(0, ck, LANES)]
                vo_ref[0, g, cc] = jnp.concatenate(parts, axis=1).astype(BF16)


def _rope_qkv(proj, ang, *, ts, slc_chunk, win_chunk):
    bsz, s, _ = proj.shape
    qw = NSA_HEADS * HEAD_DIM
    kvw = NSA_KV_HEADS * HEAD_DIM
    kv_spec = lambda blk: pl.BlockSpec((1, ts, kvw), lambda b, i, blk=blk: (b, i, blk))
    kv_out = pl.BlockSpec((1, ts, kvw), lambda b, i: (b, i, 0))
    kv_shape = jax.ShapeDtypeStruct((bsz, s, kvw), BF16)
    vt_out = lambda ck: pl.BlockSpec((1, NSA_KV_HEADS, ts // ck, HEAD_DIM, ck), lambda b, i: (b, 0, i, 0, 0))
    vt_shape = lambda ck: jax.ShapeDtypeStruct((bsz, NSA_KV_HEADS, s // ck, HEAD_DIM, ck), BF16)
    base = qw // kvw
    return pl.pallas_call(
        _rope_kernel,
        grid=(bsz, s // ts),
        in_specs=[pl.BlockSpec((1, ts, HEAD_DIM), lambda b, i: (b, i, 0)),
                  pl.BlockSpec((1, ts, qw), lambda b, i: (b, i, 0)),
                  kv_spec(base + 2), kv_spec(base + 4), kv_spec(base + 3), kv_spec(base + 5)],
        out_specs=[pl.BlockSpec((1, ts, qw), lambda b, i: (b, i, 0)), kv_out, kv_out,
                   vt_out(slc_chunk), vt_out(win_chunk)],
        out_shape=[jax.ShapeDtypeStruct((bsz, s, qw), BF16), kv_shape, kv_shape,
                   vt_shape(slc_chunk), vt_shape(win_chunk)],
        compiler_params=_params("parallel", "parallel"),
        name="rope_qkv",
    )(ang, proj, proj, proj, proj, proj)


def _gelu_tanh(x):
    return x * (0.5 * (1.0 + jnp.tanh(math.sqrt(2.0 / math.pi) * (x + 0.044715 * (x * x * x)))))


def _compress_kernel(x_ref, w1_ref, w2_ref, pos_ref, ang_ref, o_ref, *, rope):
    nsub = x_ref.shape[1] // CMP_STRIDE
    hid = w1_ref.shape[1]
    pa = jnp.zeros((nsub, hid), F32)
    pb = jnp.zeros((nsub, hid), F32)
    for l in range(CMP_STRIDE):
        xl = x_ref[0, pl.ds(l, nsub, stride=CMP_STRIDE), :].astype(BF16)
        pa = pa + _dot(xl, w1_ref[l * HEAD_DIM:(l + 1) * HEAD_DIM, :])
        pb = pb + _dot(xl, w1_ref[(CMP_STRIDE + l) * HEAD_DIM:(CMP_STRIDE + l + 1) * HEAD_DIM, :])
    bias = _dot(pos_ref[...], w1_ref[...])[0:1, :]
    hpre = pa + pltpu.roll(pb, nsub - 1, axis=0) + bias
    out = _dot(_gelu_tanh(hpre).astype(BF16), w2_ref[...])
    if rope:
        cos, sin = _rope_tables(ang_ref[0])
        out = _rope(out, cos, sin)
    o_ref[0, 0] = out.astype(o_ref.dtype)


def _compress(proj, col0, pos_emb, w1, w2, ang_cmp, *, rope):
    bsz, s, _ = proj.shape
    nsub = s // CMP_STRIDE
    blk0 = col0 // HEAD_DIM
    hid = w1.shape[1]
    pos = jnp.zeros((8, CMP_BLOCK * HEAD_DIM), BF16).at[0].set(pos_emb.reshape(-1).astype(BF16))
    return pl.pallas_call(
        functools.partial(_compress_kernel, rope=rope),
        grid=(bsz, NSA_KV_HEADS),
        in_specs=[pl.BlockSpec((1, s, HEAD_DIM), lambda b, g: (b, 0, g + blk0)),
                  pl.BlockSpec((CMP_BLOCK * HEAD_DIM, hid), lambda b, g: (0, 0)),
                  pl.BlockSpec((hid, HEAD_DIM), lambda b, g: (0, 0)),
                  pl.BlockSpec((8, CMP_BLOCK * HEAD_DIM), lambda b, g: (0, 0)),
                  pl.BlockSpec((1, nsub, HEAD_DIM), lambda b, g: (b, 0, 0))],
        out_specs=pl.BlockSpec((1, 1, nsub, HEAD_DIM), lambda b, g: (b, g, 0, 0)),
        out_shape=jax.ShapeDtypeStruct((bsz, NSA_KV_HEADS, nsub, HEAD_DIM), BF16),
        compiler_params=_params("parallel", "parallel"),
        name="compress",
    )(proj, w1.astype(BF16), w2.astype(BF16), pos, ang_cmp)


def _cmp_attn_kernel(q_ref, kc_ref, vc_ref, smat_ref, o_ref, sel_ref):
    tq = q_ref.shape[1]
    ncol = kc_ref.shape[2]
    t = pl.program_id(2) * tq + lax.broadcasted_iota(jnp.int32, (tq, ncol), 0)
    n = lax.broadcasted_iota(jnp.int32, (tq, ncol), 1)
    valid = (n * CMP_STRIDE + CMP_BLOCK - 1) <= t
    kc = kc_ref[0, 0]
    vc = vc_ref[0, 0]
    p_grp = jnp.zeros((tq, ncol), F32)
    for hh in range(HPG):
        sl = slice(hh * HEAD_DIM, (hh + 1) * HEAD_DIM)
        sc = jnp.where(valid, _dot_nt(q_ref[0, :, sl], kc), NEG_INF)
        e = jnp.exp2(sc - jnp.max(sc, axis=-1, keepdims=True))
        p = jnp.where(valid, e / jnp.sum(e, axis=-1, keepdims=True), 0.0)
        o_ref[0, :, sl] = _dot(p.astype(BF16), vc)
        p_grp = p_grp + p
    score = _dot_hi(p_grp, smat_ref[...])
    n_sel = sel_ref.shape[2]
    score = jnp.concatenate([score[r:r + LANES].T for r in range(0, tq, LANES)], axis=1)[:n_sel]
    n = lax.broadcasted_iota(jnp.int32, (n_sel, tq), 0)
    t = pl.program_id(2) * tq + lax.broadcasted_iota(jnp.int32, (n_sel, tq), 1)
    cur = t >> int(math.log2(SEL_BLOCK))
    forced = (n == 0) | (n == cur) | (n == cur - 1)
    future = n * SEL_BLOCK > t
    score = jnp.where(forced, jnp.inf, jnp.where(future, -jnp.inf, score))
    rank = jnp.zeros((n_sel, tq), jnp.int32)
    for kk in range(n_sel):
        ck = score[kk:kk + 1, :]
        ahead = (ck > score) | ((ck == score) & (kk < n))
        rank = rank + ahead.astype(jnp.int32)
    sel_ref[0, 0] = (rank < N_SELECT).astype(sel_ref.dtype)


def _sel_matrix(ncol, n_sel):
    rs = SEL_BLOCK // CMP_STRIDE
    rc = CMP_BLOCK // CMP_STRIDE
    mat = [[0.0] * ncol for _ in range(ncol)]
    for j in range(n_sel):
        for m in range(rs):
            for n in range(rc):
                i = rs * j + m + n - (rc - 1)
                if 0 <= i < ncol - 1:
                    mat[i][j] += 1.0
    return jnp.array(mat, F32)


def _cmp_attention(q_r, k_cmp, v_cmp, *, tq):
    bsz, s, _ = q_r.shape
    ncol = k_cmp.shape[2]
    n_sel = s // SEL_BLOCK
    gw = HPG * HEAD_DIM
    return pl.pallas_call(
        _cmp_attn_kernel,
        grid=(bsz, NSA_KV_HEADS, s // tq),
        in_specs=[pl.BlockSpec((1, tq, gw), lambda b, g, i: (b, i, g)),
                  pl.BlockSpec((1, 1, ncol, HEAD_DIM), lambda b, g, i: (b, g, 0, 0)),
                  pl.BlockSpec((1, 1, ncol, HEAD_DIM), lambda b, g, i: (b, g, 0, 0)),
                  pl.BlockSpec((ncol, ncol), lambda b, g, i: (0, 0))],
        out_specs=[pl.BlockSpec((1, tq, gw), lambda b, g, i: (b, i, g)),
                   pl.BlockSpec((1, 1, n_sel, tq), lambda b, g, i: (b, g, 0, i))],
        out_shape=[jax.ShapeDtypeStruct((bsz, s, NSA_HEADS * HEAD_DIM), F32),
                   jax.ShapeDtypeStruct((bsz, NSA_KV_HEADS, n_sel, s), F32)],
        compiler_params=_params("parallel", "parallel", "parallel"),
        name="cmp_attention",
    )(q_r, k_cmp, v_cmp, _sel_matrix(ncol, n_sel))


def _softmax_merge_t(parts):
    if len(parts) == 1:
        return parts[0][0] / parts[0][2]
    m = functools.reduce(jnp.maximum, [p[1] for p in parts])
    scale = [jnp.exp2(p[1] - m) for p in parts]
    acc = functools.reduce(lambda x, y: x + y, [p[0] * s for p, s in zip(parts, scale)])
    l = functools.reduce(lambda x, y: x + y, [p[2] * s for p, s in zip(parts, scale)])
    return acc / l


def _softmax_partial_t(scores, values_t):
    r = scores[0].shape[1]
    m8 = None
    for sc in scores:
        c8 = jnp.max(sc.reshape(-1, 8, r), axis=0)
        m8 = c8 if m8 is None else jnp.maximum(m8, c8)
    m = jnp.max(m8, axis=0, keepdims=True)
    l8 = None
    probs = []
    for sc in scores:
        p = jnp.exp2(sc - m)
        p8 = jnp.sum(p.reshape(-1, 8, r), axis=0)
        l8 = p8 if l8 is None else l8 + p8
        probs.append(p.astype(BF16))
    l = jnp.sum(l8, axis=0, keepdims=True)
    acc = _dot(jnp.concatenate(values_t, axis=1), jnp.concatenate(probs, axis=0))
    return acc, m, l


MASK_BIG = 2.0 ** 100


def _slc_win_kernel(q_ref, ks_ref, vst_ref, kw_ref, vwt_ref, selt_ref, blk_ref, oc_ref, gate_ref, o_ref,
                    gt_s, os_s, ow_s):
    tq = q_ref.shape[1]
    tk = vst_ref.shape[4]
    n_chunks = vst_ref.shape[2]
    n_sel = selt_ref.shape[2]
    g = pl.program_id(1)
    t0 = pl.program_id(2) * tq
    q_t = jnp.concatenate([q_ref[0, :, hh * HEAD_DIM:(hh + 1) * HEAD_DIM].astype(F32).T.astype(BF16)
                           for hh in range(HPG)], axis=1)

    def tile_heads(mask):
        return jnp.concatenate([mask] * HPG, axis=1)

    key_loc = lax.broadcasted_iota(jnp.int32, (tq, tq), 0)
    qry_loc = lax.broadcasted_iota(jnp.int32, (tq, tq), 1)
    n_win = WINDOW // tq + 1

    def window_scores(in_range):
        if in_range:
            k0 = pl.multiple_of(t0 - WINDOW, tq)
            sc = _dot(kw_ref[0, pl.ds(k0, n_win * tq), :], q_t)
            scores = [sc[cc * tq:(cc + 1) * tq] for cc in range(n_win)]
            scores[0] = jnp.where(tile_heads(key_loc > qry_loc), scores[0], NEG_INF)
            scores[-1] = jnp.where(tile_heads(key_loc <= qry_loc), scores[-1], NEG_INF)
            return scores, [vwt_ref[0, 0, k0 // tq + cc] for cc in range(n_win)]
        sc = _dot(kw_ref[0, 0:WINDOW, :], q_t)
        scores = [jnp.where(tile_heads(cc * tq + key_loc <= t0 + qry_loc), sc[cc * tq:(cc + 1) * tq], NEG_INF)
                  for cc in range(WINDOW // tq)]
        return scores, [vwt_ref[0, 0, cc] for cc in range(WINDOW // tq)]

    sel_bias = ((selt_ref[0, 0] - 1.0) * MASK_BIG).astype(BF16)
    q_aug = jnp.concatenate([q_t, tile_heads(sel_bias),
                             jnp.zeros((HEAD_DIM - n_sel, HPG * tq), BF16)], axis=0)
    k_loc = lax.broadcasted_iota(jnp.int32, (tk, tq), 0)
    t_loc = lax.broadcasted_iota(jnp.int32, (tk, tq), 1)

    def tile_variant(n_used):
        bounds = [0, n_used] if n_used == 1 else [0, n_used // 2, n_used]
        groups = []
        for lo, hi in zip(bounds[:-1], bounds[1:]):
            rows = slice(lo * tk, hi * tk)
            sc = _dot(jnp.concatenate([ks_ref[0, rows, :], blk_ref[rows, :]], axis=1), q_aug)
            groups.append([sc[cc * tk:(cc + 1) * tk] for cc in range(hi - lo)])
        causal = ((n_used - 1) * tk + k_loc) <= (t0 + t_loc)
        groups[-1][-1] = jnp.where(tile_heads(causal), groups[-1][-1], NEG_INF)
        win_scores, win_values = window_scores((n_used - 1) * tk >= WINDOW)
        parts = [_softmax_partial_t(groups[0], [vst_ref[0, 0, cc] for cc in range(bounds[0], bounds[1])])]
        ow_s[...] = _softmax_merge_t([_softmax_partial_t(win_scores, win_values)])
        if len(groups) > 1:
            parts.append(_softmax_partial_t(groups[1], [vst_ref[0, 0, cc] for cc in range(bounds[1], bounds[2])]))
        os_s[...] = _softmax_merge_t(parts)

    for vv in range(n_chunks):
        pl.when(t0 // tk == vv)(functools.partial(tile_variant, vv + 1))

    gt = _sigmoid(gate_ref[0])
    gt_s[...] = gt.T
    lane = lax.broadcasted_iota(jnp.int32, gt.shape, 1)
    for hh in range(HPG):
        base = (g * HPG + hh) * 3
        g_cmp = jnp.sum(jnp.where(lane == base, gt, 0.0), axis=1, keepdims=True)
        cs = slice(hh * tq, (hh + 1) * tq)
        mix_t = gt_s[pl.ds(base + 1, 1), :] * os_s[:, cs] + gt_s[pl.ds(base + 2, 1), :] * ow_s[:, cs]
        sl = slice(hh * HEAD_DIM, (hh + 1) * HEAD_DIM)
        o_ref[0, :, sl] = (g_cmp * oc_ref[0, :, sl] + mix_t.T).astype(o_ref.dtype)


def _slc_win_attention(q_r, ks, vs_t, kw, vw_t, sel_t, o_cmp, gates, *, tq):
    bsz, s, _ = q_r.shape
    n_sel = s // SEL_BLOCK
    gw = HPG * HEAD_DIM
    assert vw_t.shape[4] == tq and tq == LANES and vs_t.shape[4] % tq == 0
    kv_spec = pl.BlockSpec((1, s, HEAD_DIM), lambda b, g, i: (b, 0, g))
    vt_spec = lambda a: pl.BlockSpec((1, 1) + a.shape[2:], lambda b, g, i: (b, g, 0, 0, 0))
    block_onehot = (jnp.arange(s)[:, None] // SEL_BLOCK == jnp.arange(HEAD_DIM)[None, :]).astype(BF16)
    return pl.pallas_call(
        _slc_win_kernel,
        grid=(bsz, NSA_KV_HEADS, s // tq),
        in_specs=[pl.BlockSpec((1, tq, gw), lambda b, g, i: (b, i, g)),
                  kv_spec, vt_spec(vs_t), kv_spec, vt_spec(vw_t),
                  pl.BlockSpec((1, 1, n_sel, tq), lambda b, g, i: (b, g, 0, i)),
                  pl.BlockSpec((s, HEAD_DIM), lambda b, g, i: (0, 0)),
                  pl.BlockSpec((1, tq, gw), lambda b, g, i: (b, i, g)),
                  pl.BlockSpec((1, tq, LANES), lambda b, g, i: (b, i, 0))],
        out_specs=pl.BlockSpec((1, tq, gw), lambda b, g, i: (b, i, g)),
        out_shape=jax.ShapeDtypeStruct((bsz, s, NSA_HEADS * HEAD_DIM), BF16),
        scratch_shapes=[pltpu.VMEM((LANES, tq), F32), pltpu.VMEM((HEAD_DIM, HPG * tq), F32),
                        pltpu.VMEM((HEAD_DIM, HPG * tq), F32)],
        compiler_params=_params("parallel", "parallel", "arbitrary"),
        name="slc_win_attention",
    )(q_r, ks, vs_t, kw, vw_t, sel_t, block_onehot, o_cmp, gates)


def _pad_cols(w, n):
    return jnp.pad(w, ((0, 0), (0, n - w.shape[1])))


def _conv_deltanet_mixer(xb, bsz, s, w_in, sc_conv_w, dn_conv_w, a_log, dt_bias, norm_w, w_out):
    sc_w = sc_conv_w.shape[1]
    dn_w = dn_conv_w.shape[1] // 3
    n_heads = dn_w // HEAD_DIM
    main = 3 * sc_w + 4 * dn_w
    proj, ba = _in_proj(xb, w_in.astype(BF16), main, _pad_cols(w_in[:, main:], LANES).astype(BF16),
                        tm=1024, tn=1024)
    proj = proj.reshape(bsz, s, main)
    y_sc = _short_conv(proj, sc_conv_w, sc_w, tc=256)
    y_dn = _deltanet(proj, 3 * sc_w, 3 * sc_w + 3 * dn_w, dn_conv_w, ba.reshape(bsz, s, LANES), a_log, dt_bias,
                     norm_w, n_heads)
    wo = w_out.astype(BF16)
    return [y_sc.reshape(bsz * s, sc_w), y_dn.reshape(bsz * s, dn_w)], [wo[:sc_w], wo[sc_w:]]


def _nsa_mixer(xb, bsz, s, positions, w_in, cmp_pos_k, cmp_w1_k, cmp_w2_k, cmp_pos_v, cmp_w1_v, cmp_w2_v, w_out):
    qw = NSA_HEADS * HEAD_DIM
    kvw = NSA_KV_HEADS * HEAD_DIM
    main = qw + 6 * kvw
    proj, gates = _in_proj(xb, w_in.astype(BF16), main, _pad_cols(w_in[:, main:], LANES).astype(BF16),
                           tm=1024, tn=1024)
    proj = proj.reshape(bsz, s, main)
    half = HEAD_DIM // 2
    inv = jnp.power(ROPE_THETA, -jnp.arange(half, dtype=F32) / half)
    inv = jnp.concatenate([inv, inv])
    ang = positions.astype(F32)[..., None] * inv
    cmp_end = jnp.minimum(jnp.arange(s // CMP_STRIDE) * CMP_STRIDE + CMP_BLOCK - 1, s - 1)
    ang_cmp = positions[:, cmp_end].astype(F32)[..., None] * inv
    q_r, ks, kw, vs_t, vw_t = _rope_qkv(proj, ang, ts=512, slc_chunk=256, win_chunk=LANES)
    k_cmp = _compress(proj, qw, cmp_pos_k, cmp_w1_k, cmp_w2_k, ang_cmp, rope=True)
    v_cmp = _compress(proj, qw + kvw, cmp_pos_v, cmp_w1_v, cmp_w2_v, ang_cmp, rope=False)
    o_cmp, sel_t = _cmp_attention(q_r, k_cmp, v_cmp, tq=512)
    o = _slc_win_attention(q_r, ks, vs_t, kw, vw_t, sel_t, o_cmp, gates.reshape(bsz, s, LANES), tq=LANES)
    return [o.reshape(bsz * s, qw)], [w_out.astype(BF16)]


def kernel(x, positions, ln_mix_g, ln_mix_b, ln_ffn_g, ln_ffn_b, ffn_w_in, ffn_w_out, hy_w_in, sc_conv_w, dn_conv_w, dn_a_log, dn_dt_bias, dn_norm_w, hy_w_out, nsa_w_in, cmp_pos_k, cmp_w1_k, cmp_w2_k, cmp_pos_v, cmp_w1_v, cmp_w2_v, nsa_w_out):
    bsz, s, d = x.shape
    xf = x.reshape(bsz * s, d)
    xb = xf
    ffn_w_out_b = ffn_w_out.astype(BF16)
    for i in range(DEPTH):
        j = i // 2
        if i % 2 == 0:
            ys, wos = _conv_deltanet_mixer(xb, bsz, s, hy_w_in[j], sc_conv_w[j], dn_conv_w[j], dn_a_log[j],
                                           dn_dt_bias[j], dn_norm_w[j], hy_w_out[j])
        else:
            ys, wos = _nsa_mixer(xb, bsz, s, positions, nsa_w_in[j], cmp_pos_k[j], cmp_w1_k[j], cmp_w2_k[j],
                                 cmp_pos_v[j], cmp_w1_v[j], cmp_w2_v[j], nsa_w_out[j])
        xf, xb = _matmul_ln(ys, wos, xf, ln_mix_g[i], ln_mix_b[i], tm=512, tn=d)
        hmid = _ffn_in(xb, ffn_w_in, i, tm=1024, tn=512)
        xf, xb = _matmul_ln([hmid], [ffn_w_out_b], xf, ln_ffn_g[i], ln_ffn_b[i], tm=512, tn=512, layer=i)
    return xf.reshape(bsz, s, d)
```

```python
import functools
import math

import jax
import jax.numpy as jnp
from jax import lax
from jax.experimental import pallas as pl
from jax.experimental.pallas import tpu as pltpu

F32 = jnp.float32
BF16 = jnp.bfloat16
HIGHEST = lax.Precision.HIGHEST

LANES = 128
VMEM_LIMIT = 48 * 1024 * 1024
VMEM_LIMIT_BIG = 56 * 1024 * 1024

DN_HEADS = 8
DN_CHUNK = 64
DN_CONV = 4
SC_KERNEL = 3
NSA_HEADS = 16
NSA_KV_HEADS = 4
HPG = NSA_HEADS // NSA_KV_HEADS
HEAD_DIM = 128
CMP_BLOCK = 32
CMP_STRIDE = 16
SEL_BLOCK = 64
N_SELECT = 16
WINDOW = 512
ROPE_THETA = 10000.0
LN_EPS = 1e-5
NORM_EPS = 1e-6
NEG_INF = -1e30
DEPTH = 2
ALPHA = (2 * DEPTH) ** 0.25
ATTN_SCALE = HEAD_DIM ** -0.5
Q_SCALE = ATTN_SCALE * math.log2(math.e)


def _params(*sem, vmem=VMEM_LIMIT):
    return pltpu.CompilerParams(dimension_semantics=sem, vmem_limit_bytes=vmem)


def _sigmoid(x):
    return 1.0 / (1.0 + jnp.exp(-x))


def _silu(x):
    return x * _sigmoid(x)


def _dot(a, b):
    return jnp.dot(a, b, preferred_element_type=F32)


def _dot_nt(a, b):
    return lax.dot_general(a, b, (((1,), (1,)), ((), ())), preferred_element_type=F32)


def _dot_tn(a, b):
    return lax.dot_general(a, b, (((0,), (0,)), ((), ())), preferred_element_type=F32)


def _dot_hi(a, b):
    return jnp.dot(a, b, precision=HIGHEST, preferred_element_type=F32)


def _proj_kernel(x_ref, w_ref, ws_ref, o_ref, os_ref):
    xb = x_ref[...].astype(BF16)
    o_ref[...] = _dot(xb, w_ref[...])

    @pl.when(pl.program_id(1) == 0)
    def _():
        os_ref[...] = _dot(xb, ws_ref[...])


def _in_proj(x, w, n, w_side, *, tm, tn):
    m, k = x.shape
    ns = w_side.shape[1]
    return pl.pallas_call(
        _proj_kernel,
        grid=(m // tm, n // tn),
        in_specs=[pl.BlockSpec((tm, k), lambda i, j: (i, 0)),
                  pl.BlockSpec((k, tn), lambda i, j: (0, j)),
                  pl.BlockSpec((k, ns), lambda i, j: (0, 0))],
        out_specs=[pl.BlockSpec((tm, tn), lambda i, j: (i, j)),
                   pl.BlockSpec((tm, ns), lambda i, j: (i, 0))],
        out_shape=[jax.ShapeDtypeStruct((m, n), F32), jax.ShapeDtypeStruct((m, ns), F32)],
        compiler_params=_params("parallel", "arbitrary"),
        name="in_proj",
    )(x, w, w_side)


def _ffn_in_kernel(x_ref, wg_ref, wu_ref, o_ref, wgb_s, wub_s):
    @pl.when(pl.program_id(1) == 0)
    def _():
        wgb_s[...] = wg_ref[...].astype(BF16)
        wub_s[...] = wu_ref[...].astype(BF16)

    x = x_ref[...]
    gate = _dot(x, wgb_s[...])
    up = _dot(x, wub_s[...])
    o_ref[...] = (_silu(gate) * up).astype(o_ref.dtype)


def _ffn_in(xb, w_in, layer, *, tm, tn):
    m, k = xb.shape
    hidden = w_in.shape[2] // 2
    nj = hidden // tn
    return pl.pallas_call(
        _ffn_in_kernel,
        grid=(nj, m // tm),
        in_specs=[pl.BlockSpec((tm, k), lambda j, i: (i, 0)),
                  pl.BlockSpec((None, k, tn), lambda j, i: (layer, 0, j)),
                  pl.BlockSpec((None, k, tn), lambda j, i: (layer, 0, j + nj))],
        out_specs=pl.BlockSpec((tm, tn), lambda j, i: (i, j)),
        out_shape=jax.ShapeDtypeStruct((m, hidden), BF16),
        scratch_shapes=[pltpu.VMEM((k, tn), BF16), pltpu.VMEM((k, tn), BF16)],
        compiler_params=_params("parallel", "arbitrary", vmem=VMEM_LIMIT_BIG),
        name="ffn_in",
    )(xb, w_in, w_in)


MM_LN_ROWS = 128


def _mm_ln_kernel(*refs, n_pairs, nj):
    xs = refs[:n_pairs]
    ws = refs[n_pairs:2 * n_pairs]
    r_ref, g_ref, b_ref, o_ref, ob_ref, y_s = refs[2 * n_pairs:2 * n_pairs + 6]
    tm = o_ref.shape[0]
    j = pl.program_id(1)
    part = _dot(xs[0][...], ws[0][...])
    for x_ref, w_ref in zip(xs[1:], ws[1:]):
        part = part + _dot(x_ref[...], w_ref[...])
    y_s[j] = part

    @pl.when(j == nj - 1)
    def _():
        for r0 in range(0, tm, MM_LN_ROWS):
            rows = slice(r0, r0 + MM_LN_ROWS)
            y = jnp.concatenate([y_s[jj, rows, :] for jj in range(nj)], axis=1)
            v = ALPHA * r_ref[rows, :] + y
            mu = jnp.mean(v, axis=-1, keepdims=True)
            dv = v - mu
            var = jnp.mean(dv * dv, axis=-1, keepdims=True)
            out = dv * lax.rsqrt(var + LN_EPS) * g_ref[...] + b_ref[...]
            o_ref[rows, :] = out
            ob_ref[rows, :] = out.astype(BF16)


def _matmul_ln(xs, ws, resid, g, b, *, tm, tn, layer=None):
    m, d = resid.shape
    n_pairs = len(xs)
    nj = d // tn
    ks = [x.shape[1] for x in xs]
    if layer is None:
        w_specs = [pl.BlockSpec((k, tn), lambda i, j: (0, j)) for k in ks]
    else:
        w_specs = [pl.BlockSpec((None, k, tn), lambda i, j: (layer, 0, j)) for k in ks]
    in_specs = ([pl.BlockSpec((tm, k), lambda i, j: (i, 0)) for k in ks]
                + w_specs
                + [pl.BlockSpec((tm, d), lambda i, j: (i, 0)),
                   pl.BlockSpec((1, d), lambda i, j: (0, 0)),
                   pl.BlockSpec((1, d), lambda i, j: (0, 0))])
    return pl.pallas_call(
        functools.partial(_mm_ln_kernel, n_pairs=n_pairs, nj=nj),
        grid=(m // tm, nj),
        in_specs=in_specs,
        out_specs=[pl.BlockSpec((tm, d), lambda i, j: (i, 0)),
                   pl.BlockSpec((tm, d), lambda i, j: (i, 0))],
        out_shape=[jax.ShapeDtypeStruct((m, d), F32), jax.ShapeDtypeStruct((m, d), BF16)],
        scratch_shapes=[pltpu.VMEM((nj, tm, tn), F32)],
        compiler_params=_params("parallel", "arbitrary", vmem=VMEM_LIMIT_BIG),
        name="matmul_ln",
    )(*xs, *ws, resid, g.reshape(1, d), b.reshape(1, d))


def _causal_conv(u, w_ref, taps):
    def tap_sum(x, shift):
        acc = x * w_ref[taps - 1:taps, :]
        for sh in range(1, taps):
            acc = acc + shift(x, sh) * w_ref[taps - 1 - sh:taps - sh, :]
        return acc

    body = tap_sum(u, lambda x, sh: pltpu.roll(x, sh, axis=0))
    row = lax.broadcasted_iota(jnp.int32, (8, u.shape[1]), 0)
    head = tap_sum(u[0:8], lambda x, sh: jnp.where(row >= sh, pltpu.roll(x, sh, axis=0), 0.0))
    return jnp.concatenate([head, body[8:]], axis=0)


def _sc_kernel(b_ref, c_ref, h_ref, w_ref, o_ref):
    u = c_ref[0] * h_ref[0]
    o_ref[0] = (b_ref[0] * _causal_conv(u, w_ref, SC_KERNEL)).astype(o_ref.dtype)


def _short_conv(proj, conv_w, width, *, tc):
    bsz, s, _ = proj.shape
    nb = width // tc
    w = jnp.zeros((8, width), F32).at[:SC_KERNEL].set(conv_w)
    return pl.pallas_call(
        _sc_kernel,
        grid=(bsz, nb),
        in_specs=[pl.BlockSpec((1, s, tc), lambda b, j: (b, 0, j)),
                  pl.BlockSpec((1, s, tc), lambda b, j: (b, 0, j + nb)),
                  pl.BlockSpec((1, s, tc), lambda b, j: (b, 0, j + 2 * nb)),
                  pl.BlockSpec((8, tc), lambda b, j: (0, j))],
        out_specs=pl.BlockSpec((1, s, tc), lambda b, j: (b, 0, j)),
        out_shape=jax.ShapeDtypeStruct((bsz, s, width), BF16),
        compiler_params=_params("parallel", "parallel"),
        name="short_conv",
    )(proj, proj, proj, w)


def _split(x):
    hi = x.astype(BF16)
    return hi, (x - hi.astype(F32)).astype(BF16)


def _dotb(a, b):
    return _dot(a.astype(BF16), b.astype(BF16))


DN_GROUP = 8
DN_HEADS_PER_STEP = 2


def _dn_group_local(base, scr, masks, out):
    q_s, k_s, kb_s, qd_s, kf_s, kbe_s, vb_s, gc_s, gcd_s = scr
    incl, strict, m8, m16, eye = masks
    c = DN_CHUNK
    idx = range(DN_HEADS_PER_STEP * DN_GROUP)
    rows = [(e // DN_GROUP, pl.ds(base + (e % DN_GROUP) * c, c)) for e in idx]
    gc = [gc_s[hh, r, :] for hh, r in rows]
    decay = []
    for e in idx:
        cc = e % DN_GROUP
        gc_j = gcd_s[e // DN_GROUP, pl.ds(base // LANES + cc // 2, 1), (cc % 2) * c:(cc % 2) * c + c]
        decay.append(jnp.where(incl, jnp.exp(jnp.where(incl, gc[e][:, :c] - gc_j, 0.0)), 0.0))
    kbf = [k_s[hh, r, :] for hh, r in rows]
    kk = [_dot_nt(kb_s[rows[cc][0], rows[cc][1], :], kbf[cc]) for cc in idx]
    qk = [_dot_nt(q_s[rows[cc][0], rows[cc][1], :], kbf[cc]) for cc in idx]
    yield
    a = [jnp.where(strict, kk[cc] * decay[cc], 0.0) for cc in idx]
    intra = [(qk[cc] * decay[cc]).astype(BF16) for cc in idx]
    ad = [jnp.where(m8, x, 0.0) for x in a]
    adb = [x.astype(BF16) for x in ad]
    a2 = [_dot(x, x) for x in adb]
    yield
    a2b = [x.astype(BF16) for x in a2]
    p = [eye - x for x in ad]
    p1, a4 = [], []
    for cc in idx:
        p1.append(p[cc] + _dotb(p[cc], a2b[cc]))
        a4.append(_dot(a2b[cc], a2b[cc]))
    yield
    p2 = [p1[cc] + _dotb(p1[cc], a4[cc]) for cc in idx]
    yield
    pb = [x.astype(BF16) for x in p2]
    t = [_dotb(pb[cc], jnp.where(m16, a[cc] - ad[cc], 0.0)) for cc in idx]
    yield
    dinv = [(p2[cc] - _dotb(t[cc], pb[cc])).astype(BF16) for cc in idx]
    yield
    db, da = [], []
    for cc in idx:
        hh, r = rows[cc]
        rhs = jnp.concatenate([vb_s[hh, r, :], kbe_s[hh, r, :]], axis=1)
        db.append(_dotb(dinv[cc], rhs))
        da.append(_dotb(dinv[cc], jnp.where(m16, 0.0, a[cc])).astype(BF16))
    yield
    blocks = [[x[0:16]] for x in db]
    for s4 in range(1, c // 16):
        rs = slice(16 * s4, 16 * s4 + 16)
        for cc in idx:
            xprev = jnp.concatenate(blocks[cc] + [jnp.zeros((c - 16 * s4, 2 * HEAD_DIM), F32)], axis=0)
            blocks[cc].append(db[cc][rs] - _dotb(da[cc][rs], xprev))
        yield
    for cc in idx:
        sol = jnp.concatenate(blocks[cc], axis=0)
        sol_hi, sol_lo = _split(sol)
        hh, r = rows[cc]
        g_last = gc[cc][c - 1:c, :]
        k_dec_t = (kf_s[hh, r, :] * jnp.exp(g_last - gc[cc])).T.astype(BF16)
        kw = _dot(k_dec_t, sol_hi) + _dot(k_dec_t, sol_lo)
        iw = _dot(intra[cc], sol_hi) + _dot(intra[cc], sol_lo)
        out.append((kw[:, HEAD_DIM:].astype(BF16), kw[:, :HEAD_DIM],
                    (qd_s[hh, r, :] - iw[:, HEAD_DIM:]).astype(BF16), iw[:, :HEAD_DIM], jnp.exp(g_last)))
    yield


def _dn_chunk_seq(state, loc, z, nw):
    w2, n_mat, qp, op, eg_last = loc
    sb = state.astype(BF16)
    o = _dot(qp, sb) + op
    state = (state * eg_last - _dot(w2, sb)) + n_mat
    o = o * lax.rsqrt(jnp.mean(o * o, axis=-1, keepdims=True) + NORM_EPS) * nw * _silu(z)
    return state, o


def _dn_kernel(qp_ref, kp_ref, vp_ref, z_ref, ba_ref, arow_ref, dtrow_ref, cwq_ref, cwk_ref, cwv_ref, nw_ref, o_ref,
               q_s, k_s, kb_s, qd_s, kf_s, kbe_s, vb_s, gc_s, beta_s, gates_s, gcd_s, *, n_heads):
    hb = DN_HEADS_PER_STEP
    s = qp_ref.shape[1]
    c = DN_CHUNK
    @pl.when(pl.program_id(1) == 0)
    def _():
        ba = ba_ref[0]
        xa = ba + dtrow_ref[...]
        softplus = jnp.maximum(xa, 0.0) + jnp.log(1.0 + jnp.exp(-jnp.abs(xa)))
        lane0 = lax.broadcasted_iota(jnp.int32, ba.shape, 1)
        gates = jnp.where(lane0 < n_heads, _sigmoid(ba), -jnp.exp(arow_ref[...]) * softplus)
        for blk in range(s // LANES):
            rs = slice(blk * LANES, (blk + 1) * LANES)
            gates_s[rs, :] = gates[rs, :].T

    n_blk = s // LANES
    pos = lax.broadcasted_iota(jnp.int32, (n_blk, LANES), 1) & (c - 1)
    for hh in range(hb):
        h = pl.program_id(1) * hb + hh
        cols = slice(hh * HEAD_DIM, (hh + 1) * HEAD_DIM)
        beta_d = jnp.concatenate([gates_s[pl.ds(blk * LANES + h, 1), :] for blk in range(n_blk)], axis=0)
        gc_d = jnp.concatenate([gates_s[pl.ds(blk * LANES + h + n_heads, 1), :] for blk in range(n_blk)], axis=0)
        sh = 1
        while sh < c:
            gc_d = gc_d + jnp.where(pos >= sh, pltpu.roll(gc_d, sh, axis=1), 0.0)
            sh *= 2
        gcd_s[hh, 0:n_blk, :] = gc_d
        for blk in range(n_blk):
            rs = slice(blk * LANES, (blk + 1) * LANES)
            gc_s[hh, rs, :] = jnp.broadcast_to(gc_d[blk:blk + 1, :], (LANES, LANES)).T
            beta_s[hh, rs, :] = jnp.broadcast_to(beta_d[blk:blk + 1, :], (LANES, LANES)).T
        gc = gc_s[hh]
        beta = beta_s[hh]
        eg = jnp.exp(gc)
        q = _silu(_causal_conv(qp_ref[0, :, cols], cwq_ref[:, cols], DN_CONV))
        q = q * (lax.rsqrt(jnp.sum(q * q, axis=-1, keepdims=True) + NORM_EPS) * (HEAD_DIM ** -0.5))
        q_s[hh] = q.astype(BF16)
        qd_s[hh] = q * eg
        k = _silu(_causal_conv(kp_ref[0, :, cols], cwk_ref[:, cols], DN_CONV))
        k = k * lax.rsqrt(jnp.sum(k * k, axis=-1, keepdims=True) + NORM_EPS)
        kb = k * beta
        kf_s[hh] = k
        k_s[hh] = k.astype(BF16)
        kb_s[hh] = kb.astype(BF16)
        kbe_s[hh] = kb * eg
        vb_s[hh] = _silu(_causal_conv(vp_ref[0, :, cols], cwv_ref[:, cols], DN_CONV)) * beta

    row = lax.broadcasted_iota(jnp.int32, (c, c), 0)
    col = lax.broadcasted_iota(jnp.int32, (c, c), 1)
    masks = (row >= col, row > col, (row >> 3) == (col >> 3), (row >> 4) == (col >> 4), (row == col).astype(F32))
    scr = (q_s, k_s, kb_s, qd_s, kf_s, kbe_s, vb_s, gc_s, gcd_s)
    nw = nw_ref[...]
    rows_per_group = DN_GROUP * c
    n_groups = s // rows_per_group

    def group_base(gi):
        base = gi * rows_per_group
        return base if isinstance(base, int) else pl.multiple_of(base, rows_per_group)

    def run(gi_local, gi_seq, states, locs):
        nxt = []
        stages = iter(()) if gi_local is None else _dn_group_local(group_base(gi_local), scr, masks, nxt)
        todo = list(range(DN_GROUP)) if gi_seq is not None else []
        states = list(states)
        done = False
        while todo or not done:
            if not done:
                done = next(stages, "end") == "end"
            if todo:
                cc = todo.pop(0)
                rows = pl.ds(group_base(gi_seq) + cc * c, c)
                for hh in range(hb):
                    cols = slice(hh * HEAD_DIM, (hh + 1) * HEAD_DIM)
                    states[hh], o = _dn_chunk_seq(states[hh], locs[hh * DN_GROUP + cc], z_ref[0, rows, cols], nw)
                    o_ref[0, rows, cols] = o.astype(o_ref.dtype)
        return tuple(states), tuple(nxt)

    def body(gi, carry):
        return run(gi + 1, gi, *carry)

    carry = run(0, None, (jnp.zeros((HEAD_DIM, HEAD_DIM), F32),) * hb, None)
    carry = lax.fori_loop(0, n_groups - 1, body, carry)
    run(None, n_groups - 1, *carry)


def _deltanet(proj, qkv_col0, z_col0, conv_w, ba, a_log, dt_bias, norm_w, n_heads):
    bsz, s, _ = proj.shape
    qb0 = qkv_col0 // HEAD_DIM
    zb0 = z_col0 // HEAD_DIM
    arow = jnp.zeros((1, LANES), F32).at[0, n_heads:2 * n_heads].set(a_log)
    dtrow = jnp.zeros((1, LANES), F32).at[0, n_heads:2 * n_heads].set(dt_bias)
    cw = jnp.zeros((8, 3 * n_heads * HEAD_DIM), F32).at[:DN_CONV].set(conv_w)
    hb = DN_HEADS_PER_STEP
    assert n_heads % hb == 0 and qb0 % hb == 0 and zb0 % hb == 0
    blk = (1, s, hb * HEAD_DIM)
    col_spec = lambda off: pl.BlockSpec(blk, lambda b, h, off=off: (b, 0, h + off // hb))
    cw_spec = lambda off: pl.BlockSpec((8, hb * HEAD_DIM), lambda b, h, off=off: (0, h + off // hb))
    row_spec = pl.BlockSpec((1, LANES), lambda b, h: (0, 0))
    per_head = lambda dtype: pltpu.VMEM((hb, s, HEAD_DIM), dtype)
    return pl.pallas_call(
        functools.partial(_dn_kernel, n_heads=n_heads),
        grid=(bsz, n_heads // hb),
        in_specs=[col_spec(qb0), col_spec(qb0 + n_heads), col_spec(qb0 + 2 * n_heads), col_spec(zb0),
                  pl.BlockSpec((1, s, LANES), lambda b, h: (b, 0, 0)), row_spec, row_spec,
                  cw_spec(0), cw_spec(n_heads), cw_spec(2 * n_heads), row_spec],
        out_specs=pl.BlockSpec(blk, lambda b, h: (b, 0, h)),
        out_shape=jax.ShapeDtypeStruct((bsz, s, n_heads * HEAD_DIM), BF16),
        scratch_shapes=([per_head(BF16)] * 3 + [per_head(F32)] * 6 + [pltpu.VMEM((s, LANES), F32)]
                        + [pltpu.VMEM((hb, max(8, s // LANES), LANES), F32)]),
        compiler_params=_params("parallel", "arbitrary", vmem=VMEM_LIMIT_BIG),
        name="deltanet",
    )(proj, proj, proj, proj, ba, arow, dtrow, cw, cw, cw, norm_w.reshape(1, HEAD_DIM))


def _rope_tables(ang):
    lane = lax.broadcasted_iota(jnp.int32, ang.shape, 1)
    sin = jnp.sin(ang)
    return jnp.cos(ang), jnp.where(lane < HEAD_DIM // 2, -sin, sin)


def _rope(x, cos, sin_signed):
    return x * cos + pltpu.roll(x, HEAD_DIM // 2, axis=1) * sin_signed


def _rope_kernel(ang_ref, q_ref, ks_ref, kw_ref, vs_ref, vw_ref, qo_ref, kso_ref, kwo_ref, vso_ref, vwo_ref):
    cos, sin = _rope_tables(ang_ref[0])
    for hh in range(NSA_HEADS):
        sl = slice(hh * HEAD_DIM, (hh + 1) * HEAD_DIM)
        qo_ref[0, :, sl] = (_rope(q_ref[0, :, sl], cos, sin) * Q_SCALE).astype(BF16)
    for g in range(NSA_KV_HEADS):
        sl = slice(g * HEAD_DIM, (g + 1) * HEAD_DIM)
        kso_ref[0, :, sl] = _rope(ks_ref[0, :, sl], cos, sin).astype(BF16)
        kwo_ref[0, :, sl] = _rope(kw_ref[0, :, sl], cos, sin).astype(BF16)
    ts = vs_ref.shape[1]
    for v_ref, vo_ref in ((vs_ref, vso_ref), (vw_ref, vwo_ref)):
        ck = vo_ref.shape[4]
        for g in range(NSA_KV_HEADS):
            for cc in range(ts // ck):
                parts = [v_ref[0, cc * ck + r:cc * ck + r + LANES, g * HEAD_DIM:(g + 1) * HEAD_DIM].T
                         for r in range(0, ck, LANES)]
                vo_ref[0, g, cc] = jnp.concatenate(parts, axis=1).astype(BF16)


def _rope_qkv(proj, ang, *, ts, slc_chunk, win_chunk):
    bsz, s, _ = proj.shape
    qw = NSA_HEADS * HEAD_DIM
    kvw = NSA_KV_HEADS * HEAD_DIM
    kv_spec = lambda blk: pl.BlockSpec((1, ts, kvw), lambda b, i, blk=blk: (b, i, blk))
    kv_out = pl.BlockSpec((1, ts, kvw), lambda b, i: (b, i, 0))
    kv_shape = jax.ShapeDtypeStruct((bsz, s, kvw), BF16)
    vt_out = lambda ck: pl.BlockSpec((1, NSA_KV_HEADS, ts // ck, HEAD_DIM, ck), lambda b, i: (b, 0, i, 0, 0))
    vt_shape = lambda ck: jax.ShapeDtypeStruct((bsz, NSA_KV_HEADS, s // ck, HEAD_DIM, ck), BF16)
    base = qw // kvw
    return pl.pallas_call(
        _rope_kernel,
        grid=(bsz, s // ts),
        in_specs=[pl.BlockSpec((1, ts, HEAD_DIM), lambda b, i: (b, i, 0)),
                  pl.BlockSpec((1, ts, qw), lambda b, i: (b, i, 0)),
                  kv_spec(base + 2), kv_spec(base + 4), kv_spec(base + 3), kv_spec(base + 5)],
        out_specs=[pl.BlockSpec((1, ts, qw), lambda b, i: (b, i, 0)), kv_out, kv_out,
                   vt_out(slc_chunk), vt_out(win_chunk)],
        out_shape=[jax.ShapeDtypeStruct((bsz, s, qw), BF16), kv_shape, kv_shape,
                   vt_shape(slc_chunk), vt_shape(win_chunk)],
        compiler_params=_params("parallel", "parallel"),
        name="rope_qkv",
    )(ang, proj, proj, proj, proj, proj)


def _gelu_tanh(x):
    return x * (0.5 * (1.0 + jnp.tanh(math.sqrt(2.0 / math.pi) * (x + 0.044715 * (x * x * x)))))


def _compress_kernel(x_ref, w1_ref, w2_ref, pos_ref, ang_ref, o_ref, *, rope):
    nsub = x_ref.shape[1] // CMP_STRIDE
    hid = w1_ref.shape[1]
    pa = jnp.zeros((nsub, hid), F32)
    pb = jnp.zeros((nsub, hid), F32)
    for l in range(CMP_STRIDE):
        xl = x_ref[0, pl.ds(l, nsub, stride=CMP_STRIDE), :].astype(BF16)
        pa = pa + _dot(xl, w1_ref[l * HEAD_DIM:(l + 1) * HEAD_DIM, :])
        pb = pb + _dot(xl, w1_ref[(CMP_STRIDE + l) * HEAD_DIM:(CMP_STRIDE + l + 1) * HEAD_DIM, :])
    bias = _dot(pos_ref[...], w1_ref[...])[0:1, :]
    hpre = pa + pltpu.roll(pb, nsub - 1, axis=0) + bias
    out = _dot(_gelu_tanh(hpre).astype(BF16), w2_ref[...])
    if rope:
        cos, sin = _rope_tables(ang_ref[0])
        out = _rope(out, cos, sin)
    o_ref[0, 0] = out.astype(o_ref.dtype)


def _compress(proj, col0, pos_emb, w1, w2, ang_cmp, *, rope):
    bsz, s, _ = proj.shape
    nsub = s // CMP_STRIDE
    blk0 = col0 // HEAD_DIM
    hid = w1.shape[1]
    pos = jnp.zeros((8, CMP_BLOCK * HEAD_DIM), BF16).at[0].set(pos_emb.reshape(-1).astype(BF16))
    return pl.pallas_call(
        functools.partial(_compress_kernel, rope=rope),
        grid=(bsz, NSA_KV_HEADS),
        in_specs=[pl.BlockSpec((1, s, HEAD_DIM), lambda b, g: (b, 0, g + blk0)),
                  pl.BlockSpec((CMP_BLOCK * HEAD_DIM, hid), lambda b, g: (0, 0)),
                  pl.BlockSpec((hid, HEAD_DIM), lambda b, g: (0, 0)),
                  pl.BlockSpec((8, CMP_BLOCK * HEAD_DIM), lambda b, g: (0, 0)),
                  pl.BlockSpec((1, nsub, HEAD_DIM), lambda b, g: (b, 0, 0))],
        out_specs=pl.BlockSpec((1, 1, nsub, HEAD_DIM), lambda b, g: (b, g, 0, 0)),
        out_shape=jax.ShapeDtypeStruct((bsz, NSA_KV_HEADS, nsub, HEAD_DIM), BF16),
        compiler_params=_params("parallel", "parallel"),
        name="compress",
    )(proj, w1.astype(BF16), w2.astype(BF16), pos, ang_cmp)


def _cmp_attn_kernel(q_ref, kc_ref, vc_ref, smat_ref, o_ref, sel_ref):
    tq = q_ref.shape[1]
    ncol = kc_ref.shape[2]
    t = pl.program_id(2) * tq + lax.broadcasted_iota(jnp.int32, (tq, ncol), 0)
    n = lax.broadcasted_iota(jnp.int32, (tq, ncol), 1)
    valid = (n * CMP_STRIDE + CMP_BLOCK - 1) <= t
    kc = kc_ref[0, 0]
    vc = vc_ref[0, 0]
    p_grp = jnp.zeros((tq, ncol), F32)
    for hh in range(HPG):
        sl = slice(hh * HEAD_DIM, (hh + 1) * HEAD_DIM)
        sc = jnp.where(valid, _dot_nt(q_ref[0, :, sl], kc), NEG_INF)
        e = jnp.exp2(sc - jnp.max(sc, axis=-1, keepdims=True))
        p = jnp.where(valid, e / jnp.sum(e, axis=-1, keepdims=True), 0.0)
        o_ref[0, :, sl] = _dot(p.astype(BF16), vc)
        p_grp = p_grp + p
    score = _dot_hi(p_grp, smat_ref[...])
    n_sel = sel_ref.shape[2]
    score = jnp.concatenate([score[r:r + LANES].T for r in range(0, tq, LANES)], axis=1)[:n_sel]
    n = lax.broadcasted_iota(jnp.int32, (n_sel, tq), 0)
    t = pl.program_id(2) * tq + lax.broadcasted_iota(jnp.int32, (n_sel, tq), 1)
    cur = t >> int(math.log2(SEL_BLOCK))
    forced = (n == 0) | (n == cur) | (n == cur - 1)
    future = n * SEL_BLOCK > t
    score = jnp.where(forced, jnp.inf, jnp.where(future, -jnp.inf, score))
    rank = jnp.zeros((n_sel, tq), jnp.int32)
    for kk in range(n_sel):
        ck = score[kk:kk + 1, :]
        ahead = (ck > score) | ((ck == score) & (kk < n))
        rank = rank + ahead.astype(jnp.int32)
    sel_ref[0, 0] = (rank < N_SELECT).astype(sel_ref.dtype)


def _sel_matrix(ncol, n_sel):
    rs = SEL_BLOCK // CMP_STRIDE
    rc = CMP_BLOCK // CMP_STRIDE
    mat = [[0.0] * ncol for _ in range(ncol)]
    for j in range(n_sel):
        for m in range(rs):
            for n in range(rc):
                i = rs * j + m + n - (rc - 1)
                if 0 <= i < ncol - 1:
                    mat[i][j] += 1.0
    return jnp.array(mat, F32)


def _cmp_attention(q_r, k_cmp, v_cmp, *, tq):
    bsz, s, _ = q_r.shape
    ncol = k_cmp.shape[2]
    n_sel = s // SEL_BLOCK
    gw = HPG * HEAD_DIM
    return pl.pallas_call(
        _cmp_attn_kernel,
        grid=(bsz, NSA_KV_HEADS, s // tq),
        in_specs=[pl.BlockSpec((1, tq, gw), lambda b, g, i: (b, i, g)),
                  pl.BlockSpec((1, 1, ncol, HEAD_DIM), lambda b, g, i: (b, g, 0, 0)),
                  pl.BlockSpec((1, 1, ncol, HEAD_DIM), lambda b, g, i: (b, g, 0, 0)),
                  pl.BlockSpec((ncol, ncol), lambda b, g, i: (0, 0))],
        out_specs=[pl.BlockSpec((1, tq, gw), lambda b, g, i: (b, i, g)),
                   pl.BlockSpec((1, 1, n_sel, tq), lambda b, g, i: (b, g, 0, i))],
        out_shape=[jax.ShapeDtypeStruct((bsz, s, NSA_HEADS * HEAD_DIM), F32),
                   jax.ShapeDtypeStruct((bsz, NSA_KV_HEADS, n_sel, s), F32)],
        compiler_params=_params("parallel", "parallel", "parallel"),
        name="cmp_attention",
    )(q_r, k_cmp, v_cmp, _sel_matrix(ncol, n_sel))


def _softmax_merge_t(parts):
    if len(parts) == 1:
        return parts[0][0] / parts[0][2]
    m = functools.reduce(jnp.maximum, [p[1] for p in parts])
    scale = [jnp.exp2(p[1] - m) for p in parts]
    acc = functools.reduce(lambda x, y: x + y, [p[0] * s for p, s in zip(parts, scale)])
    l = functools.reduce(lambda x, y: x + y, [p[2] * s for p, s in zip(parts, scale)])
    return acc / l


def _softmax_partial_t(scores, values_t):
    r = scores[0].shape[1]
    m8 = None
    for sc in scores:
        c8 = jnp.max(sc.reshape(-1, 8, r), axis=0)
        m8 = c8 if m8 is None else jnp.maximum(m8, c8)
    m = jnp.max(m8, axis=0, keepdims=True)
    l8 = None
    probs = []
    for sc in scores:
        p = jnp.exp2(sc - m)
        p8 = jnp.sum(p.reshape(-1, 8, r), axis=0)
        l8 = p8 if l8 is None else l8 + p8
        probs.append(p.astype(BF16))
    l = jnp.sum(l8, axis=0, keepdims=True)
    acc = _dot(jnp.concatenate(values_t, axis=1), jnp.concatenate(probs, axis=0))
    return acc, m, l


MASK_BIG = 2.0 ** 100


def _slc_win_kernel(q_ref, ks_ref, vst_ref, kw_ref, vwt_ref, selt_ref, blk_ref, oc_ref, gate_ref, o_ref,
                    gt_s, os_s, ow_s):
    tq = q_ref.shape[1]
    tk = vst_ref.shape[4]
    n_chunks = vst_ref.shape[2]
    n_sel = selt_ref.shape[2]
    g = pl.program_id(1)
    t0 = pl.program_id(2) * tq
    q_t = jnp.concatenate([q_ref[0, :, hh * HEAD_DIM:(hh + 1) * HEAD_DIM].astype(F32).T.astype(BF16)
                           for hh in range(HPG)], axis=1)

    def tile_heads(mask):
        return jnp.concatenate([mask] * HPG, axis=1)

    key_loc = lax.broadcasted_iota(jnp.int32, (tq, tq), 0)
    qry_loc = lax.broadcasted_iota(jnp.int32, (tq, tq), 1)
    n_win = WINDOW // tq + 1

    def window_scores(in_range):
        if in_range:
            k0 = pl.multiple_of(t0 - WINDOW, tq)
            sc = _dot(kw_ref[0, pl.ds(k0, n_win * tq), :], q_t)
            scores = [sc[cc * tq:(cc + 1) * tq] for cc in range(n_win)]
            scores[0] = jnp.where(tile_heads(key_loc > qry_loc), scores[0], NEG_INF)
            scores[-1] = jnp.where(tile_heads(key_loc <= qry_loc), scores[-1], NEG_INF)
            return scores, [vwt_ref[0, 0, k0 // tq + cc] for cc in range(n_win)]
        sc = _dot(kw_ref[0, 0:WINDOW, :], q_t)
        scores = [jnp.where(tile_heads(cc * tq + key_loc <= t0 + qry_loc), sc[cc * tq:(cc + 1) * tq], NEG_INF)
                  for cc in range(WINDOW // tq)]
        return scores, [vwt_ref[0, 0, cc] for cc in range(WINDOW // tq)]

    sel_bias = ((selt_ref[0, 0] - 1.0) * MASK_BIG).astype(BF16)
    q_aug = jnp.concatenate([q_t, tile_heads(sel_bias),
                             jnp.zeros((HEAD_DIM - n_sel, HPG * tq), BF16)], axis=0)
    k_loc = lax.broadcasted_iota(jnp.int32, (tk, tq), 0)
    t_loc = lax.broadcasted_iota(jnp.int32, (tk, tq), 1)

    def tile_variant(n_used):
        n_groups = min(n_used, 2)
        bounds = [(n_used * gg) // n_groups for gg in range(n_groups + 1)]
        groups = []
        for lo, hi in zip(bounds[:-1], bounds[1:]):
            rows = slice(lo * tk, hi * tk)
            sc = _dot(jnp.concatenate([ks_ref[0, rows, :], blk_ref[rows, :]], axis=1), q_aug)
            groups.append([sc[cc * tk:(cc + 1) * tk] for cc in range(hi - lo)])
        causal = ((n_used - 1) * tk + k_loc) <= (t0 + t_loc)
        groups[-1][-1] = jnp.where(tile_heads(causal), groups[-1][-1], NEG_INF)
        win_scores, win_values = window_scores((n_used - 1) * tk >= WINDOW)
        parts = [_softmax_partial_t(groups[0], [vst_ref[0, 0, cc] for cc in range(bounds[0], bounds[1])])]
        ow_s[...] = _softmax_merge_t([_softmax_partial_t(win_scores, win_values)])
        for gg in range(1, n_groups):
            parts.append(_softmax_partial_t(groups[gg],
                                            [vst_ref[0, 0, cc] for cc in range(bounds[gg], bounds[gg + 1])]))
        os_s[...] = _softmax_merge_t(parts)

    for vv in range(n_chunks):
        pl.when(t0 // tk == vv)(functools.partial(tile_variant, vv + 1))

    gt = _sigmoid(gate_ref[0])
    gt_s[...] = gt.T
    lane = lax.broadcasted_iota(jnp.int32, gt.shape, 1)
    for hh in range(HPG):
        base = (g * HPG + hh) * 3
        g_cmp = jnp.sum(jnp.where(lane == base, gt, 0.0), axis=1, keepdims=True)
        cs = slice(hh * tq, (hh + 1) * tq)
        mix_t = gt_s[pl.ds(base + 1, 1), :] * os_s[:, cs] + gt_s[pl.ds(base + 2, 1), :] * ow_s[:, cs]
        sl = slice(hh * HEAD_DIM, (hh + 1) * HEAD_DIM)
        o_ref[0, :, sl] = (g_cmp * oc_ref[0, :, sl] + mix_t.T).astype(o_ref.dtype)


def _slc_win_attention(q_r, ks, vs_t, kw, vw_t, sel_t, o_cmp, gates, *, tq):
    bsz, s, _ = q_r.shape
    n_sel = s // SEL_BLOCK
    gw = HPG * HEAD_DIM
    assert vw_t.shape[4] == tq and tq == LANES and vs_t.shape[4] % tq == 0
    kv_spec = pl.BlockSpec((1, s, HEAD_DIM), lambda b, g, i: (b, 0, g))
    vt_spec = lambda a: pl.BlockSpec((1, 1) + a.shape[2:], lambda b, g, i: (b, g, 0, 0, 0))
    block_onehot = (jnp.arange(s)[:, None] // SEL_BLOCK == jnp.arange(HEAD_DIM)[None, :]).astype(BF16)
    return pl.pallas_call(
        _slc_win_kernel,
        grid=(bsz, NSA_KV_HEADS, s // tq),
        in_specs=[pl.BlockSpec((1, tq, gw), lambda b, g, i: (b, i, g)),
                  kv_spec, vt_spec(vs_t), kv_spec, vt_spec(vw_t),
                  pl.BlockSpec((1, 1, n_sel, tq), lambda b, g, i: (b, g, 0, i)),
                  pl.BlockSpec((s, HEAD_DIM), lambda b, g, i: (0, 0)),
                  pl.BlockSpec((1, tq, gw), lambda b, g, i: (b, i, g)),
                  pl.BlockSpec((1, tq, LANES), lambda b, g, i: (b, i, 0))],
        out_specs=pl.BlockSpec((1, tq, gw), lambda b, g, i: (b, i, g)),
        out_shape=jax.ShapeDtypeStruct((bsz, s, NSA_HEADS * HEAD_DIM), BF16),
        scratch_shapes=[pltpu.VMEM((LANES, tq), F32), pltpu.VMEM((HEAD_DIM, HPG * tq), F32),
                        pltpu.VMEM((HEAD_DIM, HPG * tq), F32)],
        compiler_params=_params("parallel", "parallel", "arbitrary"),
        name="slc_win_attention",
    )(q_r, ks, vs_t, kw, vw_t, sel_t, block_onehot, o_cmp, gates)


def _pad_cols(w, n):
    return jnp.pad(w, ((0, 0), (0, n - w.shape[1])))


def _conv_deltanet_mixer(xb, bsz, s, w_in, sc_conv_w, dn_conv_w, a_log, dt_bias, norm_w, w_out):
    sc_w = sc_conv_w.shape[1]
    dn_w = dn_conv_w.shape[1] // 3
    n_heads = dn_w // HEAD_DIM
    main = 3 * sc_w + 4 * dn_w
    proj, ba = _in_proj(xb, w_in.astype(BF16), main, _pad_cols(w_in[:, main:], LANES).astype(BF16),
                        tm=1024, tn=1024)
    proj = proj.reshape(bsz, s, main)
    y_sc = _short_conv(proj, sc_conv_w, sc_w, tc=256)
    y_dn = _deltanet(proj, 3 * sc_w, 3 * sc_w + 3 * dn_w, dn_conv_w, ba.reshape(bsz, s, LANES), a_log, dt_bias,
                     norm_w, n_heads)
    wo = w_out.astype(BF16)
    return [y_sc.reshape(bsz * s, sc_w), y_dn.reshape(bsz * s, dn_w)], [wo[:sc_w], wo[sc_w:]]


def _nsa_mixer(xb, bsz, s, positions, w_in, cmp_pos_k, cmp_w1_k, cmp_w2_k, cmp_pos_v, cmp_w1_v, cmp_w2_v, w_out):
    qw = NSA_HEADS * HEAD_DIM
    kvw = NSA_KV_HEADS * HEAD_DIM
    main = qw + 6 * kvw
    proj, gates = _in_proj(xb, w_in.astype(BF16), main, _pad_cols(w_in[:, main:], LANES).astype(BF16),
                           tm=1024, tn=1024)
    proj = proj.reshape(bsz, s, main)
    half = HEAD_DIM // 2
    inv = jnp.power(ROPE_THETA, -jnp.arange(half, dtype=F32) / half)
    inv = jnp.concatenate([inv, inv])
    ang = positions.astype(F32)[..., None] * inv
    cmp_end = jnp.minimum(jnp.arange(s // CMP_STRIDE) * CMP_STRIDE + CMP_BLOCK - 1, s - 1)
    ang_cmp = positions[:, cmp_end].astype(F32)[..., None] * inv
    q_r, ks, kw, vs_t, vw_t = _rope_qkv(proj, ang, ts=512, slc_chunk=256, win_chunk=LANES)
    k_cmp = _compress(proj, qw, cmp_pos_k, cmp_w1_k, cmp_w2_k, ang_cmp, rope=True)
    v_cmp = _compress(proj, qw + kvw, cmp_pos_v, cmp_w1_v, cmp_w2_v, ang_cmp, rope=False)
    o_cmp, sel_t = _cmp_attention(q_r, k_cmp, v_cmp, tq=512)
    o = _slc_win_attention(q_r, ks, vs_t, kw, vw_t, sel_t, o_cmp, gates.reshape(bsz, s, LANES), tq=LANES)
    return [o.reshape(bsz * s, qw)], [w_out.astype(BF16)]


def kernel(x, positions, ln_mix_g, ln_mix_b, ln_ffn_g, ln_ffn_b, ffn_w_in, ffn_w_out, hy_w_in, sc_conv_w, dn_conv_w, dn_a_log, dn_dt_bias, dn_norm_w, hy_w_out, nsa_w_in, cmp_pos_k, cmp_w1_k, cmp_w2_k, cmp_pos_v, cmp_w1_v, cmp_w2_v, nsa_w_out):
    bsz, s, d = x.shape
    xf = x.reshape(bsz * s, d)
    xb = xf
    ffn_w_out_b = ffn_w_out.astype(BF16)
    for i in range(DEPTH):
        j = i // 2
        if i % 2 == 0:
            ys, wos = _conv_deltanet_mixer(xb, bsz, s, hy_w_in[j], sc_conv_w[j], dn_conv_w[j], dn_a_log[j],
                                           dn_dt_bias[j], dn_norm_w[j], hy_w_out[j])
        else:
            ys, wos = _nsa_mixer(xb, bsz, s, positions, nsa_w_in[j], cmp_pos_k[j], cmp_w1_k[j], cmp_w2_k[j],
                                 cmp_pos_v[j], cmp_w1_v[j], cmp_w2_v[j], nsa_w_out[j])
        xf, xb = _matmul_ln(ys, wos, xf, ln_mix_g[i], ln_mix_b[i], tm=512, tn=d)
        hmid = _ffn_in(xb, ffn_w_in, i, tm=2048, tn=512)
        xf, xb = _matmul_ln([hmid], [ffn_w_out_b], xf, ln_ffn_g[i], ln_ffn_b[i], tm=512, tn=512, layer=i)
    return xf.reshape(bsz, s, d)
```

```python
import functools
import math

import jax
import jax.numpy as jnp
from jax import lax
from jax.experimental import pallas as pl
from jax.experimental.pallas import tpu as pltpu

F32 = jnp.float32
BF16 = jnp.bfloat16
HIGHEST = lax.Precision.HIGHEST

LANES = 128
VMEM_LIMIT = 48 * 1024 * 1024
VMEM_LIMIT_BIG = 56 * 1024 * 1024

DN_HEADS = 8
DN_CHUNK = 64
DN_CONV = 4
SC_KERNEL = 3
NSA_HEADS = 16
NSA_KV_HEADS = 4
HPG = NSA_HEADS // NSA_KV_HEADS
HEAD_DIM = 128
CMP_BLOCK = 32
CMP_STRIDE = 16
SEL_BLOCK = 64
N_SELECT = 16
WINDOW = 512
ROPE_THETA = 10000.0
LN_EPS = 1e-5
NORM_EPS = 1e-6
NEG_INF = -1e30
DEPTH = 2
ALPHA = (2 * DEPTH) ** 0.25
ATTN_SCALE = HEAD_DIM ** -0.5
Q_SCALE = ATTN_SCALE * math.log2(math.e)


def _params(*sem, vmem=VMEM_LIMIT):
    return pltpu.CompilerParams(dimension_semantics=sem, vmem_limit_bytes=vmem)


def _sigmoid(x):
    return 1.0 / (1.0 + jnp.exp(-x))


def _silu(x):
    return x * _sigmoid(x)


def _dot(a, b):
    return jnp.dot(a, b, preferred_element_type=F32)


def _dot_nt(a, b):
    return lax.dot_general(a, b, (((1,), (1,)), ((), ())), preferred_element_type=F32)


def _dot_tn(a, b):
    return lax.dot_general(a, b, (((0,), (0,)), ((), ())), preferred_element_type=F32)


def _dot_hi(a, b):
    return jnp.dot(a, b, precision=HIGHEST, preferred_element_type=F32)


def _proj_kernel(x_ref, w_ref, ws_ref, o_ref, os_ref):
    xb = x_ref[...].astype(BF16)
    o_ref[...] = _dot(xb, w_ref[...])

    @pl.when(pl.program_id(1) == 0)
    def _():
        os_ref[...] = _dot(xb, ws_ref[...])


def _in_proj(x, w, n, w_side, *, tm, tn):
    m, k = x.shape
    ns = w_side.shape[1]
    return pl.pallas_call(
        _proj_kernel,
        grid=(m // tm, n // tn),
        in_specs=[pl.BlockSpec((tm, k), lambda i, j: (i, 0)),
                  pl.BlockSpec((k, tn), lambda i, j: (0, j)),
                  pl.BlockSpec((k, ns), lambda i, j: (0, 0))],
        out_specs=[pl.BlockSpec((tm, tn), lambda i, j: (i, j)),
                   pl.BlockSpec((tm, ns), lambda i, j: (i, 0))],
        out_shape=[jax.ShapeDtypeStruct((m, n), F32), jax.ShapeDtypeStruct((m, ns), F32)],
        compiler_params=_params("parallel", "arbitrary"),
        name="in_proj",
    )(x, w, w_side)


def _ffn_in_kernel(x_ref, wg_ref, wu_ref, o_ref, wgb_s, wub_s):
    @pl.when(pl.program_id(1) == 0)
    def _():
        wgb_s[...] = wg_ref[...].astype(BF16)
        wub_s[...] = wu_ref[...].astype(BF16)

    x = x_ref[...]
    gate = _dot(x, wgb_s[...])
    up = _dot(x, wub_s[...])
    o_ref[...] = (_silu(gate) * up).astype(o_ref.dtype)


def _ffn_in(xb, w_in, layer, *, tm, tn):
    m, k = xb.shape
    hidden = w_in.shape[2] // 2
    nj = hidden // tn
    return pl.pallas_call(
        _ffn_in_kernel,
        grid=(nj, m // tm),
        in_specs=[pl.BlockSpec((tm, k), lambda j, i: (i, 0)),
                  pl.BlockSpec((None, k, tn), lambda j, i: (layer, 0, j)),
                  pl.BlockSpec((None, k, tn), lambda j, i: (layer, 0, j + nj))],
        out_specs=pl.BlockSpec((tm, tn), lambda j, i: (i, j)),
        out_shape=jax.ShapeDtypeStruct((m, hidden), BF16),
        scratch_shapes=[pltpu.VMEM((k, tn), BF16), pltpu.VMEM((k, tn), BF16)],
        compiler_params=_params("parallel", "arbitrary"),
        name="ffn_in",
    )(xb, w_in, w_in)


MM_LN_ROWS = 128


def _mm_ln_kernel(*refs, n_pairs, nj):
    xs = refs[:n_pairs]
    ws = refs[n_pairs:2 * n_pairs]
    r_ref, g_ref, b_ref, o_ref, ob_ref, y_s = refs[2 * n_pairs:2 * n_pairs + 6]
    tm = o_ref.shape[0]
    j = pl.program_id(1)
    part = _dot(xs[0][...], ws[0][...])
    for x_ref, w_ref in zip(xs[1:], ws[1:]):
        part = part + _dot(x_ref[...], w_ref[...])
    y_s[j] = part

    @pl.when(j == nj - 1)
    def _():
        for r0 in range(0, tm, MM_LN_ROWS):
            rows = slice(r0, r0 + MM_LN_ROWS)
            y = jnp.concatenate([y_s[jj, rows, :] for jj in range(nj)], axis=1)
            v = ALPHA * r_ref[rows, :] + y
            mu = jnp.mean(v, axis=-1, keepdims=True)
            dv = v - mu
            var = jnp.mean(dv * dv, axis=-1, keepdims=True)
            out = dv * lax.rsqrt(var + LN_EPS) * g_ref[...] + b_ref[...]
            o_ref[rows, :] = out
            ob_ref[rows, :] = out.astype(BF16)


def _matmul_ln(xs, ws, resid, g, b, *, tm, tn, layer=None):
    m, d = resid.shape
    n_pairs = len(xs)
    nj = d // tn
    ks = [x.shape[1] for x in xs]
    if layer is None:
        w_specs = [pl.BlockSpec((k, tn), lambda i, j: (0, j)) for k in ks]
    else:
        w_specs = [pl.BlockSpec((None, k, tn), lambda i, j: (layer, 0, j)) for k in ks]
    in_specs = ([pl.BlockSpec((tm, k), lambda i, j: (i, 0)) for k in ks]
                + w_specs
                + [pl.BlockSpec((tm, d), lambda i, j: (i, 0)),
                   pl.BlockSpec((1, d), lambda i, j: (0, 0)),
                   pl.BlockSpec((1, d), lambda i, j: (0, 0))])
    return pl.pallas_call(
        functools.partial(_mm_ln_kernel, n_pairs=n_pairs, nj=nj),
        grid=(m // tm, nj),
        in_specs=in_specs,
        out_specs=[pl.BlockSpec((tm, d), lambda i, j: (i, 0)),
                   pl.BlockSpec((tm, d), lambda i, j: (i, 0))],
        out_shape=[jax.ShapeDtypeStruct((m, d), F32), jax.ShapeDtypeStruct((m, d), BF16)],
        scratch_shapes=[pltpu.VMEM((nj, tm, tn), F32)],
        compiler_params=_params("parallel", "arbitrary", vmem=VMEM_LIMIT_BIG),
        name="matmul_ln",
    )(*xs, *ws, resid, g.reshape(1, d), b.reshape(1, d))


def _causal_conv(u, w_ref, taps):
    def tap_sum(x, shift):
        acc = x * w_ref[taps - 1:taps, :]
        for sh in range(1, taps):
            acc = acc + shift(x, sh) * w_ref[taps - 1 - sh:taps - sh, :]
        return acc

    body = tap_sum(u, lambda x, sh: pltpu.roll(x, sh, axis=0))
    row = lax.broadcasted_iota(jnp.int32, (8, u.shape[1]), 0)
    head = tap_sum(u[0:8], lambda x, sh: jnp.where(row >= sh, pltpu.roll(x, sh, axis=0), 0.0))
    return jnp.concatenate([head, body[8:]], axis=0)


def _sc_kernel(b_ref, c_ref, h_ref, w_ref, o_ref):
    u = c_ref[0] * h_ref[0]
    o_ref[0] = (b_ref[0] * _causal_conv(u, w_ref, SC_KERNEL)).astype(o_ref.dtype)


def _short_conv(proj, conv_w, width, *, tc):
    bsz, s, _ = proj.shape
    nb = width // tc
    w = jnp.zeros((8, width), F32).at[:SC_KERNEL].set(conv_w)
    return pl.pallas_call(
        _sc_kernel,
        grid=(bsz, nb),
        in_specs=[pl.BlockSpec((1, s, tc), lambda b, j: (b, 0, j)),
                  pl.BlockSpec((1, s, tc), lambda b, j: (b, 0, j + nb)),
                  pl.BlockSpec((1, s, tc), lambda b, j: (b, 0, j + 2 * nb)),
                  pl.BlockSpec((8, tc), lambda b, j: (0, j))],
        out_specs=pl.BlockSpec((1, s, tc), lambda b, j: (b, 0, j)),
        out_shape=jax.ShapeDtypeStruct((bsz, s, width), BF16),
        compiler_params=_params("parallel", "parallel"),
        name="short_conv",
    )(proj, proj, proj, w)


def _split(x):
    hi = x.astype(BF16)
    return hi, (x - hi.astype(F32)).astype(BF16)


def _dotb(a, b):
    return _dot(a.astype(BF16), b.astype(BF16))


DN_GROUP = 8
DN_HEADS_PER_STEP = 2


def _dn_group_local(base, scr, masks, out):
    q_s, k_s, kb_s, qd_s, kf_s, kbe_s, vb_s, gc_s, gcd_s = scr
    incl, strict, m8, m16, eye = masks
    c = DN_CHUNK
    idx = range(DN_HEADS_PER_STEP * DN_GROUP)
    rows = [(e // DN_GROUP, pl.ds(base + (e % DN_GROUP) * c, c)) for e in idx]
    gc = [gc_s[hh, r, :] for hh, r in rows]
    decay = []
    for e in idx:
        cc = e % DN_GROUP
        gc_j = gcd_s[e // DN_GROUP, pl.ds(base // LANES + cc // 2, 1), (cc % 2) * c:(cc % 2) * c + c]
        decay.append(jnp.where(incl, jnp.exp(jnp.where(incl, gc[e][:, :c] - gc_j, 0.0)), 0.0))
    kbf = [k_s[hh, r, :] for hh, r in rows]
    kk = [_dot_nt(kb_s[rows[cc][0], rows[cc][1], :], kbf[cc]) for cc in idx]
    qk = [_dot_nt(q_s[rows[cc][0], rows[cc][1], :], kbf[cc]) for cc in idx]
    yield
    a = [jnp.where(strict, kk[cc] * decay[cc], 0.0) for cc in idx]
    intra = [(qk[cc] * decay[cc]).astype(BF16) for cc in idx]
    ad = [jnp.where(m8, x, 0.0) for x in a]
    adb = [x.astype(BF16) for x in ad]
    a2 = [_dot(x, x) for x in adb]
    yield
    a2b = [x.astype(BF16) for x in a2]
    p = [eye - x for x in ad]
    p1, a4 = [], []
    for cc in idx:
        p1.append(p[cc] + _dotb(p[cc], a2b[cc]))
        a4.append(_dot(a2b[cc], a2b[cc]))
    yield
    p2 = [p1[cc] + _dotb(p1[cc], a4[cc]) for cc in idx]
    yield
    pb = [x.astype(BF16) for x in p2]
    t = [_dotb(pb[cc], jnp.where(m16, a[cc] - ad[cc], 0.0)) for cc in idx]
    yield
    dinv = [(p2[cc] - _dotb(t[cc], pb[cc])).astype(BF16) for cc in idx]
    yield
    db, da = [], []
    for cc in idx:
        hh, r = rows[cc]
        rhs = jnp.concatenate([vb_s[hh, r, :], kbe_s[hh, r, :]], axis=1)
        db.append(_dotb(dinv[cc], rhs))
        da.append(_dotb(dinv[cc], jnp.where(m16, 0.0, a[cc])).astype(BF16))
    yield
    blocks = [[x[0:16]] for x in db]
    for s4 in range(1, c // 16):
        rs = slice(16 * s4, 16 * s4 + 16)
        for cc in idx:
            xprev = jnp.concatenate(blocks[cc] + [jnp.zeros((c - 16 * s4, 2 * HEAD_DIM), F32)], axis=0)
            blocks[cc].append(db[cc][rs] - _dotb(da[cc][rs], xprev))
        yield
    for cc in idx:
        sol = jnp.concatenate(blocks[cc], axis=0)
        sol_hi, sol_lo = _split(sol)
        hh, r = rows[cc]
        g_last = gc[cc][c - 1:c, :]
        k_dec_t = (kf_s[hh, r, :] * jnp.exp(g_last - gc[cc])).T.astype(BF16)
        kw = _dot(k_dec_t, sol_hi) + _dot(k_dec_t, sol_lo)
        iw = _dot(intra[cc], sol_hi) + _dot(intra[cc], sol_lo)
        out.append((kw[:, HEAD_DIM:].astype(BF16), kw[:, :HEAD_DIM],
                    (qd_s[hh, r, :] - iw[:, HEAD_DIM:]).astype(BF16), iw[:, :HEAD_DIM], jnp.exp(g_last)))
    yield


def _dn_chunk_seq(state, loc, z, nw):
    w2, n_mat, qp, op, eg_last = loc
    sb = state.astype(BF16)
    o = _dot(qp, sb) + op
    state = (state * eg_last - _dot(w2, sb)) + n_mat
    o = o * lax.rsqrt(jnp.mean(o * o, axis=-1, keepdims=True) + NORM_EPS) * nw * _silu(z)
    return state, o


def _dn_kernel(qp_ref, kp_ref, vp_ref, z_ref, ba_ref, arow_ref, dtrow_ref, cwq_ref, cwk_ref, cwv_ref, nw_ref, o_ref,
               q_s, k_s, kb_s, qd_s, kf_s, kbe_s, vb_s, gc_s, beta_s, gates_s, gcd_s, *, n_heads):
    hb = DN_HEADS_PER_STEP
    s = qp_ref.shape[1]
    c = DN_CHUNK
    @pl.when(pl.program_id(1) == 0)
    def _():
        ba = ba_ref[0]
        xa = ba + dtrow_ref[...]
        softplus = jnp.maximum(xa, 0.0) + jnp.log(1.0 + jnp.exp(-jnp.abs(xa)))
        lane0 = lax.broadcasted_iota(jnp.int32, ba.shape, 1)
        gates = jnp.where(lane0 < n_heads, _sigmoid(ba), -jnp.exp(arow_ref[...]) * softplus)
        for blk in range(s // LANES):
            rs = slice(blk * LANES, (blk + 1) * LANES)
            gates_s[rs, :] = gates[rs, :].T

    n_blk = s // LANES
    pos = lax.broadcasted_iota(jnp.int32, (n_blk, LANES), 1) & (c - 1)
    for hh in range(hb):
        h = pl.program_id(1) * hb + hh
        cols = slice(hh * HEAD_DIM, (hh + 1) * HEAD_DIM)
        beta_d = jnp.concatenate([gates_s[pl.ds(blk * LANES + h, 1), :] for blk in range(n_blk)], axis=0)
        gc_d = jnp.concatenate([gates_s[pl.ds(blk * LANES + h + n_heads, 1), :] for blk in range(n_blk)], axis=0)
        sh = 1
        while sh < c:
            gc_d = gc_d + jnp.where(pos >= sh, pltpu.roll(gc_d, sh, axis=1), 0.0)
            sh *= 2
        gcd_s[hh, 0:n_blk, :] = gc_d
        for blk in range(n_blk):
            rs = slice(blk * LANES, (blk + 1) * LANES)
            gc_s[hh, rs, :] = jnp.broadcast_to(gc_d[blk:blk + 1, :], (LANES, LANES)).T
            beta_s[hh, rs, :] = jnp.broadcast_to(beta_d[blk:blk + 1, :], (LANES, LANES)).T
        gc = gc_s[hh]
        beta = beta_s[hh]
        eg = jnp.exp(gc)
        q = _silu(_causal_conv(qp_ref[0, :, cols], cwq_ref[:, cols], DN_CONV))
        q = q * (lax.rsqrt(jnp.sum(q * q, axis=-1, keepdims=True) + NORM_EPS) * (HEAD_DIM ** -0.5))
        q_s[hh] = q.astype(BF16)
        qd_s[hh] = q * eg
        k = _silu(_causal_conv(kp_ref[0, :, cols], cwk_ref[:, cols], DN_CONV))
        k = k * lax.rsqrt(jnp.sum(k * k, axis=-1, keepdims=True) + NORM_EPS)
        kb = k * beta
        kf_s[hh] = k
        k_s[hh] = k.astype(BF16)
        kb_s[hh] = kb.astype(BF16)
        kbe_s[hh] = kb * eg
        vb_s[hh] = _silu(_causal_conv(vp_ref[0, :, cols], cwv_ref[:, cols], DN_CONV)) * beta

    row = lax.broadcasted_iota(jnp.int32, (c, c), 0)
    col = lax.broadcasted_iota(jnp.int32, (c, c), 1)
    masks = (row >= col, row > col, (row >> 3) == (col >> 3), (row >> 4) == (col >> 4), (row == col).astype(F32))
    scr = (q_s, k_s, kb_s, qd_s, kf_s, kbe_s, vb_s, gc_s, gcd_s)
    nw = nw_ref[...]
    rows_per_group = DN_GROUP * c
    n_groups = s // rows_per_group

    def group_base(gi):
        base = gi * rows_per_group
        return base if isinstance(base, int) else pl.multiple_of(base, rows_per_group)

    def run(gi_local, gi_seq, states, locs):
        nxt = []
        stages = iter(()) if gi_local is None else _dn_group_local(group_base(gi_local), scr, masks, nxt)
        todo = list(range(DN_GROUP)) if gi_seq is not None else []
        states = list(states)
        done = False
        while todo or not done:
            if not done:
                done = next(stages, "end") == "end"
            if todo:
                cc = todo.pop(0)
                rows = pl.ds(group_base(gi_seq) + cc * c, c)
                for hh in range(hb):
                    cols = slice(hh * HEAD_DIM, (hh + 1) * HEAD_DIM)
                    states[hh], o = _dn_chunk_seq(states[hh], locs[hh * DN_GROUP + cc], z_ref[0, rows, cols], nw)
                    o_ref[0, rows, cols] = o.astype(o_ref.dtype)
        return tuple(states), tuple(nxt)

    def body(gi, carry):
        return run(gi + 1, gi, *carry)

    carry = run(0, None, (jnp.zeros((HEAD_DIM, HEAD_DIM), F32),) * hb, None)
    carry = lax.fori_loop(0, n_groups - 1, body, carry)
    run(None, n_groups - 1, *carry)


def _deltanet(proj, qkv_col0, z_col0, conv_w, ba, a_log, dt_bias, norm_w, n_heads):
    bsz, s, _ = proj.shape
    qb0 = qkv_col0 // HEAD_DIM
    zb0 = z_col0 // HEAD_DIM
    arow = jnp.zeros((1, LANES), F32).at[0, n_heads:2 * n_heads].set(a_log)
    dtrow = jnp.zeros((1, LANES), F32).at[0, n_heads:2 * n_heads].set(dt_bias)
    cw = jnp.zeros((8, 3 * n_heads * HEAD_DIM), F32).at[:DN_CONV].set(conv_w)
    hb = DN_HEADS_PER_STEP
    assert n_heads % hb == 0 and qb0 % hb == 0 and zb0 % hb == 0
    blk = (1, s, hb * HEAD_DIM)
    col_spec = lambda off: pl.BlockSpec(blk, lambda b, h, off=off: (b, 0, h + off // hb))
    cw_spec = lambda off: pl.BlockSpec((8, hb * HEAD_DIM), lambda b, h, off=off: (0, h + off // hb))
    row_spec = pl.BlockSpec((1, LANES), lambda b, h: (0, 0))
    per_head = lambda dtype: pltpu.VMEM((hb, s, HEAD_DIM), dtype)
    return pl.pallas_call(
        functools.partial(_dn_kernel, n_heads=n_heads),
        grid=(bsz, n_heads // hb),
        in_specs=[col_spec(qb0), col_spec(qb0 + n_heads), col_spec(qb0 + 2 * n_heads), col_spec(zb0),
                  pl.BlockSpec((1, s, LANES), lambda b, h: (b, 0, 0)), row_spec, row_spec,
                  cw_spec(0), cw_spec(n_heads), cw_spec(2 * n_heads), row_spec],
        out_specs=pl.BlockSpec(blk, lambda b, h: (b, 0, h)),
        out_shape=jax.ShapeDtypeStruct((bsz, s, n_heads * HEAD_DIM), BF16),
        scratch_shapes=([per_head(BF16)] * 3 + [per_head(F32)] * 6 + [pltpu.VMEM((s, LANES), F32)]
                        + [pltpu.VMEM((hb, max(8, s // LANES), LANES), F32)]),
        compiler_params=_params("parallel", "arbitrary", vmem=VMEM_LIMIT_BIG),
        name="deltanet",
    )(proj, proj, proj, proj, ba, arow, dtrow, cw, cw, cw, norm_w.reshape(1, HEAD_DIM))


def _rope_tables(ang):
    lane = lax.broadcasted_iota(jnp.int32, ang.shape, 1)
    sin = jnp.sin(ang)
    return jnp.cos(ang), jnp.where(lane < HEAD_DIM // 2, -sin, sin)


def _rope(x, cos, sin_signed):
    return x * cos + pltpu.roll(x, HEAD_DIM // 2, axis=1) * sin_signed


def _rope_kernel(ang_ref, q_ref, ks_ref, kw_ref, vs_ref, vw_ref, qo_ref, kso_ref, kwo_ref, vso_ref, vwo_ref):
    cos, sin = _rope_tables(ang_ref[0])
    for hh in range(NSA_HEADS):
        sl = slice(hh * HEAD_DIM, (hh + 1) * HEAD_DIM)
        qo_ref[0, :, sl] = (_rope(q_ref[0, :, sl], cos, sin) * Q_SCALE).astype(BF16)
    for g in range(NSA_KV_HEADS):
        sl = slice(g * HEAD_DIM, (g + 1) * HEAD_DIM)
        kso_ref[0, :, sl] = _rope(ks_ref[0, :, sl], cos, sin).astype(BF16)
        kwo_ref[0, :, sl] = _rope(kw_ref[0, :, sl], cos, sin).astype(BF16)
    ts = vs_ref.shape[1]
    for v_ref, vo_ref in ((vs_ref, vso_ref), (vw_ref, vwo_ref)):
        ck = vo_ref.shape[4]
        for g in range(NSA_KV_HEADS):
            for cc in range(ts // ck):
                parts = [v_ref[0, cc * ck + r:cc * ck + r + LANES, g * HEAD_DIM:(g + 1) * HEAD_DIM].T
                         for r in range(0, ck, LANES)]
                vo_ref[0, g, cc] = jnp.concatenate(parts, axis=1).astype(BF16)


def _rope_qkv(proj, ang, *, ts, slc_chunk, win_chunk):
    bsz, s, _ = proj.shape
    qw = NSA_HEADS * HEAD_DIM
    kvw = NSA_KV_HEADS * HEAD_DIM
    kv_spec = lambda blk: pl.BlockSpec((1, ts, kvw), lambda b, i, blk=blk: (b, i, blk))
    kv_out = pl.BlockSpec((1, ts, kvw), lambda b, i: (b, i, 0))
    kv_shape = jax.ShapeDtypeStruct((bsz, s, kvw), BF16)
    vt_out = lambda ck: pl.BlockSpec((1, NSA_KV_HEADS, ts // ck, HEAD_DIM, ck), lambda b, i: (b, 0, i, 0, 0))
    vt_shape = lambda ck: jax.ShapeDtypeStruct((bsz, NSA_KV_HEADS, s // ck, HEAD_DIM, ck), BF16)
    base = qw // kvw
    return pl.pallas_call(
        _rope_kernel,
        grid=(bsz, s // ts),
        in_specs=[pl.BlockSpec((1, ts, HEAD_DIM), lambda b, i: (b, i, 0)),
                  pl.BlockSpec((1, ts, qw), lambda b, i: (b, i, 0)),
                  kv_spec(base + 2), kv_spec(base + 4), kv_spec(base + 3), kv_spec(base + 5)],
        out_specs=[pl.BlockSpec((1, ts, qw), lambda b, i: (b, i, 0)), kv_out, kv_out,
                   vt_out(slc_chunk), vt_out(win_chunk)],
        out_shape=[jax.ShapeDtypeStruct((bsz, s, qw), BF16), kv_shape, kv_shape,
                   vt_shape(slc_chunk), vt_shape(win_chunk)],
        compiler_params=_params("parallel", "parallel"),
        name="rope_qkv",
    )(ang, proj, proj, proj, proj, proj)


def _gelu_tanh(x):
    return x * (0.5 * (1.0 + jnp.tanh(math.sqrt(2.0 / math.pi) * (x + 0.044715 * (x * x * x)))))


def _compress_kernel(x_ref, w1_ref, w2_ref, pos_ref, ang_ref, o_ref, *, rope):
    nsub = x_ref.shape[1] // CMP_STRIDE
    hid = w1_ref.shape[1]
    pa = jnp.zeros((nsub, hid), F32)
    pb = jnp.zeros((nsub, hid), F32)
    for l in range(CMP_STRIDE):
        xl = x_ref[0, pl.ds(l, nsub, stride=CMP_STRIDE), :].astype(BF16)
        pa = pa + _dot(xl, w1_ref[l * HEAD_DIM:(l + 1) * HEAD_DIM, :])
        pb = pb + _dot(xl, w1_ref[(CMP_STRIDE + l) * HEAD_DIM:(CMP_STRIDE + l + 1) * HEAD_DIM, :])
    bias = _dot(pos_ref[...], w1_ref[...])[0:1, :]
    hpre = pa + pltpu.roll(pb, nsub - 1, axis=0) + bias
    out = _dot(_gelu_tanh(hpre).astype(BF16), w2_ref[...])
    if rope:
        cos, sin = _rope_tables(ang_ref[0])
        out = _rope(out, cos, sin)
    o_ref[0, 0] = out.astype(o_ref.dtype)


def _compress(proj, col0, pos_emb, w1, w2, ang_cmp, *, rope):
    bsz, s, _ = proj.shape
    nsub = s // CMP_STRIDE
    blk0 = col0 // HEAD_DIM
    hid = w1.shape[1]
    pos = jnp.zeros((8, CMP_BLOCK * HEAD_DIM), BF16).at[0].set(pos_emb.reshape(-1).astype(BF16))
    return pl.pallas_call(
        functools.partial(_compress_kernel, rope=rope),
        grid=(bsz, NSA_KV_HEADS),
        in_specs=[pl.BlockSpec((1, s, HEAD_DIM), lambda b, g: (b, 0, g + blk0)),
                  pl.BlockSpec((CMP_BLOCK * HEAD_DIM, hid), lambda b, g: (0, 0)),
                  pl.BlockSpec((hid, HEAD_DIM), lambda b, g: (0, 0)),
                  pl.BlockSpec((8, CMP_BLOCK * HEAD_DIM), lambda b, g: (0, 0)),
                  pl.BlockSpec((1, nsub, HEAD_DIM), lambda b, g: (b, 0, 0))],
        out_specs=pl.BlockSpec((1, 1, nsub, HEAD_DIM), lambda b, g: (b, g, 0, 0)),
        out_shape=jax.ShapeDtypeStruct((bsz, NSA_KV_HEADS, nsub, HEAD_DIM), BF16),
        compiler_params=_params("parallel", "parallel"),
        name="compress",
    )(proj, w1.astype(BF16), w2.astype(BF16), pos, ang_cmp)


def _cmp_attn_kernel(q_ref, kc_ref, vc_ref, smat_ref, o_ref, sel_ref):
    tq = q_ref.shape[1]
    ncol = kc_ref.shape[2]
    t = pl.program_id(2) * tq + lax.broadcasted_iota(jnp.int32, (tq, ncol), 0)
    n = lax.broadcasted_iota(jnp.int32, (tq, ncol), 1)
    valid = (n * CMP_STRIDE + CMP_BLOCK - 1) <= t
    kc = kc_ref[0, 0]
    vc = vc_ref[0, 0]
    p_grp = jnp.zeros((tq, ncol), F32)
    for hh in range(HPG):
        sl = slice(hh * HEAD_DIM, (hh + 1) * HEAD_DIM)
        sc = jnp.where(valid, _dot_nt(q_ref[0, :, sl], kc), NEG_INF)
        e = jnp.exp2(sc - jnp.max(sc, axis=-1, keepdims=True))
        p = jnp.where(valid, e / jnp.sum(e, axis=-1, keepdims=True), 0.0)
        o_ref[0, :, sl] = _dot(p.astype(BF16), vc)
        p_grp = p_grp + p
    score = _dot_hi(p_grp, smat_ref[...])
    n_sel = sel_ref.shape[2]
    score = jnp.concatenate([score[r:r + LANES].T for r in range(0, tq, LANES)], axis=1)[:n_sel]
    n = lax.broadcasted_iota(jnp.int32, (n_sel, tq), 0)
    t = pl.program_id(2) * tq + lax.broadcasted_iota(jnp.int32, (n_sel, tq), 1)
    cur = t >> int(math.log2(SEL_BLOCK))
    forced = (n == 0) | (n == cur) | (n == cur - 1)
    future = n * SEL_BLOCK > t
    score = jnp.where(forced, jnp.inf, jnp.where(future, -jnp.inf, score))
    rank = jnp.zeros((n_sel, tq), jnp.int32)
    for kk in range(n_sel):
        ck = score[kk:kk + 1, :]
        ahead = (ck > score) | ((ck == score) & (kk < n))
        rank = rank + ahead.astype(jnp.int32)
    sel_ref[0, 0] = (rank < N_SELECT).astype(sel_ref.dtype)


def _sel_matrix(ncol, n_sel):
    rs = SEL_BLOCK // CMP_STRIDE
    rc = CMP_BLOCK // CMP_STRIDE
    mat = [[0.0] * ncol for _ in range(ncol)]
    for j in range(n_sel):
        for m in range(rs):
            for n in range(rc):
                i = rs * j + m + n - (rc - 1)
                if 0 <= i < ncol - 1:
                    mat[i][j] += 1.0
    return jnp.array(mat, F32)


def _cmp_attention(q_r, k_cmp, v_cmp, *, tq):
    bsz, s, _ = q_r.shape
    ncol = k_cmp.shape[2]
    n_sel = s // SEL_BLOCK
    gw = HPG * HEAD_DIM
    return pl.pallas_call(
        _cmp_attn_kernel,
        grid=(bsz, NSA_KV_HEADS, s // tq),
        in_specs=[pl.BlockSpec((1, tq, gw), lambda b, g, i: (b, i, g)),
                  pl.BlockSpec((1, 1, ncol, HEAD_DIM), lambda b, g, i: (b, g, 0, 0)),
                  pl.BlockSpec((1, 1, ncol, HEAD_DIM), lambda b, g, i: (b, g, 0, 0)),
                  pl.BlockSpec((ncol, ncol), lambda b, g, i: (0, 0))],
        out_specs=[pl.BlockSpec((1, tq, gw), lambda b, g, i: (b, i, g)),
                   pl.BlockSpec((1, 1, n_sel, tq), lambda b, g, i: (b, g, 0, i))],
        out_shape=[jax.ShapeDtypeStruct((bsz, s, NSA_HEADS * HEAD_DIM), F32),
                   jax.ShapeDtypeStruct((bsz, NSA_KV_HEADS, n_sel, s), F32)],
        compiler_params=_params("parallel", "parallel", "parallel"),
        name="cmp_attention",
    )(q_r, k_cmp, v_cmp, _sel_matrix(ncol, n_sel))


def _softmax_merge_t(parts):
    if len(parts) == 1:
        return parts[0][0] / parts[0][2]
    m = functools.reduce(jnp.maximum, [p[1] for p in parts])
    scale = [jnp.exp2(p[1] - m) for p in parts]
    acc = functools.reduce(lambda x, y: x + y, [p[0] * s for p, s in zip(parts, scale)])
    l = functools.reduce(lambda x, y: x + y, [p[2] * s for p, s in zip(parts, scale)])
    return acc / l


def _softmax_partial_t(scores, values_t):
    r = scores[0].shape[1]
    m8 = None
    for sc in scores:
        c8 = jnp.max(sc.reshape(-1, 8, r), axis=0)
        m8 = c8 if m8 is None else jnp.maximum(m8, c8)
    m = jnp.max(m8, axis=0, keepdims=True)
    l8 = None
    probs = []
    for sc in scores:
        p = jnp.exp2(sc - m)
        p8 = jnp.sum(p.reshape(-1, 8, r), axis=0)
        l8 = p8 if l8 is None else l8 + p8
        probs.append(p.astype(BF16))
    l = jnp.sum(l8, axis=0, keepdims=True)
    acc = _dot(jnp.concatenate(values_t, axis=1), jnp.concatenate(probs, axis=0))
    return acc, m, l


MASK_BIG = 2.0 ** 100


def _slc_win_kernel(q_ref, ks_ref, vst_ref, kw_ref, vwt_ref, selt_ref, blk_ref, oc_ref, gate_ref, o_ref,
                    gt_s, os_s, ow_s):
    tq = vwt_ref.shape[4]
    tk = vst_ref.shape[4]
    n_sub = q_ref.shape[1] // tq
    n_chunks = vst_ref.shape[2]
    n_sel = selt_ref.shape[2]
    g = pl.program_id(1)
    t_blk = pl.program_id(2) * (n_sub * tq)

    def tile_heads(mask):
        return jnp.concatenate([mask] * HPG, axis=1)

    key_loc = lax.broadcasted_iota(jnp.int32, (tq, tq), 0)
    qry_loc = lax.broadcasted_iota(jnp.int32, (tq, tq), 1)
    k_loc = lax.broadcasted_iota(jnp.int32, (tk, tq), 0)
    t_loc = lax.broadcasted_iota(jnp.int32, (tk, tq), 1)
    n_win = WINDOW // tq + 1
    q_ts, q_augs = [], []
    for sub in range(n_sub):
        rows = slice(sub * tq, (sub + 1) * tq)
        q_t = jnp.concatenate([q_ref[0, rows, hh * HEAD_DIM:(hh + 1) * HEAD_DIM].astype(F32).T.astype(BF16)
                               for hh in range(HPG)], axis=1)
        sel_bias = ((selt_ref[0, 0, :, rows] - 1.0) * MASK_BIG).astype(BF16)
        q_ts.append(q_t)
        q_augs.append(jnp.concatenate([q_t, tile_heads(sel_bias),
                                       jnp.zeros((HEAD_DIM - n_sel, HPG * tq), BF16)], axis=0))

    def window_scores(in_range, sub):
        t0 = t_blk + sub * tq
        if in_range:
            k0 = pl.multiple_of(t0 - WINDOW, tq)
            sc = _dot(kw_ref[0, pl.ds(k0, n_win * tq), :], q_ts[sub])
            scores = [sc[cc * tq:(cc + 1) * tq] for cc in range(n_win)]
            scores[0] = jnp.where(tile_heads(key_loc > qry_loc), scores[0], NEG_INF)
            scores[-1] = jnp.where(tile_heads(key_loc <= qry_loc), scores[-1], NEG_INF)
            return scores, [vwt_ref[0, 0, k0 // tq + cc] for cc in range(n_win)]
        sc = _dot(kw_ref[0, 0:WINDOW, :], q_ts[sub])
        scores = [jnp.where(tile_heads(cc * tq + key_loc <= t0 + qry_loc), sc[cc * tq:(cc + 1) * tq], NEG_INF)
                  for cc in range(WINDOW // tq)]
        return scores, [vwt_ref[0, 0, cc] for cc in range(WINDOW // tq)]

    def block_variant(n_used):
        n_groups = min(n_used, 2)
        bounds = [(n_used * gg) // n_groups for gg in range(n_groups + 1)]
        sel_groups, win = [], []
        for sub in range(n_sub):
            groups = []
            for lo, hi in zip(bounds[:-1], bounds[1:]):
                rows = slice(lo * tk, hi * tk)
                sc = _dot(jnp.concatenate([ks_ref[0, rows, :], blk_ref[rows, :]], axis=1), q_augs[sub])
                groups.append([sc[cc * tk:(cc + 1) * tk] for cc in range(hi - lo)])
            causal = ((n_used - 1) * tk + k_loc) <= (t_blk + sub * tq + t_loc)
            groups[-1][-1] = jnp.where(tile_heads(causal), groups[-1][-1], NEG_INF)
            sel_groups.append(groups)
            win.append(window_scores((n_used - 1) * tk >= WINDOW, sub))
        values = [[vst_ref[0, 0, cc] for cc in range(bounds[gg], bounds[gg + 1])] for gg in range(n_groups)]
        parts = [[_softmax_partial_t(sel_groups[sub][0], values[0])] for sub in range(n_sub)]
        for sub in range(n_sub):
            ow_s[sub] = _softmax_merge_t([_softmax_partial_t(*win[sub])])
        for gg in range(1, n_groups):
            for sub in range(n_sub):
                parts[sub].append(_softmax_partial_t(sel_groups[sub][gg], values[gg]))
        for sub in range(n_sub):
            os_s[sub] = _softmax_merge_t(parts[sub])

    for vv in range(n_chunks):
        pl.when(t_blk // tk == vv)(functools.partial(block_variant, vv + 1))

    for sub in range(n_sub):
        rows = slice(sub * tq, (sub + 1) * tq)
        gt = _sigmoid(gate_ref[0, rows, :])
        gt_s[sub] = gt.T
        lane = lax.broadcasted_iota(jnp.int32, gt.shape, 1)
        for hh in range(HPG):
            base = (g * HPG + hh) * 3
            g_cmp = jnp.sum(jnp.where(lane == base, gt, 0.0), axis=1, keepdims=True)
            cs = slice(hh * tq, (hh + 1) * tq)
            mix_t = (gt_s[sub, pl.ds(base + 1, 1), :] * os_s[sub, :, cs]
                     + gt_s[sub, pl.ds(base + 2, 1), :] * ow_s[sub, :, cs])
            sl = slice(hh * HEAD_DIM, (hh + 1) * HEAD_DIM)
            o_ref[0, rows, sl] = (g_cmp * oc_ref[0, rows, sl] + mix_t.T).astype(o_ref.dtype)


def _slc_win_attention(q_r, ks, vs_t, kw, vw_t, sel_t, o_cmp, gates):
    bsz, s, _ = q_r.shape
    n_sel = s // SEL_BLOCK
    gw = HPG * HEAD_DIM
    tb = vs_t.shape[4]
    tq = vw_t.shape[4]
    assert tq == LANES and tb % tq == 0 and WINDOW % tb == 0
    n_sub = tb // tq
    kv_spec = pl.BlockSpec((1, s, HEAD_DIM), lambda b, g, i: (b, 0, g))
    vt_spec = lambda a: pl.BlockSpec((1, 1) + a.shape[2:], lambda b, g, i: (b, g, 0, 0, 0))
    block_onehot = (jnp.arange(s)[:, None] // SEL_BLOCK == jnp.arange(HEAD_DIM)[None, :]).astype(BF16)
    return pl.pallas_call(
        _slc_win_kernel,
        grid=(bsz, NSA_KV_HEADS, s // tb),
        in_specs=[pl.BlockSpec((1, tb, gw), lambda b, g, i: (b, i, g)),
                  kv_spec, vt_spec(vs_t), kv_spec, vt_spec(vw_t),
                  pl.BlockSpec((1, 1, n_sel, tb), lambda b, g, i: (b, g, 0, i)),
                  pl.BlockSpec((s, HEAD_DIM), lambda b, g, i: (0, 0)),
                  pl.BlockSpec((1, tb, gw), lambda b, g, i: (b, i, g)),
                  pl.BlockSpec((1, tb, LANES), lambda b, g, i: (b, i, 0))],
        out_specs=pl.BlockSpec((1, tb, gw), lambda b, g, i: (b, i, g)),
        out_shape=jax.ShapeDtypeStruct((bsz, s, NSA_HEADS * HEAD_DIM), BF16),
        scratch_shapes=[pltpu.VMEM((n_sub, LANES, tq), F32), pltpu.VMEM((n_sub, HEAD_DIM, HPG * tq), F32),
                        pltpu.VMEM((n_sub, HEAD_DIM, HPG * tq), F32)],
        compiler_params=_params("parallel", "parallel", "arbitrary"),
        name="slc_win_attention",
    )(q_r, ks, vs_t, kw, vw_t, sel_t, block_onehot, o_cmp, gates)


def _pad_cols(w, n):
    return jnp.pad(w, ((0, 0), (0, n - w.shape[1])))


def _conv_deltanet_mixer(xb, bsz, s, w_in, sc_conv_w, dn_conv_w, a_log, dt_bias, norm_w, w_out):
    sc_w = sc_conv_w.shape[1]
    dn_w = dn_conv_w.shape[1] // 3
    n_heads = dn_w // HEAD_DIM
    main = 3 * sc_w + 4 * dn_w
    proj, ba = _in_proj(xb, w_in.astype(BF16), main, _pad_cols(w_in[:, main:], LANES).astype(BF16),
                        tm=1024, tn=1024)
    proj = proj.reshape(bsz, s, main)
    y_sc = _short_conv(proj, sc_conv_w, sc_w, tc=256)
    y_dn = _deltanet(proj, 3 * sc_w, 3 * sc_w + 3 * dn_w, dn_conv_w, ba.reshape(bsz, s, LANES), a_log, dt_bias,
                     norm_w, n_heads)
    wo = w_out.astype(BF16)
    return [y_sc.reshape(bsz * s, sc_w), y_dn.reshape(bsz * s, dn_w)], [wo[:sc_w], wo[sc_w:]]


def _nsa_mixer(xb, bsz, s, positions, w_in, cmp_pos_k, cmp_w1_k, cmp_w2_k, cmp_pos_v, cmp_w1_v, cmp_w2_v, w_out):
    qw = NSA_HEADS * HEAD_DIM
    kvw = NSA_KV_HEADS * HEAD_DIM
    main = qw + 6 * kvw
    proj, gates = _in_proj(xb, w_in.astype(BF16), main, _pad_cols(w_in[:, main:], LANES).astype(BF16),
                           tm=1024, tn=1024)
    proj = proj.reshape(bsz, s, main)
    half = HEAD_DIM // 2
    inv = jnp.power(ROPE_THETA, -jnp.arange(half, dtype=F32) / half)
    inv = jnp.concatenate([inv, inv])
    ang = positions.astype(F32)[..., None] * inv
    cmp_end = jnp.minimum(jnp.arange(s // CMP_STRIDE) * CMP_STRIDE + CMP_BLOCK - 1, s - 1)
    ang_cmp = positions[:, cmp_end].astype(F32)[..., None] * inv
    q_r, ks, kw, vs_t, vw_t = _rope_qkv(proj, ang, ts=512, slc_chunk=256, win_chunk=LANES)
    k_cmp = _compress(proj, qw, cmp_pos_k, cmp_w1_k, cmp_w2_k, ang_cmp, rope=True)
    v_cmp = _compress(proj, qw + kvw, cmp_pos_v, cmp_w1_v, cmp_w2_v, ang_cmp, rope=False)
    o_cmp, sel_t = _cmp_attention(q_r, k_cmp, v_cmp, tq=1024)
    o = _slc_win_attention(q_r, ks, vs_t, kw, vw_t, sel_t, o_cmp, gates.reshape(bsz, s, LANES))
    return [o.reshape(bsz * s, qw)], [w_out.astype(BF16)]


def kernel(x, positions, ln_mix_g, ln_mix_b, ln_ffn_g, ln_ffn_b, ffn_w_in, ffn_w_out, hy_w_in, sc_conv_w, dn_conv_w, dn_a_log, dn_dt_bias, dn_norm_w, hy_w_out, nsa_w_in, cmp_pos_k, cmp_w1_k, cmp_w2_k, cmp_pos_v, cmp_w1_v, cmp_w2_v, nsa_w_out):
    bsz, s, d = x.shape
    xf = x.reshape(bsz * s, d)
    xb = xf
    ffn_w_out_b = ffn_w_out.astype(BF16)
    for i in range(DEPTH):
        j = i // 2
        if i % 2 == 0:
            ys, wos = _conv_deltanet_mixer(xb, bsz, s, hy_w_in[j], sc_conv_w[j], dn_conv_w[j], dn_a_log[j],
                                           dn_dt_bias[j], dn_norm_w[j], hy_w_out[j])
        else:
            ys, wos = _nsa_mixer(xb, bsz, s, positions, nsa_w_in[j], cmp_pos_k[j], cmp_w1_k[j], cmp_w2_k[j],
                                 cmp_pos_v[j], cmp_w1_v[j], cmp_w2_v[j], nsa_w_out[j])
        xf, xb = _matmul_ln(ys, wos, xf, ln_mix_g[i], ln_mix_b[i], tm=512, tn=d)
        hmid = _ffn_in(xb, ffn_w_in, i, tm=1024, tn=512)
        xf, xb = _matmul_ln([hmid], [ffn_w_out_b], xf, ln_ffn_g[i], ln_ffn_b[i], tm=512, tn=512, layer=i)
    return xf.reshape(bsz, s, d)
```

```python
import functools
import math

import jax
import jax.numpy as jnp
from jax import lax
from jax.experimental import pallas as pl
from jax.experimental.pallas import tpu as pltpu

F32 = jnp.float32
BF16 = jnp.bfloat16
HIGHEST = lax.Precision.HIGHEST

LANES = 128
VMEM_LIMIT = 48 * 1024 * 1024
VMEM_LIMIT_BIG = 56 * 1024 * 1024

DN_HEADS = 8
DN_CHUNK = 64
DN_CONV = 4
SC_KERNEL = 3
NSA_HEADS = 16
NSA_KV_HEADS = 4
HPG = NSA_HEADS // NSA_KV_HEADS
HEAD_DIM = 128
CMP_BLOCK = 32
CMP_STRIDE = 16
SEL_BLOCK = 64
N_SELECT = 16
WINDOW = 512
ROPE_THETA = 10000.0
LN_EPS = 1e-5
NORM_EPS = 1e-6
NEG_INF = -1e30
DEPTH = 2
ALPHA = (2 * DEPTH) ** 0.25
ATTN_SCALE = HEAD_DIM ** -0.5
Q_SCALE = ATTN_SCALE * math.log2(math.e)


def _params(*sem, vmem=VMEM_LIMIT):
    return pltpu.CompilerParams(dimension_semantics=sem, vmem_limit_bytes=vmem)


def _sigmoid(x):
    return 1.0 / (1.0 + jnp.exp(-x))


def _silu(x):
    return x * _sigmoid(x)


def _dot(a, b):
    return jnp.dot(a, b, preferred_element_type=F32)


def _dot_nt(a, b):
    return lax.dot_general(a, b, (((1,), (1,)), ((), ())), preferred_element_type=F32)


def _dot_hi(a, b):
    return jnp.dot(a, b, precision=HIGHEST, preferred_element_type=F32)


def _proj_kernel(x_ref, w_ref, ws_ref, o_ref, os_ref):
    xb = x_ref[...].astype(BF16)
    o_ref[...] = _dot(xb, w_ref[...])

    @pl.when(pl.program_id(1) == 0)
    def _():
        os_ref[...] = _dot(xb, ws_ref[...])


def _in_proj(x, w, n, w_side, *, tm, tn):
    m, k = x.shape
    ns = w_side.shape[1]
    return pl.pallas_call(
        _proj_kernel,
        grid=(m // tm, n // tn),
        in_specs=[pl.BlockSpec((tm, k), lambda i, j: (i, 0)),
                  pl.BlockSpec((k, tn), lambda i, j: (0, j)),
                  pl.BlockSpec((k, ns), lambda i, j: (0, 0))],
        out_specs=[pl.BlockSpec((tm, tn), lambda i, j: (i, j)),
                   pl.BlockSpec((tm, ns), lambda i, j: (i, 0))],
        out_shape=[jax.ShapeDtypeStruct((m, n), F32), jax.ShapeDtypeStruct((m, ns), F32)],
        compiler_params=_params("parallel", "arbitrary"),
        name="in_proj",
    )(x, w, w_side)


def _ffn_in_kernel(x_ref, wg_ref, wu_ref, o_ref, wgb_s, wub_s):
    @pl.when(pl.program_id(1) == 0)
    def _():
        wgb_s[...] = wg_ref[...].astype(BF16)
        wub_s[...] = wu_ref[...].astype(BF16)

    x = x_ref[...]
    gate = _dot(x, wgb_s[...])
    up = _dot(x, wub_s[...])
    o_ref[...] = (_silu(gate) * up).astype(o_ref.dtype)


def _ffn_in(xb, w_in, layer, *, tm, tn):
    m, k = xb.shape
    hidden = w_in.shape[2] // 2
    nj = hidden // tn
    return pl.pallas_call(
        _ffn_in_kernel,
        grid=(nj, m // tm),
        in_specs=[pl.BlockSpec((tm, k), lambda j, i: (i, 0)),
                  pl.BlockSpec((None, k, tn), lambda j, i: (layer, 0, j)),
                  pl.BlockSpec((None, k, tn), lambda j, i: (layer, 0, j + nj))],
        out_specs=pl.BlockSpec((tm, tn), lambda j, i: (i, j)),
        out_shape=jax.ShapeDtypeStruct((m, hidden), BF16),
        scratch_shapes=[pltpu.VMEM((k, tn), BF16), pltpu.VMEM((k, tn), BF16)],
        compiler_params=_params("parallel", "arbitrary"),
        name="ffn_in",
    )(xb, w_in, w_in)


MM_LN_ROWS = 128


def _mm_ln_kernel(*refs, n_pairs, nj):
    xs = refs[:n_pairs]
    ws = refs[n_pairs:2 * n_pairs]
    r_ref, g_ref, b_ref, o_ref, ob_ref, y_s = refs[2 * n_pairs:2 * n_pairs + 6]
    tm = o_ref.shape[0]
    j = pl.program_id(1)
    part = _dot(xs[0][...], ws[0][...])
    for x_ref, w_ref in zip(xs[1:], ws[1:]):
        part = part + _dot(x_ref[...], w_ref[...])
    y_s[j] = part

    @pl.when(j == nj - 1)
    def _():
        for r0 in range(0, tm, MM_LN_ROWS):
            rows = slice(r0, r0 + MM_LN_ROWS)
            y = jnp.concatenate([y_s[jj, rows, :] for jj in range(nj)], axis=1)
            v = ALPHA * r_ref[rows, :] + y
            mu = jnp.mean(v, axis=-1, keepdims=True)
            dv = v - mu
            var = jnp.mean(dv * dv, axis=-1, keepdims=True)
            out = dv * lax.rsqrt(var + LN_EPS) * g_ref[...] + b_ref[...]
            o_ref[rows, :] = out
            ob_ref[rows, :] = out.astype(BF16)


def _matmul_ln(xs, ws, resid, g, b, *, tm, tn, layer=None):
    m, d = resid.shape
    n_pairs = len(xs)
    nj = d // tn
    ks = [x.shape[1] for x in xs]
    if layer is None:
        w_specs = [pl.BlockSpec((k, tn), lambda i, j: (0, j)) for k in ks]
    else:
        w_specs = [pl.BlockSpec((None, k, tn), lambda i, j: (layer, 0, j)) for k in ks]
    in_specs = ([pl.BlockSpec((tm, k), lambda i, j: (i, 0)) for k in ks]
                + w_specs
                + [pl.BlockSpec((tm, d), lambda i, j: (i, 0)),
                   pl.BlockSpec((1, d), lambda i, j: (0, 0)),
                   pl.BlockSpec((1, d), lambda i, j: (0, 0))])
    return pl.pallas_call(
        functools.partial(_mm_ln_kernel, n_pairs=n_pairs, nj=nj),
        grid=(m // tm, nj),
        in_specs=in_specs,
        out_specs=[pl.BlockSpec((tm, d), lambda i, j: (i, 0)),
                   pl.BlockSpec((tm, d), lambda i, j: (i, 0))],
        out_shape=[jax.ShapeDtypeStruct((m, d), F32), jax.ShapeDtypeStruct((m, d), BF16)],
        scratch_shapes=[pltpu.VMEM((nj, tm, tn), F32)],
        compiler_params=_params("parallel", "arbitrary", vmem=VMEM_LIMIT_BIG),
        name="matmul_ln",
    )(*xs, *ws, resid, g.reshape(1, d), b.reshape(1, d))


def _causal_conv(u, w_ref, taps):
    def tap_sum(x, shift):
        acc = x * w_ref[taps - 1:taps, :]
        for sh in range(1, taps):
            acc = acc + shift(x, sh) * w_ref[taps - 1 - sh:taps - sh, :]
        return acc

    body = tap_sum(u, lambda x, sh: pltpu.roll(x, sh, axis=0))
    row = lax.broadcasted_iota(jnp.int32, (8, u.shape[1]), 0)
    head = tap_sum(u[0:8], lambda x, sh: jnp.where(row >= sh, pltpu.roll(x, sh, axis=0), 0.0))
    return jnp.concatenate([head, body[8:]], axis=0)


def _sc_kernel(b_ref, c_ref, h_ref, w_ref, o_ref):
    u = c_ref[0] * h_ref[0]
    o_ref[0] = (b_ref[0] * _causal_conv(u, w_ref, SC_KERNEL)).astype(o_ref.dtype)


def _short_conv(proj, conv_w, width, *, tc):
    bsz, s, _ = proj.shape
    nb = width // tc
    w = jnp.zeros((8, width), F32).at[:SC_KERNEL].set(conv_w)
    return pl.pallas_call(
        _sc_kernel,
        grid=(bsz, nb),
        in_specs=[pl.BlockSpec((1, s, tc), lambda b, j: (b, 0, j)),
                  pl.BlockSpec((1, s, tc), lambda b, j: (b, 0, j + nb)),
                  pl.BlockSpec((1, s, tc), lambda b, j: (b, 0, j + 2 * nb)),
                  pl.BlockSpec((8, tc), lambda b, j: (0, j))],
        out_specs=pl.BlockSpec((1, s, tc), lambda b, j: (b, 0, j)),
        out_shape=jax.ShapeDtypeStruct((bsz, s, width), BF16),
        compiler_params=_params("parallel", "parallel"),
        name="short_conv",
    )(proj, proj, proj, w)


def _split(x):
    hi = x.astype(BF16)
    return hi, (x - hi.astype(F32)).astype(BF16)


def _dotb(a, b):
    return _dot(a.astype(BF16), b.astype(BF16))


DN_GROUP = 8
DN_HEADS_PER_STEP = 2


def _dn_group_local(base, scr, masks, out):
    q_s, k_s, kb_s, qd_s, kf_s, kbe_s, vb_s, gc_s, gcd_s = scr
    incl, strict, m8, m16, eye = masks
    c = DN_CHUNK
    idx = range(DN_HEADS_PER_STEP * DN_GROUP)
    rows = [(e // DN_GROUP, pl.ds(base + (e % DN_GROUP) * c, c)) for e in idx]
    gc = [gc_s[hh, r, :] for hh, r in rows]
    decay = []
    for e in idx:
        cc = e % DN_GROUP
        gc_j = gcd_s[e // DN_GROUP, pl.ds(base // LANES + cc // 2, 1), (cc % 2) * c:(cc % 2) * c + c]
        decay.append(jnp.where(incl, jnp.exp(jnp.where(incl, gc[e][:, :c] - gc_j, 0.0)), 0.0))
    kbf = [k_s[hh, r, :] for hh, r in rows]
    kk = [_dot_nt(kb_s[rows[cc][0], rows[cc][1], :], kbf[cc]) for cc in idx]
    qk = [_dot_nt(q_s[rows[cc][0], rows[cc][1], :], kbf[cc]) for cc in idx]
    yield
    a = [jnp.where(strict, kk[cc] * decay[cc], 0.0) for cc in idx]
    intra = [(qk[cc] * decay[cc]).astype(BF16) for cc in idx]
    ad = [jnp.where(m8, x, 0.0) for x in a]
    adb = [x.astype(BF16) for x in ad]
    a2 = [_dot(x, x) for x in adb]
    yield
    a2b = [x.astype(BF16) for x in a2]
    p = [eye - x for x in ad]
    p1, a4 = [], []
    for cc in idx:
        p1.append(p[cc] + _dotb(p[cc], a2b[cc]))
        a4.append(_dot(a2b[cc], a2b[cc]))
    yield
    p2 = [p1[cc] + _dotb(p1[cc], a4[cc]) for cc in idx]
    yield
    pb = [x.astype(BF16) for x in p2]
    t = [_dotb(pb[cc], jnp.where(m16, a[cc] - ad[cc], 0.0)) for cc in idx]
    yield
    dinv = [(p2[cc] - _dotb(t[cc], pb[cc])).astype(BF16) for cc in idx]
    yield
    db, da = [], []
    for cc in idx:
        hh, r = rows[cc]
        rhs = jnp.concatenate([vb_s[hh, r, :], kbe_s[hh, r, :]], axis=1)
        db.append(_dotb(dinv[cc], rhs))
        da.append(_dotb(dinv[cc], jnp.where(m16, 0.0, a[cc])).astype(BF16))
    yield
    blocks = [[x[0:16]] for x in db]
    for s4 in range(1, c // 16):
        rs = slice(16 * s4, 16 * s4 + 16)
        for cc in idx:
            xprev = jnp.concatenate(blocks[cc] + [jnp.zeros((c - 16 * s4, 2 * HEAD_DIM), F32)], axis=0)
            blocks[cc].append(db[cc][rs] - _dotb(da[cc][rs], xprev))
        yield
    for cc in idx:
        sol = jnp.concatenate(blocks[cc], axis=0)
        sol_hi, sol_lo = _split(sol)
        hh, r = rows[cc]
        g_last = gc[cc][c - 1:c, :]
        k_dec_t = (kf_s[hh, r, :] * jnp.exp(g_last - gc[cc])).T.astype(BF16)
        kw = _dot(k_dec_t, sol_hi) + _dot(k_dec_t, sol_lo)
        iw = _dot(intra[cc], sol_hi) + _dot(intra[cc], sol_lo)
        out.append((kw[:, HEAD_DIM:].astype(BF16), kw[:, :HEAD_DIM],
                    (qd_s[hh, r, :] - iw[:, HEAD_DIM:]).astype(BF16), iw[:, :HEAD_DIM], jnp.exp(g_last)))
    yield


def _dn_chunk_seq(state, loc, z, nw):
    w2, n_mat, qp, op, eg_last = loc
    sb = state.astype(BF16)
    o = _dot(qp, sb) + op
    state = (state * eg_last - _dot(w2, sb)) + n_mat
    o = o * lax.rsqrt(jnp.mean(o * o, axis=-1, keepdims=True) + NORM_EPS) * nw * _silu(z)
    return state, o


def _dn_kernel(qp_ref, kp_ref, vp_ref, z_ref, ba_ref, arow_ref, dtrow_ref, cwq_ref, cwk_ref, cwv_ref, nw_ref, o_ref,
               q_s, k_s, kb_s, qd_s, kf_s, kbe_s, vb_s, gc_s, beta_s, gates_s, gcd_s, *, n_heads):
    hb = DN_HEADS_PER_STEP
    s = qp_ref.shape[1]
    c = DN_CHUNK
    @pl.when(pl.program_id(1) == 0)
    def _():
        ba = ba_ref[0]
        xa = ba + dtrow_ref[...]
        softplus = jnp.maximum(xa, 0.0) + jnp.log(1.0 + jnp.exp(-jnp.abs(xa)))
        lane0 = lax.broadcasted_iota(jnp.int32, ba.shape, 1)
        gates = jnp.where(lane0 < n_heads, _sigmoid(ba), -jnp.exp(arow_ref[...]) * softplus)
        for blk in range(s // LANES):
            rs = slice(blk * LANES, (blk + 1) * LANES)
            gates_s[rs, :] = gates[rs, :].T

    n_blk = s // LANES
    pos = lax.broadcasted_iota(jnp.int32, (n_blk, LANES), 1) & (c - 1)
    for hh in range(hb):
        h = pl.program_id(1) * hb + hh
        cols = slice(hh * HEAD_DIM, (hh + 1) * HEAD_DIM)
        beta_d = jnp.concatenate([gates_s[pl.ds(blk * LANES + h, 1), :] for blk in range(n_blk)], axis=0)
        gc_d = jnp.concatenate([gates_s[pl.ds(blk * LANES + h + n_heads, 1), :] for blk in range(n_blk)], axis=0)
        sh = 1
        while sh < c:
            gc_d = gc_d + jnp.where(pos >= sh, pltpu.roll(gc_d, sh, axis=1), 0.0)
            sh *= 2
        gcd_s[hh, 0:n_blk, :] = gc_d
        for blk in range(n_blk):
            rs = slice(blk * LANES, (blk + 1) * LANES)
            gc_s[hh, rs, :] = jnp.broadcast_to(gc_d[blk:blk + 1, :], (LANES, LANES)).T
            beta_s[hh, rs, :] = jnp.broadcast_to(beta_d[blk:blk + 1, :], (LANES, LANES)).T
        gc = gc_s[hh]
        beta = beta_s[hh]
        eg = jnp.exp(gc)
        q = _silu(_causal_conv(qp_ref[0, :, cols], cwq_ref[:, cols], DN_CONV))
        q = q * (lax.rsqrt(jnp.sum(q * q, axis=-1, keepdims=True) + NORM_EPS) * (HEAD_DIM ** -0.5))
        q_s[hh] = q.astype(BF16)
        qd_s[hh] = q * eg
        k = _silu(_causal_conv(kp_ref[0, :, cols], cwk_ref[:, cols], DN_CONV))
        k = k * lax.rsqrt(jnp.sum(k * k, axis=-1, keepdims=True) + NORM_EPS)
        kb = k * beta
        kf_s[hh] = k
        k_s[hh] = k.astype(BF16)
        kb_s[hh] = kb.astype(BF16)
        kbe_s[hh] = kb * eg
        vb_s[hh] = _silu(_causal_conv(vp_ref[0, :, cols], cwv_ref[:, cols], DN_CONV)) * beta

    row = lax.broadcasted_iota(jnp.int32, (c, c), 0)
    col = lax.broadcasted_iota(jnp.int32, (c, c), 1)
    masks = (row >= col, row > col, (row >> 3) == (col >> 3), (row >> 4) == (col >> 4), (row == col).astype(F32))
    scr = (q_s, k_s, kb_s, qd_s, kf_s, kbe_s, vb_s, gc_s, gcd_s)
    nw = nw_ref[...]
    rows_per_group = DN_GROUP * c
    n_groups = s // rows_per_group

    def group_base(gi):
        base = gi * rows_per_group
        return base if isinstance(base, int) else pl.multiple_of(base, rows_per_group)

    def run(gi_local, gi_seq, states, locs):
        nxt = []
        stages = iter(()) if gi_local is None else _dn_group_local(group_base(gi_local), scr, masks, nxt)
        todo = list(range(DN_GROUP)) if gi_seq is not None else []
        states = list(states)
        done = False
        while todo or not done:
            if not done:
                done = next(stages, "end") == "end"
            if todo:
                cc = todo.pop(0)
                rows = pl.ds(group_base(gi_seq) + cc * c, c)
                for hh in range(hb):
                    cols = slice(hh * HEAD_DIM, (hh + 1) * HEAD_DIM)
                    states[hh], o = _dn_chunk_seq(states[hh], locs[hh * DN_GROUP + cc], z_ref[0, rows, cols], nw)
                    o_ref[0, rows, cols] = o.astype(o_ref.dtype)
        return tuple(states), tuple(nxt)

    def body(gi, carry):
        return run(gi + 1, gi, *carry)

    carry = run(0, None, (jnp.zeros((HEAD_DIM, HEAD_DIM), F32),) * hb, None)
    carry = lax.fori_loop(0, n_groups - 1, body, carry)
    run(None, n_groups - 1, *carry)


def _deltanet(proj, qkv_col0, z_col0, conv_w, ba, a_log, dt_bias, norm_w, n_heads):
    bsz, s, _ = proj.shape
    qb0 = qkv_col0 // HEAD_DIM
    zb0 = z_col0 // HEAD_DIM
    arow = jnp.zeros((1, LANES), F32).at[0, n_heads:2 * n_heads].set(a_log)
    dtrow = jnp.zeros((1, LANES), F32).at[0, n_heads:2 * n_heads].set(dt_bias)
    cw = jnp.zeros((8, 3 * n_heads * HEAD_DIM), F32).at[:DN_CONV].set(conv_w)
    hb = DN_HEADS_PER_STEP
    assert n_heads % hb == 0 and qb0 % hb == 0 and zb0 % hb == 0
    blk = (1, s, hb * HEAD_DIM)
    col_spec = lambda off: pl.BlockSpec(blk, lambda b, h, off=off: (b, 0, h + off // hb))
    cw_spec = lambda off: pl.BlockSpec((8, hb * HEAD_DIM), lambda b, h, off=off: (0, h + off // hb))
    row_spec = pl.BlockSpec((1, LANES), lambda b, h: (0, 0))
    per_head = lambda dtype: pltpu.VMEM((hb, s, HEAD_DIM), dtype)
    return pl.pallas_call(
        functools.partial(_dn_kernel, n_heads=n_heads),
        grid=(bsz, n_heads // hb),
        in_specs=[col_spec(qb0), col_spec(qb0 + n_heads), col_spec(qb0 + 2 * n_heads), col_spec(zb0),
                  pl.BlockSpec((1, s, LANES), lambda b, h: (b, 0, 0)), row_spec, row_spec,
                  cw_spec(0), cw_spec(n_heads), cw_spec(2 * n_heads), row_spec],
        out_specs=pl.BlockSpec(blk, lambda b, h: (b, 0, h)),
        out_shape=jax.ShapeDtypeStruct((bsz, s, n_heads * HEAD_DIM), BF16),
        scratch_shapes=([per_head(BF16)] * 3 + [per_head(F32)] * 6 + [pltpu.VMEM((s, LANES), F32)]
                        + [pltpu.VMEM((hb, max(8, s // LANES), LANES), F32)]),
        compiler_params=_params("parallel", "arbitrary", vmem=VMEM_LIMIT_BIG),
        name="deltanet",
    )(proj, proj, proj, proj, ba, arow, dtrow, cw, cw, cw, norm_w.reshape(1, HEAD_DIM))


def _rope_tables(ang):
    lane = lax.broadcasted_iota(jnp.int32, ang.shape, 1)
    sin = jnp.sin(ang)
    return jnp.cos(ang), jnp.where(lane < HEAD_DIM // 2, -sin, sin)


def _rope(x, cos, sin_signed):
    return x * cos + pltpu.roll(x, HEAD_DIM // 2, axis=1) * sin_signed


def _rope_kernel(ang_ref, q_ref, ks_ref, kw_ref, vs_ref, vw_ref, qo_ref, kso_ref, kwo_ref, vso_ref, vwo_ref):
    cos, sin = _rope_tables(ang_ref[0])
    for hh in range(NSA_HEADS):
        sl = slice(hh * HEAD_DIM, (hh + 1) * HEAD_DIM)
        qo_ref[0, :, sl] = (_rope(q_ref[0, :, sl], cos, sin) * Q_SCALE).astype(BF16)
    for g in range(NSA_KV_HEADS):
        sl = slice(g * HEAD_DIM, (g + 1) * HEAD_DIM)
        kso_ref[0, :, sl] = _rope(ks_ref[0, :, sl], cos, sin).astype(BF16)
        kwo_ref[0, :, sl] = _rope(kw_ref[0, :, sl], cos, sin).astype(BF16)
    ts = vs_ref.shape[1]
    for v_ref, vo_ref in ((vs_ref, vso_ref), (vw_ref, vwo_ref)):
        ck = vo_ref.shape[4]
        for g in range(NSA_KV_HEADS):
            for cc in range(ts // ck):
                parts = [v_ref[0, cc * ck + r:cc * ck + r + LANES, g * HEAD_DIM:(g + 1) * HEAD_DIM].T
                         for r in range(0, ck, LANES)]
                vo_ref[0, g, cc] = jnp.concatenate(parts, axis=1).astype(BF16)


def _rope_qkv(proj, ang, *, ts, slc_chunk, win_chunk):
    bsz, s, _ = proj.shape
    qw = NSA_HEADS * HEAD_DIM
    kvw = NSA_KV_HEADS * HEAD_DIM
    kv_spec = lambda blk: pl.BlockSpec((1, ts, kvw), lambda b, i, blk=blk: (b, i, blk))
    kv_out = pl.BlockSpec((1, ts, kvw), lambda b, i: (b, i, 0))
    kv_shape = jax.ShapeDtypeStruct((bsz, s, kvw), BF16)
    vt_out = lambda ck: pl.BlockSpec((1, NSA_KV_HEADS, ts // ck, HEAD_DIM, ck), lambda b, i: (b, 0, i, 0, 0))
    vt_shape = lambda ck: jax.ShapeDtypeStruct((bsz, NSA_KV_HEADS, s // ck, HEAD_DIM, ck), BF16)
    base = qw // kvw
    return pl.pallas_call(
        _rope_kernel,
        grid=(bsz, s // ts),
        in_specs=[pl.BlockSpec((1, ts, HEAD_DIM), lambda b, i: (b, i, 0)),
                  pl.BlockSpec((1, ts, qw), lambda b, i: (b, i, 0)),
                  kv_spec(base + 2), kv_spec(base + 4), kv_spec(base + 3), kv_spec(base + 5)],
        out_specs=[pl.BlockSpec((1, ts, qw), lambda b, i: (b, i, 0)), kv_out, kv_out,
                   vt_out(slc_chunk), vt_out(win_chunk)],
        out_shape=[jax.ShapeDtypeStruct((bsz, s, qw), BF16), kv_shape, kv_shape,
                   vt_shape(slc_chunk), vt_shape(win_chunk)],
        compiler_params=_params("parallel", "parallel"),
        name="rope_qkv",
    )(ang, proj, proj, proj, proj, proj)


def _gelu_tanh(x):
    return x * (0.5 * (1.0 + jnp.tanh(math.sqrt(2.0 / math.pi) * (x + 0.044715 * (x * x * x)))))


def _compress_kernel(x_ref, w1_ref, w2_ref, pos_ref, ang_ref, o_ref, *, rope):
    nsub = x_ref.shape[1] // CMP_STRIDE
    hid = w1_ref.shape[1]
    pa = jnp.zeros((nsub, hid), F32)
    pb = jnp.zeros((nsub, hid), F32)
    for l in range(CMP_STRIDE):
        xl = x_ref[0, pl.ds(l, nsub, stride=CMP_STRIDE), :].astype(BF16)
        pa = pa + _dot(xl, w1_ref[l * HEAD_DIM:(l + 1) * HEAD_DIM, :])
        pb = pb + _dot(xl, w1_ref[(CMP_STRIDE + l) * HEAD_DIM:(CMP_STRIDE + l + 1) * HEAD_DIM, :])
    bias = _dot(pos_ref[...], w1_ref[...])[0:1, :]
    hpre = pa + pltpu.roll(pb, nsub - 1, axis=0) + bias
    out = _dot(_gelu_tanh(hpre).astype(BF16), w2_ref[...])
    if rope:
        cos, sin = _rope_tables(ang_ref[0])
        out = _rope(out, cos, sin)
    o_ref[0, 0] = out.astype(o_ref.dtype)


def _compress(proj, col0, pos_emb, w1, w2, ang_cmp, *, rope):
    bsz, s, _ = proj.shape
    nsub = s // CMP_STRIDE
    blk0 = col0 // HEAD_DIM
    hid = w1.shape[1]
    pos = jnp.zeros((8, CMP_BLOCK * HEAD_DIM), BF16).at[0].set(pos_emb.reshape(-1).astype(BF16))
    return pl.pallas_call(
        functools.partial(_compress_kernel, rope=rope),
        grid=(bsz, NSA_KV_HEADS),
        in_specs=[pl.BlockSpec((1, s, HEAD_DIM), lambda b, g: (b, 0, g + blk0)),
                  pl.BlockSpec((CMP_BLOCK * HEAD_DIM, hid), lambda b, g: (0, 0)),
                  pl.BlockSpec((hid, HEAD_DIM), lambda b, g: (0, 0)),
                  pl.BlockSpec((8, CMP_BLOCK * HEAD_DIM), lambda b, g: (0, 0)),
                  pl.BlockSpec((1, nsub, HEAD_DIM), lambda b, g: (b, 0, 0))],
        out_specs=pl.BlockSpec((1, 1, nsub, HEAD_DIM), lambda b, g: (b, g, 0, 0)),
        out_shape=jax.ShapeDtypeStruct((bsz, NSA_KV_HEADS, nsub, HEAD_DIM), BF16),
        compiler_params=_params("parallel", "parallel"),
        name="compress",
    )(proj, w1.astype(BF16), w2.astype(BF16), pos, ang_cmp)


def _cmp_attn_kernel(q_ref, kc_ref, vc_ref, smat_ref, o_ref, sel_ref):
    tq = q_ref.shape[1]
    ncol = kc_ref.shape[2]
    t = pl.program_id(2) * tq + lax.broadcasted_iota(jnp.int32, (tq, ncol), 0)
    n = lax.broadcasted_iota(jnp.int32, (tq, ncol), 1)
    valid = (n * CMP_STRIDE + CMP_BLOCK - 1) <= t
    kc = kc_ref[0, 0]
    vc = vc_ref[0, 0]
    p_grp = jnp.zeros((tq, ncol), F32)
    for hh in range(HPG):
        sl = slice(hh * HEAD_DIM, (hh + 1) * HEAD_DIM)
        sc = jnp.where(valid, _dot_nt(q_ref[0, :, sl], kc), NEG_INF)
        e = jnp.exp2(sc - jnp.max(sc, axis=-1, keepdims=True))
        p = jnp.where(valid, e / jnp.sum(e, axis=-1, keepdims=True), 0.0)
        o_ref[0, :, sl] = _dot(p.astype(BF16), vc)
        p_grp = p_grp + p
    score = _dot_hi(p_grp, smat_ref[...])
    n_sel = sel_ref.shape[2]
    score = jnp.concatenate([score[r:r + LANES].T for r in range(0, tq, LANES)], axis=1)[:n_sel]
    n = lax.broadcasted_iota(jnp.int32, (n_sel, tq), 0)
    t = pl.program_id(2) * tq + lax.broadcasted_iota(jnp.int32, (n_sel, tq), 1)
    cur = t >> int(math.log2(SEL_BLOCK))
    forced = (n == 0) | (n == cur) | (n == cur - 1)
    future = n * SEL_BLOCK > t
    score = jnp.where(forced, jnp.inf, jnp.where(future, -jnp.inf, score))
    rank = jnp.zeros((n_sel, tq), jnp.int32)
    for kk in range(n_sel):
        ck = score[kk:kk + 1, :]
        ahead = (ck > score) | ((ck == score) & (kk < n))
        rank = rank + ahead.astype(jnp.int32)
    sel_ref[0, 0] = (rank < N_SELECT).astype(sel_ref.dtype)


def _sel_matrix(ncol, n_sel):
    rs = SEL_BLOCK // CMP_STRIDE
    rc = CMP_BLOCK // CMP_STRIDE
    mat = [[0.0] * ncol for _ in range(ncol)]
    for j in range(n_sel):
        for m in range(rs):
            for n in range(rc):
                i = rs * j + m + n - (rc - 1)
                if 0 <= i < ncol - 1:
                    mat[i][j] += 1.0
    return jnp.array(mat, F32)


def _cmp_attention(q_r, k_cmp, v_cmp, *, tq):
    bsz, s, _ = q_r.shape
    ncol = k_cmp.shape[2]
    n_sel = s // SEL_BLOCK
    gw = HPG * HEAD_DIM
    return pl.pallas_call(
        _cmp_attn_kernel,
        grid=(bsz, NSA_KV_HEADS, s // tq),
        in_specs=[pl.BlockSpec((1, tq, gw), lambda b, g, i: (b, i, g)),
                  pl.BlockSpec((1, 1, ncol, HEAD_DIM), lambda b, g, i: (b, g, 0, 0)),
                  pl.BlockSpec((1, 1, ncol, HEAD_DIM), lambda b, g, i: (b, g, 0, 0)),
                  pl.BlockSpec((ncol, ncol), lambda b, g, i: (0, 0))],
        out_specs=[pl.BlockSpec((1, tq, gw), lambda b, g, i: (b, i, g)),
                   pl.BlockSpec((1, 1, n_sel, tq), lambda b, g, i: (b, g, 0, i))],
        out_shape=[jax.ShapeDtypeStruct((bsz, s, NSA_HEADS * HEAD_DIM), F32),
                   jax.ShapeDtypeStruct((bsz, NSA_KV_HEADS, n_sel, s), F32)],
        compiler_params=_params("parallel", "parallel", "parallel"),
        name="cmp_attention",
    )(q_r, k_cmp, v_cmp, _sel_matrix(ncol, n_sel))


def _softmax_merge_t(parts):
    if len(parts) == 1:
        return parts[0][0] / parts[0][2]
    m = functools.reduce(jnp.maximum, [p[1] for p in parts])
    scale = [jnp.exp2(p[1] - m) for p in parts]
    acc = functools.reduce(lambda x, y: x + y, [p[0] * s for p, s in zip(parts, scale)])
    l = functools.reduce(lambda x, y: x + y, [p[2] * s for p, s in zip(parts, scale)])
    return acc / l


def _softmax_partial_t(scores, values_t):
    r = scores[0].shape[1]
    m8 = None
    for sc in scores:
        c8 = jnp.max(sc.reshape(-1, 8, r), axis=0)
        m8 = c8 if m8 is None else jnp.maximum(m8, c8)
    m = jnp.max(m8, axis=0, keepdims=True)
    l8 = None
    probs = []
    for sc in scores:
        p = jnp.exp2(sc - m)
        p8 = jnp.sum(p.reshape(-1, 8, r), axis=0)
        l8 = p8 if l8 is None else l8 + p8
        probs.append(p.astype(BF16))
    l = jnp.sum(l8, axis=0, keepdims=True)
    acc = _dot(jnp.concatenate(values_t, axis=1), jnp.concatenate(probs, axis=0))
    return acc, m, l


MASK_BIG = 2.0 ** 100


def _slc_win_kernel(q_ref, ks_ref, vst_ref, kw_ref, vwt_ref, selt_ref, blk_ref, oc_ref, gate_ref, o_ref,
                    gt_s, os_s, ow_s):
    tq = vwt_ref.shape[4]
    tk = vst_ref.shape[4]
    n_sub = q_ref.shape[1] // tq
    n_chunks = vst_ref.shape[2]
    n_sel = selt_ref.shape[2]
    g = pl.program_id(1)
    t_blk = pl.program_id(2) * (n_sub * tq)

    def tile_heads(mask):
        return jnp.concatenate([mask] * HPG, axis=1)

    key_loc = lax.broadcasted_iota(jnp.int32, (tq, tq), 0)
    qry_loc = lax.broadcasted_iota(jnp.int32, (tq, tq), 1)
    k_loc = lax.broadcasted_iota(jnp.int32, (tk, tq), 0)
    t_loc = lax.broadcasted_iota(jnp.int32, (tk, tq), 1)
    n_win = WINDOW // tq + 1
    q_ts, q_augs = [], []
    for sub in range(n_sub):
        rows = slice(sub * tq, (sub + 1) * tq)
        q_t = jnp.concatenate([q_ref[0, rows, hh * HEAD_DIM:(hh + 1) * HEAD_DIM].astype(F32).T.astype(BF16)
                               for hh in range(HPG)], axis=1)
        sel_bias = ((selt_ref[0, 0, :, rows] - 1.0) * MASK_BIG).astype(BF16)
        q_ts.append(q_t)
        q_augs.append(jnp.concatenate([q_t, tile_heads(sel_bias),
                                       jnp.zeros((HEAD_DIM - n_sel, HPG * tq), BF16)], axis=0))

    def window_scores(in_range, sub):
        t0 = t_blk + sub * tq
        if in_range:
            k0 = pl.multiple_of(t0 - WINDOW, tq)
            sc = _dot(kw_ref[0, pl.ds(k0, n_win * tq), :], q_ts[sub])
            scores = [sc[cc * tq:(cc + 1) * tq] for cc in range(n_win)]
            scores[0] = jnp.where(tile_heads(key_loc > qry_loc), scores[0], NEG_INF)
            scores[-1] = jnp.where(tile_heads(key_loc <= qry_loc), scores[-1], NEG_INF)
            return scores, [vwt_ref[0, 0, k0 // tq + cc] for cc in range(n_win)]
        sc = _dot(kw_ref[0, 0:WINDOW, :], q_ts[sub])
        scores = [jnp.where(tile_heads(cc * tq + key_loc <= t0 + qry_loc), sc[cc * tq:(cc + 1) * tq], NEG_INF)
                  for cc in range(WINDOW // tq)]
        return scores, [vwt_ref[0, 0, cc] for cc in range(WINDOW // tq)]

    def block_variant(n_used):
        n_groups = min(n_used, 2)
        bounds = [(n_used * gg) // n_groups for gg in range(n_groups + 1)]
        sel_groups, win = [], []
        for sub in range(n_sub):
            groups = []
            for lo, hi in zip(bounds[:-1], bounds[1:]):
                rows = slice(lo * tk, hi * tk)
                sc = _dot(jnp.concatenate([ks_ref[0, rows, :], blk_ref[rows, :]], axis=1), q_augs[sub])
                groups.append([sc[cc * tk:(cc + 1) * tk] for cc in range(hi - lo)])
            causal = ((n_used - 1) * tk + k_loc) <= (t_blk + sub * tq + t_loc)
            groups[-1][-1] = jnp.where(tile_heads(causal), groups[-1][-1], NEG_INF)
            sel_groups.append(groups)
            win.append(window_scores((n_used - 1) * tk >= WINDOW, sub))
        values = [[vst_ref[0, 0, cc] for cc in range(bounds[gg], bounds[gg + 1])] for gg in range(n_groups)]
        parts = [[_softmax_partial_t(sel_groups[sub][0], values[0])] for sub in range(n_sub)]
        for sub in range(n_sub):
            ow_s[sub] = _softmax_merge_t([_softmax_partial_t(*win[sub])])
        for gg in range(1, n_groups):
            for sub in range(n_sub):
                parts[sub].append(_softmax_partial_t(sel_groups[sub][gg], values[gg]))
        for sub in range(n_sub):
            os_s[sub] = _softmax_merge_t(parts[sub])

    for vv in range(n_chunks):
        pl.when(t_blk // tk == vv)(functools.partial(block_variant, vv + 1))

    for sub in range(n_sub):
        rows = slice(sub * tq, (sub + 1) * tq)
        gt = _sigmoid(gate_ref[0, rows, :])
        gt_s[sub] = gt.T
        lane = lax.broadcasted_iota(jnp.int32, gt.shape, 1)
        for hh in range(HPG):
            base = (g * HPG + hh) * 3
            g_cmp = jnp.sum(jnp.where(lane == base, gt, 0.0), axis=1, keepdims=True)
            cs = slice(hh * tq, (hh + 1) * tq)
            mix_t = (gt_s[sub, pl.ds(base + 1, 1), :] * os_s[sub, :, cs]
                     + gt_s[sub, pl.ds(base + 2, 1), :] * ow_s[sub, :, cs])
            sl = slice(hh * HEAD_DIM, (hh + 1) * HEAD_DIM)
            o_ref[0, rows, sl] = (g_cmp * oc_ref[0, rows, sl] + mix_t.T).astype(o_ref.dtype)


def _slc_win_attention(q_r, ks, vs_t, kw, vw_t, sel_t, o_cmp, gates):
    bsz, s, _ = q_r.shape
    n_sel = s // SEL_BLOCK
    gw = HPG * HEAD_DIM
    tb = vs_t.shape[4]
    tq = vw_t.shape[4]
    assert tq == LANES and tb % tq == 0 and WINDOW % tb == 0
    n_sub = tb // tq
    kv_spec = pl.BlockSpec((1, s, HEAD_DIM), lambda b, g, i: (b, 0, g))
    vt_spec = lambda a: pl.BlockSpec((1, 1) + a.shape[2:], lambda b, g, i: (b, g, 0, 0, 0))
    block_onehot = (jnp.arange(s)[:, None] // SEL_BLOCK == jnp.arange(HEAD_DIM)[None, :]).astype(BF16)
    return pl.pallas_call(
        _slc_win_kernel,
        grid=(bsz, NSA_KV_HEADS, s // tb),
        in_specs=[pl.BlockSpec((1, tb, gw), lambda b, g, i: (b, i, g)),
                  kv_spec, vt_spec(vs_t), kv_spec, vt_spec(vw_t),
                  pl.BlockSpec((1, 1, n_sel, tb), lambda b, g, i: (b, g, 0, i)),
                  pl.BlockSpec((s, HEAD_DIM), lambda b, g, i: (0, 0)),
                  pl.BlockSpec((1, tb, gw), lambda b, g, i: (b, i, g)),
                  pl.BlockSpec((1, tb, LANES), lambda b, g, i: (b, i, 0))],
        out_specs=pl.BlockSpec((1, tb, gw), lambda b, g, i: (b, i, g)),
        out_shape=jax.ShapeDtypeStruct((bsz, s, NSA_HEADS * HEAD_DIM), BF16),
        scratch_shapes=[pltpu.VMEM((n_sub, LANES, tq), F32), pltpu.VMEM((n_sub, HEAD_DIM, HPG * tq), F32),
                        pltpu.VMEM((n_sub, HEAD_DIM, HPG * tq), F32)],
        compiler_params=_params("parallel", "parallel", "arbitrary"),
        name="slc_win_attention",
    )(q_r, ks, vs_t, kw, vw_t, sel_t, block_onehot, o_cmp, gates)


def _pad_cols(w, n):
    return jnp.pad(w, ((0, 0), (0, n - w.shape[1])))


def _conv_deltanet_mixer(xb, bsz, s, w_in, sc_conv_w, dn_conv_w, a_log, dt_bias, norm_w, w_out):
    sc_w = sc_conv_w.shape[1]
    dn_w = dn_conv_w.shape[1] // 3
    n_heads = dn_w // HEAD_DIM
    main = 3 * sc_w + 4 * dn_w
    proj, ba = _in_proj(xb, w_in.astype(BF16), main, _pad_cols(w_in[:, main:], LANES).astype(BF16),
                        tm=1024, tn=1024)
    proj = proj.reshape(bsz, s, main)
    y_sc = _short_conv(proj, sc_conv_w, sc_w, tc=256)
    y_dn = _deltanet(proj, 3 * sc_w, 3 * sc_w + 3 * dn_w, dn_conv_w, ba.reshape(bsz, s, LANES), a_log, dt_bias,
                     norm_w, n_heads)
    wo = w_out.astype(BF16)
    return [y_sc.reshape(bsz * s, sc_w), y_dn.reshape(bsz * s, dn_w)], [wo[:sc_w], wo[sc_w:]]


def _nsa_mixer(xb, bsz, s, positions, w_in, cmp_pos_k, cmp_w1_k, cmp_w2_k, cmp_pos_v, cmp_w1_v, cmp_w2_v, w_out):
    qw = NSA_HEADS * HEAD_DIM
    kvw = NSA_KV_HEADS * HEAD_DIM
    main = qw + 6 * kvw
    proj, gates = _in_proj(xb, w_in.astype(BF16), main, _pad_cols(w_in[:, main:], LANES).astype(BF16),
                           tm=1024, tn=1024)
    proj = proj.reshape(bsz, s, main)
    half = HEAD_DIM // 2
    inv = jnp.power(ROPE_THETA, -jnp.arange(half, dtype=F32) / half)
    inv = jnp.concatenate([inv, inv])
    ang = positions.astype(F32)[..., None] * inv
    cmp_end = jnp.minimum(jnp.arange(s // CMP_STRIDE) * CMP_STRIDE + CMP_BLOCK - 1, s - 1)
    ang_cmp = positions[:, cmp_end].astype(F32)[..., None] * inv
    q_r, ks, kw, vs_t, vw_t = _rope_qkv(proj, ang, ts=512, slc_chunk=256, win_chunk=LANES)
    k_cmp = _compress(proj, qw, cmp_pos_k, cmp_w1_k, cmp_w2_k, ang_cmp, rope=True)
    v_cmp = _compress(proj, qw + kvw, cmp_pos_v, cmp_w1_v, cmp_w2_v, ang_cmp, rope=False)
    o_cmp, sel_t = _cmp_attention(q_r, k_cmp, v_cmp, tq=s)
    o = _slc_win_attention(q_r, ks, vs_t, kw, vw_t, sel_t, o_cmp, gates.reshape(bsz, s, LANES))
    return [o.reshape(bsz * s, qw)], [w_out.astype(BF16)]


def kernel(x, positions, ln_mix_g, ln_mix_b, ln_ffn_g, ln_ffn_b, ffn_w_in, ffn_w_out, hy_w_in, sc_conv_w, dn_conv_w, dn_a_log, dn_dt_bias, dn_norm_w, hy_w_out, nsa_w_in, cmp_pos_k, cmp_w1_k, cmp_w2_k, cmp_pos_v, cmp_w1_v, cmp_w2_v, nsa_w_out):
    bsz, s, d = x.shape
    xf = x.reshape(bsz * s, d)
    xb = xf
    ffn_w_out_b = ffn_w_out.astype(BF16)
    for i in range(DEPTH):
        j = i // 2
        if i % 2 == 0:
            ys, wos = _conv_deltanet_mixer(xb, bsz, s, hy_w_in[j], sc_conv_w[j], dn_conv_w[j], dn_a_log[j],
                                           dn_dt_bias[j], dn_norm_w[j], hy_w_out[j])
        else:
            ys, wos = _nsa_mixer(xb, bsz, s, positions, nsa_w_in[j], cmp_pos_k[j], cmp_w1_k[j], cmp_w2_k[j],
                                 cmp_pos_v[j], cmp_w1_v[j], cmp_w2_v[j], nsa_w_out[j])
        xf, xb = _matmul_ln(ys, wos, xf, ln_mix_g[i], ln_mix_b[i], tm=512, tn=d)
        hmid = _ffn_in(xb, ffn_w_in, i, tm=1024, tn=512)
        xf, xb = _matmul_ln([hmid], [ffn_w_out_b], xf, ln_ffn_g[i], ln_ffn_b[i], tm=512, tn=512, layer=i)
    return xf.reshape(bsz, s, d)
```

```python
import functools
import math

import jax
import jax.numpy as jnp
from jax import lax
from jax.experimental import pallas as pl
from jax.experimental.pallas import tpu as pltpu

F32 = jnp.float32
BF16 = jnp.bfloat16
HIGHEST = lax.Precision.HIGHEST

LANES = 128
VMEM_LIMIT = 48 * 1024 * 1024
VMEM_LIMIT_BIG = 56 * 1024 * 1024

DN_HEADS = 8
DN_CHUNK = 64
DN_CONV = 4
SC_KERNEL = 3
NSA_HEADS = 16
NSA_KV_HEADS = 4
HPG = NSA_HEADS // NSA_KV_HEADS
HEAD_DIM = 128
CMP_BLOCK = 32
CMP_STRIDE = 16
SEL_BLOCK = 64
N_SELECT = 16
WINDOW = 512
ROPE_THETA = 10000.0
LN_EPS = 1e-5
NORM_EPS = 1e-6
NEG_INF = -1e30
DEPTH = 2
ALPHA = (2 * DEPTH) ** 0.25
ATTN_SCALE = HEAD_DIM ** -0.5
Q_SCALE = ATTN_SCALE * math.log2(math.e)


def _params(*sem, vmem=VMEM_LIMIT):
    return pltpu.CompilerParams(dimension_semantics=sem, vmem_limit_bytes=vmem)


def _sigmoid(x):
    return 1.0 / (1.0 + jnp.exp(-x))


def _silu(x):
    return x * _sigmoid(x)


def _dot(a, b):
    return jnp.dot(a, b, preferred_element_type=F32)


def _dot_nt(a, b):
    return lax.dot_general(a, b, (((1,), (1,)), ((), ())), preferred_element_type=F32)


def _dot_hi(a, b):
    return jnp.dot(a, b, precision=HIGHEST, preferred_element_type=F32)


def _proj_kernel(x_ref, w_ref, ws_ref, o_ref, os_ref):
    xb = x_ref[...].astype(BF16)
    o_ref[...] = _dot(xb, w_ref[...])

    @pl.when(pl.program_id(1) == 0)
    def _():
        os_ref[...] = _dot(xb, ws_ref[...])


def _in_proj(x, w, n, w_side, *, tm, tn):
    m, k = x.shape
    ns = w_side.shape[1]
    return pl.pallas_call(
        _proj_kernel,
        grid=(m // tm, n // tn),
        in_specs=[pl.BlockSpec((tm, k), lambda i, j: (i, 0)),
                  pl.BlockSpec((k, tn), lambda i, j: (0, j)),
                  pl.BlockSpec((k, ns), lambda i, j: (0, 0))],
        out_specs=[pl.BlockSpec((tm, tn), lambda i, j: (i, j)),
                   pl.BlockSpec((tm, ns), lambda i, j: (i, 0))],
        out_shape=[jax.ShapeDtypeStruct((m, n), F32), jax.ShapeDtypeStruct((m, ns), F32)],
        compiler_params=_params("parallel", "arbitrary"),
        name="in_proj",
    )(x, w, w_side)


def _ffn_in_kernel(x_ref, wg_ref, wu_ref, o_ref, wgb_s, wub_s):
    @pl.when(pl.program_id(1) == 0)
    def _():
        wgb_s[...] = wg_ref[...].astype(BF16)
        wub_s[...] = wu_ref[...].astype(BF16)

    x = x_ref[...]
    gate = _dot(x, wgb_s[...])
    up = _dot(x, wub_s[...])
    o_ref[...] = (_silu(gate) * up).astype(o_ref.dtype)


def _ffn_in(xb, w_in, layer, *, tm, tn):
    m, k = xb.shape
    hidden = w_in.shape[2] // 2
    nj = hidden // tn
    return pl.pallas_call(
        _ffn_in_kernel,
        grid=(nj, m // tm),
        in_specs=[pl.BlockSpec((tm, k), lambda j, i: (i, 0)),
                  pl.BlockSpec((None, k, tn), lambda j, i: (layer, 0, j)),
                  pl.BlockSpec((None, k, tn), lambda j, i: (layer, 0, j + nj))],
        out_specs=pl.BlockSpec((tm, tn), lambda j, i: (i, j)),
        out_shape=jax.ShapeDtypeStruct((m, hidden), BF16),
        scratch_shapes=[pltpu.VMEM((k, tn), BF16), pltpu.VMEM((k, tn), BF16)],
        compiler_params=_params("parallel", "arbitrary"),
        name="ffn_in",
    )(xb, w_in, w_in)


MM_LN_ROWS = 128


def _mm_ln_kernel(*refs, n_pairs, nj):
    xs = refs[:n_pairs]
    ws = refs[n_pairs:2 * n_pairs]
    r_ref, g_ref, b_ref, o_ref, ob_ref, y_s = refs[2 * n_pairs:2 * n_pairs + 6]
    tm = o_ref.shape[0]
    j = pl.program_id(1)
    part = _dot(xs[0][...], ws[0][...])
    for x_ref, w_ref in zip(xs[1:], ws[1:]):
        part = part + _dot(x_ref[...], w_ref[...])
    y_s[j] = part

    @pl.when(j == nj - 1)
    def _():
        for r0 in range(0, tm, MM_LN_ROWS):
            rows = slice(r0, r0 + MM_LN_ROWS)
            y = jnp.concatenate([y_s[jj, rows, :] for jj in range(nj)], axis=1)
            v = ALPHA * r_ref[rows, :] + y
            mu = jnp.mean(v, axis=-1, keepdims=True)
            dv = v - mu
            var = jnp.mean(dv * dv, axis=-1, keepdims=True)
            out = dv * lax.rsqrt(var + LN_EPS) * g_ref[...] + b_ref[...]
            o_ref[rows, :] = out
            ob_ref[rows, :] = out.astype(BF16)


def _matmul_ln(xs, ws, resid, g, b, *, tm, tn, layer=None):
    m, d = resid.shape
    n_pairs = len(xs)
    nj = d // tn
    ks = [x.shape[1] for x in xs]
    if layer is None:
        w_specs = [pl.BlockSpec((k, tn), lambda i, j: (0, j)) for k in ks]
    else:
        w_specs = [pl.BlockSpec((None, k, tn), lambda i, j: (layer, 0, j)) for k in ks]
    in_specs = ([pl.BlockSpec((tm, k), lambda i, j: (i, 0)) for k in ks]
                + w_specs
                + [pl.BlockSpec((tm, d), lambda i, j: (i, 0)),
                   pl.BlockSpec((1, d), lambda i, j: (0, 0)),
                   pl.BlockSpec((1, d), lambda i, j: (0, 0))])
    return pl.pallas_call(
        functools.partial(_mm_ln_kernel, n_pairs=n_pairs, nj=nj),
        grid=(m // tm, nj),
        in_specs=in_specs,
        out_specs=[pl.BlockSpec((tm, d), lambda i, j: (i, 0)),
                   pl.BlockSpec((tm, d), lambda i, j: (i, 0))],
        out_shape=[jax.ShapeDtypeStruct((m, d), F32), jax.ShapeDtypeStruct((m, d), BF16)],
        scratch_shapes=[pltpu.VMEM((nj, tm, tn), F32)],
        compiler_params=_params("parallel", "arbitrary", vmem=VMEM_LIMIT_BIG),
        name="matmul_ln",
    )(*xs, *ws, resid, g.reshape(1, d), b.reshape(1, d))


def _causal_conv(u, w_ref, taps):
    def tap_sum(x, shift):
        acc = x * w_ref[taps - 1:taps, :]
        for sh in range(1, taps):
            acc = acc + shift(x, sh) * w_ref[taps - 1 - sh:taps - sh, :]
        return acc

    body = tap_sum(u, lambda x, sh: pltpu.roll(x, sh, axis=0))
    row = lax.broadcasted_iota(jnp.int32, (8, u.shape[1]), 0)
    head = tap_sum(u[0:8], lambda x, sh: jnp.where(row >= sh, pltpu.roll(x, sh, axis=0), 0.0))
    return jnp.concatenate([head, body[8:]], axis=0)


def _sc_kernel(b_ref, c_ref, h_ref, w_ref, o_ref):
    u = c_ref[0] * h_ref[0]
    o_ref[0] = (b_ref[0] * _causal_conv(u, w_ref, SC_KERNEL)).astype(o_ref.dtype)


def _short_conv(proj, conv_w, width, *, tc):
    bsz, s, _ = proj.shape
    nb = width // tc
    w = jnp.zeros((8, width), F32).at[:SC_KERNEL].set(conv_w)
    return pl.pallas_call(
        _sc_kernel,
        grid=(bsz, nb),
        in_specs=[pl.BlockSpec((1, s, tc), lambda b, j: (b, 0, j)),
                  pl.BlockSpec((1, s, tc), lambda b, j: (b, 0, j + nb)),
                  pl.BlockSpec((1, s, tc), lambda b, j: (b, 0, j + 2 * nb)),
                  pl.BlockSpec((8, tc), lambda b, j: (0, j))],
        out_specs=pl.BlockSpec((1, s, tc), lambda b, j: (b, 0, j)),
        out_shape=jax.ShapeDtypeStruct((bsz, s, width), BF16),
        compiler_params=_params("parallel", "parallel"),
        name="short_conv",
    )(proj, proj, proj, w)


def _split(x):
    hi = x.astype(BF16)
    return hi, (x - hi.astype(F32)).astype(BF16)


def _dotb(a, b):
    return _dot(a.astype(BF16), b.astype(BF16))


DN_GROUP = 16
DN_HEADS_PER_STEP = 2


def _dn_group_local(base, scr, masks, out):
    q_s, k_s, kb_s, qd_s, kf_s, kbe_s, vb_s, gc_s, gcd_s = scr
    incl, strict, m8, m16, eye = masks
    c = DN_CHUNK
    idx = range(DN_HEADS_PER_STEP * DN_GROUP)
    rows = [(e // DN_GROUP, pl.ds(base + (e % DN_GROUP) * c, c)) for e in idx]
    gc = [gc_s[hh, r, :] for hh, r in rows]
    decay = []
    for e in idx:
        cc = e % DN_GROUP
        gc_j = gcd_s[e // DN_GROUP, pl.ds(base // LANES + cc // 2, 1), (cc % 2) * c:(cc % 2) * c + c]
        decay.append(jnp.where(incl, jnp.exp(jnp.where(incl, gc[e][:, :c] - gc_j, 0.0)), 0.0))
    kbf = [k_s[hh, r, :] for hh, r in rows]
    kk = [_dot_nt(kb_s[rows[cc][0], rows[cc][1], :], kbf[cc]) for cc in idx]
    qk = [_dot_nt(q_s[rows[cc][0], rows[cc][1], :], kbf[cc]) for cc in idx]
    yield
    a = [jnp.where(strict, kk[cc] * decay[cc], 0.0) for cc in idx]
    intra = [(qk[cc] * decay[cc]).astype(BF16) for cc in idx]
    ad = [jnp.where(m8, x, 0.0) for x in a]
    adb = [x.astype(BF16) for x in ad]
    a2 = [_dot(x, x) for x in adb]
    yield
    a2b = [x.astype(BF16) for x in a2]
    p = [eye - x for x in ad]
    p1, a4 = [], []
    for cc in idx:
        p1.append(p[cc] + _dotb(p[cc], a2b[cc]))
        a4.append(_dot(a2b[cc], a2b[cc]))
    yield
    p2 = [p1[cc] + _dotb(p1[cc], a4[cc]) for cc in idx]
    yield
    pb = [x.astype(BF16) for x in p2]
    t = [_dotb(pb[cc], jnp.where(m16, a[cc] - ad[cc], 0.0)) for cc in idx]
    yield
    dinv = [(p2[cc] - _dotb(t[cc], pb[cc])).astype(BF16) for cc in idx]
    yield
    db, da = [], []
    for cc in idx:
        hh, r = rows[cc]
        rhs = jnp.concatenate([vb_s[hh, r, :], kbe_s[hh, r, :]], axis=1)
        db.append(_dotb(dinv[cc], rhs))
        da.append(_dotb(dinv[cc], jnp.where(m16, 0.0, a[cc])).astype(BF16))
    yield
    blocks = [[x[0:16]] for x in db]
    for s4 in range(1, c // 16):
        rs = slice(16 * s4, 16 * s4 + 16)
        for cc in idx:
            xprev = jnp.concatenate(blocks[cc] + [jnp.zeros((c - 16 * s4, 2 * HEAD_DIM), F32)], axis=0)
            blocks[cc].append(db[cc][rs] - _dotb(da[cc][rs], xprev))
        yield
    for cc in idx:
        sol = jnp.concatenate(blocks[cc], axis=0)
        sol_hi, sol_lo = _split(sol)
        hh, r = rows[cc]
        g_last = gc[cc][c - 1:c, :]
        k_dec_t = (kf_s[hh, r, :] * jnp.exp(g_last - gc[cc])).T.astype(BF16)
        kw = _dot(k_dec_t, sol_hi) + _dot(k_dec_t, sol_lo)
        iw = _dot(intra[cc], sol_hi) + _dot(intra[cc], sol_lo)
        out.append((kw[:, HEAD_DIM:].astype(BF16), kw[:, :HEAD_DIM],
                    (qd_s[hh, r, :] - iw[:, HEAD_DIM:]).astype(BF16), iw[:, :HEAD_DIM], jnp.exp(g_last)))
    yield


def _dn_chunk_seq(state, loc, z, nw):
    w2, n_mat, qp, op, eg_last = loc
    sb = state.astype(BF16)
    o = _dot(qp, sb) + op
    state = (state * eg_last - _dot(w2, sb)) + n_mat
    o = o * lax.rsqrt(jnp.mean(o * o, axis=-1, keepdims=True) + NORM_EPS) * nw * _silu(z)
    return state, o


def _dn_kernel(qp_ref, kp_ref, vp_ref, z_ref, ba_ref, arow_ref, dtrow_ref, cwq_ref, cwk_ref, cwv_ref, nw_ref, o_ref,
               q_s, k_s, kb_s, qd_s, kf_s, kbe_s, vb_s, gc_s, beta_s, gates_s, gcd_s, *, n_heads):
    hb = DN_HEADS_PER_STEP
    s = qp_ref.shape[1]
    c = DN_CHUNK
    @pl.when(pl.program_id(1) == 0)
    def _():
        ba = ba_ref[0]
        xa = ba + dtrow_ref[...]
        softplus = jnp.maximum(xa, 0.0) + jnp.log(1.0 + jnp.exp(-jnp.abs(xa)))
        lane0 = lax.broadcasted_iota(jnp.int32, ba.shape, 1)
        gates = jnp.where(lane0 < n_heads, _sigmoid(ba), -jnp.exp(arow_ref[...]) * softplus)
        for blk in range(s // LANES):
            rs = slice(blk * LANES, (blk + 1) * LANES)
            gates_s[rs, :] = gates[rs, :].T

    n_blk = s // LANES
    pos = lax.broadcasted_iota(jnp.int32, (n_blk, LANES), 1) & (c - 1)
    for hh in range(hb):
        h = pl.program_id(1) * hb + hh
        cols = slice(hh * HEAD_DIM, (hh + 1) * HEAD_DIM)
        beta_d = jnp.concatenate([gates_s[pl.ds(blk * LANES + h, 1), :] for blk in range(n_blk)], axis=0)
        gc_d = jnp.concatenate([gates_s[pl.ds(blk * LANES + h + n_heads, 1), :] for blk in range(n_blk)], axis=0)
        sh = 1
        while sh < c:
            gc_d = gc_d + jnp.where(pos >= sh, pltpu.roll(gc_d, sh, axis=1), 0.0)
            sh *= 2
        gcd_s[hh, 0:n_blk, :] = gc_d
        for blk in range(n_blk):
            rs = slice(blk * LANES, (blk + 1) * LANES)
            gc_s[hh, rs, :] = jnp.broadcast_to(gc_d[blk:blk + 1, :], (LANES, LANES)).T
            beta_s[hh, rs, :] = jnp.broadcast_to(beta_d[blk:blk + 1, :], (LANES, LANES)).T
        gc = gc_s[hh]
        beta = beta_s[hh]
        eg = jnp.exp(gc)
        q = _silu(_causal_conv(qp_ref[0, :, cols], cwq_ref[:, cols], DN_CONV))
        q = q * (lax.rsqrt(jnp.sum(q * q, axis=-1, keepdims=True) + NORM_EPS) * (HEAD_DIM ** -0.5))
        q_s[hh] = q.astype(BF16)
        qd_s[hh] = q * eg
        k = _silu(_causal_conv(kp_ref[0, :, cols], cwk_ref[:, cols], DN_CONV))
        k = k * lax.rsqrt(jnp.sum(k * k, axis=-1, keepdims=True) + NORM_EPS)
        kb = k * beta
        kf_s[hh] = k
        k_s[hh] = k.astype(BF16)
        kb_s[hh] = kb.astype(BF16)
        kbe_s[hh] = kb * eg
        vb_s[hh] = _silu(_causal_conv(vp_ref[0, :, cols], cwv_ref[:, cols], DN_CONV)) * beta

    row = lax.broadcasted_iota(jnp.int32, (c, c), 0)
    col = lax.broadcasted_iota(jnp.int32, (c, c), 1)
    masks = (row >= col, row > col, (row >> 3) == (col >> 3), (row >> 4) == (col >> 4), (row == col).astype(F32))
    scr = (q_s, k_s, kb_s, qd_s, kf_s, kbe_s, vb_s, gc_s, gcd_s)
    nw = nw_ref[...]
    rows_per_group = DN_GROUP * c
    n_groups = s // rows_per_group

    def group_base(gi):
        base = gi * rows_per_group
        return base if isinstance(base, int) else pl.multiple_of(base, rows_per_group)

    def run(gi_local, gi_seq, states, locs):
        nxt = []
        stages = iter(()) if gi_local is None else _dn_group_local(group_base(gi_local), scr, masks, nxt)
        todo = list(range(DN_GROUP)) if gi_seq is not None else []
        states = list(states)
        done = False
        while todo or not done:
            if not done:
                done = next(stages, "end") == "end"
            if todo:
                cc = todo.pop(0)
                rows = pl.ds(group_base(gi_seq) + cc * c, c)
                for hh in range(hb):
                    cols = slice(hh * HEAD_DIM, (hh + 1) * HEAD_DIM)
                    states[hh], o = _dn_chunk_seq(states[hh], locs[hh * DN_GROUP + cc], z_ref[0, rows, cols], nw)
                    o_ref[0, rows, cols] = o.astype(o_ref.dtype)
        return tuple(states), tuple(nxt)

    def body(gi, carry):
        return run(gi + 1, gi, *carry)

    carry = run(0, None, (jnp.zeros((HEAD_DIM, HEAD_DIM), F32),) * hb, None)
    carry = lax.fori_loop(0, n_groups - 1, body, carry)
    run(None, n_groups - 1, *carry)


def _deltanet(proj, qkv_col0, z_col0, conv_w, ba, a_log, dt_bias, norm_w, n_heads):
    bsz, s, _ = proj.shape
    qb0 = qkv_col0 // HEAD_DIM
    zb0 = z_col0 // HEAD_DIM
    arow = jnp.zeros((1, LANES), F32).at[0, n_heads:2 * n_heads].set(a_log)
    dtrow = jnp.zeros((1, LANES), F32).at[0, n_heads:2 * n_heads].set(dt_bias)
    cw = jnp.zeros((8, 3 * n_heads * HEAD_DIM), F32).at[:DN_CONV].set(conv_w)
    hb = DN_HEADS_PER_STEP
    assert n_heads % hb == 0 and qb0 % hb == 0 and zb0 % hb == 0
    blk = (1, s, hb * HEAD_DIM)
    col_spec = lambda off: pl.BlockSpec(blk, lambda b, h, off=off: (b, 0, h + off // hb))
    cw_spec = lambda off: pl.BlockSpec((8, hb * HEAD_DIM), lambda b, h, off=off: (0, h + off // hb))
    row_spec = pl.BlockSpec((1, LANES), lambda b, h: (0, 0))
    per_head = lambda dtype: pltpu.VMEM((hb, s, HEAD_DIM), dtype)
    return pl.pallas_call(
        functools.partial(_dn_kernel, n_heads=n_heads),
        grid=(bsz, n_heads // hb),
        in_specs=[col_spec(qb0), col_spec(qb0 + n_heads), col_spec(qb0 + 2 * n_heads), col_spec(zb0),
                  pl.BlockSpec((1, s, LANES), lambda b, h: (b, 0, 0)), row_spec, row_spec,
                  cw_spec(0), cw_spec(n_heads), cw_spec(2 * n_heads), row_spec],
        out_specs=pl.BlockSpec(blk, lambda b, h: (b, 0, h)),
        out_shape=jax.ShapeDtypeStruct((bsz, s, n_heads * HEAD_DIM), BF16),
        scratch_shapes=([per_head(BF16)] * 3 + [per_head(F32)] * 6 + [pltpu.VMEM((s, LANES), F32)]
                        + [pltpu.VMEM((hb, max(8, s // LANES), LANES), F32)]),
        compiler_params=_params("parallel", "arbitrary", vmem=VMEM_LIMIT_BIG),
        name="deltanet",
    )(proj, proj, proj, proj, ba, arow, dtrow, cw, cw, cw, norm_w.reshape(1, HEAD_DIM))


def _rope_tables(ang):
    lane = lax.broadcasted_iota(jnp.int32, ang.shape, 1)
    sin = jnp.sin(ang)
    return jnp.cos(ang), jnp.where(lane < HEAD_DIM // 2, -sin, sin)


def _rope(x, cos, sin_signed):
    return x * cos + pltpu.roll(x, HEAD_DIM // 2, axis=1) * sin_signed


def _rope_kernel(ang_ref, q_ref, ks_ref, kw_ref, vs_ref, vw_ref, qo_ref, kso_ref, kwo_ref, vso_ref, vwo_ref):
    cos, sin = _rope_tables(ang_ref[0])
    for hh in range(NSA_HEADS):
        sl = slice(hh * HEAD_DIM, (hh + 1) * HEAD_DIM)
        qo_ref[0, :, sl] = (_rope(q_ref[0, :, sl], cos, sin) * Q_SCALE).astype(BF16)
    for g in range(NSA_KV_HEADS):
        sl = slice(g * HEAD_DIM, (g + 1) * HEAD_DIM)
        kso_ref[0, :, sl] = _rope(ks_ref[0, :, sl], cos, sin).astype(BF16)
        kwo_ref[0, :, sl] = _rope(kw_ref[0, :, sl], cos, sin).astype(BF16)
    ts = vs_ref.shape[1]
    for v_ref, vo_ref in ((vs_ref, vso_ref), (vw_ref, vwo_ref)):
        ck = vo_ref.shape[4]
        for g in range(NSA_KV_HEADS):
            for cc in range(ts // ck):
                parts = [v_ref[0, cc * ck + r:cc * ck + r + LANES, g * HEAD_DIM:(g + 1) * HEAD_DIM].T
                         for r in range(0, ck, LANES)]
                vo_ref[0, g, cc] = jnp.concatenate(parts, axis=1).astype(BF16)


def _rope_qkv(proj, ang, *, ts, slc_chunk, win_chunk):
    bsz, s, _ = proj.shape
    qw = NSA_HEADS * HEAD_DIM
    kvw = NSA_KV_HEADS * HEAD_DIM
    kv_spec = lambda blk: pl.BlockSpec((1, ts, kvw), lambda b, i, blk=blk: (b, i, blk))
    kv_out = pl.BlockSpec((1, ts, kvw), lambda b, i: (b, i, 0))
    kv_shape = jax.ShapeDtypeStruct((bsz, s, kvw), BF16)
    vt_out = lambda ck: pl.BlockSpec((1, NSA_KV_HEADS, ts // ck, HEAD_DIM, ck), lambda b, i: (b, 0, i, 0, 0))
    vt_shape = lambda ck: jax.ShapeDtypeStruct((bsz, NSA_KV_HEADS, s // ck, HEAD_DIM, ck), BF16)
    base = qw // kvw
    return pl.pallas_call(
        _rope_kernel,
        grid=(bsz, s // ts),
        in_specs=[pl.BlockSpec((1, ts, HEAD_DIM), lambda b, i: (b, i, 0)),
                  pl.BlockSpec((1, ts, qw), lambda b, i: (b, i, 0)),
                  kv_spec(base + 2), kv_spec(base + 4), kv_spec(base + 3), kv_spec(base + 5)],
        out_specs=[pl.BlockSpec((1, ts, qw), lambda b, i: (b, i, 0)), kv_out, kv_out,
                   vt_out(slc_chunk), vt_out(win_chunk)],
        out_shape=[jax.ShapeDtypeStruct((bsz, s, qw), BF16), kv_shape, kv_shape,
                   vt_shape(slc_chunk), vt_shape(win_chunk)],
        compiler_params=_params("parallel", "parallel"),
        name="rope_qkv",
    )(ang, proj, proj, proj, proj, proj)


def _gelu_tanh(x):
    return x * (0.5 * (1.0 + jnp.tanh(math.sqrt(2.0 / math.pi) * (x + 0.044715 * (x * x * x)))))


def _compress_kernel(x_ref, w1_ref, w2_ref, pos_ref, ang_ref, o_ref, *, rope):
    nsub = x_ref.shape[1] // CMP_STRIDE
    hid = w1_ref.shape[1]
    pa = jnp.zeros((nsub, hid), F32)
    pb = jnp.zeros((nsub, hid), F32)
    for l in range(CMP_STRIDE):
        xl = x_ref[0, pl.ds(l, nsub, stride=CMP_STRIDE), :].astype(BF16)
        pa = pa + _dot(xl, w1_ref[l * HEAD_DIM:(l + 1) * HEAD_DIM, :])
        pb = pb + _dot(xl, w1_ref[(CMP_STRIDE + l) * HEAD_DIM:(CMP_STRIDE + l + 1) * HEAD_DIM, :])
    bias = _dot(pos_ref[...], w1_ref[...])[0:1, :]
    hpre = pa + pltpu.roll(pb, nsub - 1, axis=0) + bias
    out = _dot(_gelu_tanh(hpre).astype(BF16), w2_ref[...])
    if rope:
        cos, sin = _rope_tables(ang_ref[0])
        out = _rope(out, cos, sin)
    o_ref[0, 0] = out.astype(o_ref.dtype)


def _compress(proj, col0, pos_emb, w1, w2, ang_cmp, *, rope):
    bsz, s, _ = proj.shape
    nsub = s // CMP_STRIDE
    blk0 = col0 // HEAD_DIM
    hid = w1.shape[1]
    pos = jnp.zeros((8, CMP_BLOCK * HEAD_DIM), BF16).at[0].set(pos_emb.reshape(-1).astype(BF16))
    return pl.pallas_call(
        functools.partial(_compress_kernel, rope=rope),
        grid=(bsz, NSA_KV_HEADS),
        in_specs=[pl.BlockSpec((1, s, HEAD_DIM), lambda b, g: (b, 0, g + blk0)),
                  pl.BlockSpec((CMP_BLOCK * HEAD_DIM, hid), lambda b, g: (0, 0)),
                  pl.BlockSpec((hid, HEAD_DIM), lambda b, g: (0, 0)),
                  pl.BlockSpec((8, CMP_BLOCK * HEAD_DIM), lambda b, g: (0, 0)),
                  pl.BlockSpec((1, nsub, HEAD_DIM), lambda b, g: (b, 0, 0))],
        out_specs=pl.BlockSpec((1, 1, nsub, HEAD_DIM), lambda b, g: (b, g, 0, 0)),
        out_shape=jax.ShapeDtypeStruct((bsz, NSA_KV_HEADS, nsub, HEAD_DIM), BF16),
        compiler_params=_params("parallel", "parallel"),
        name="compress",
    )(proj, w1.astype(BF16), w2.astype(BF16), pos, ang_cmp)


def _cmp_attn_kernel(q_ref, kc_ref, vc_ref, smat_ref, o_ref, sel_ref):
    tq = q_ref.shape[1]
    ncol = kc_ref.shape[2]
    t = pl.program_id(2) * tq + lax.broadcasted_iota(jnp.int32, (tq, ncol), 0)
    n = lax.broadcasted_iota(jnp.int32, (tq, ncol), 1)
    valid = (n * CMP_STRIDE + CMP_BLOCK - 1) <= t
    kc = kc_ref[0, 0]
    vc = vc_ref[0, 0]
    p_grp = jnp.zeros((tq, ncol), F32)
    for hh in range(HPG):
        sl = slice(hh * HEAD_DIM, (hh + 1) * HEAD_DIM)
        sc = jnp.where(valid, _dot_nt(q_ref[0, :, sl], kc), NEG_INF)
        e = jnp.exp2(sc - jnp.max(sc, axis=-1, keepdims=True))
        p = jnp.where(valid, e / jnp.sum(e, axis=-1, keepdims=True), 0.0)
        o_ref[0, :, sl] = _dot(p.astype(BF16), vc)
        p_grp = p_grp + p
    score = _dot_hi(p_grp, smat_ref[...])
    n_sel = sel_ref.shape[2]
    score = jnp.concatenate([score[r:r + LANES].T for r in range(0, tq, LANES)], axis=1)[:n_sel]
    n = lax.broadcasted_iota(jnp.int32, (n_sel, tq), 0)
    t = pl.program_id(2) * tq + lax.broadcasted_iota(jnp.int32, (n_sel, tq), 1)
    cur = t >> int(math.log2(SEL_BLOCK))
    forced = (n == 0) | (n == cur) | (n == cur - 1)
    future = n * SEL_BLOCK > t
    score = jnp.where(forced, jnp.inf, jnp.where(future, -jnp.inf, score))
    rank = jnp.zeros((n_sel, tq), jnp.int32)
    for kk in range(n_sel):
        ck = score[kk:kk + 1, :]
        ahead = (ck > score) | ((ck == score) & (kk < n))
        rank = rank + ahead.astype(jnp.int32)
    sel_ref[0, 0] = (rank < N_SELECT).astype(sel_ref.dtype)


def _sel_matrix(ncol, n_sel):
    rs = SEL_BLOCK // CMP_STRIDE
    rc = CMP_BLOCK // CMP_STRIDE
    mat = [[0.0] * ncol for _ in range(ncol)]
    for j in range(n_sel):
        for m in range(rs):
            for n in range(rc):
                i = rs * j + m + n - (rc - 1)
                if 0 <= i < ncol - 1:
                    mat[i][j] += 1.0
    return jnp.array(mat, F32)


def _cmp_attention(q_r, k_cmp, v_cmp, *, tq):
    bsz, s, _ = q_r.shape
    ncol = k_cmp.shape[2]
    n_sel = s // SEL_BLOCK
    gw = HPG * HEAD_DIM
    return pl.pallas_call(
        _cmp_attn_kernel,
        grid=(bsz, NSA_KV_HEADS, s // tq),
        in_specs=[pl.BlockSpec((1, tq, gw), lambda b, g, i: (b, i, g)),
                  pl.BlockSpec((1, 1, ncol, HEAD_DIM), lambda b, g, i: (b, g, 0, 0)),
                  pl.BlockSpec((1, 1, ncol, HEAD_DIM), lambda b, g, i: (b, g, 0, 0)),
                  pl.BlockSpec((ncol, ncol), lambda b, g, i: (0, 0))],
        out_specs=[pl.BlockSpec((1, tq, gw), lambda b, g, i: (b, i, g)),
                   pl.BlockSpec((1, 1, n_sel, tq), lambda b, g, i: (b, g, 0, i))],
        out_shape=[jax.ShapeDtypeStruct((bsz, s, NSA_HEADS * HEAD_DIM), F32),
                   jax.ShapeDtypeStruct((bsz, NSA_KV_HEADS, n_sel, s), F32)],
        compiler_params=_params("parallel", "parallel", "parallel"),
        name="cmp_attention",
    )(q_r, k_cmp, v_cmp, _sel_matrix(ncol, n_sel))


def _softmax_merge_t(parts):
    if len(parts) == 1:
        return parts[0][0] / parts[0][2]
    m = functools.reduce(jnp.maximum, [p[1] for p in parts])
    scale = [jnp.exp2(p[1] - m) for p in parts]
    acc = functools.reduce(lambda x, y: x + y, [p[0] * s for p, s in zip(parts, scale)])
    l = functools.reduce(lambda x, y: x + y, [p[2] * s for p, s in zip(parts, scale)])
    return acc / l


def _softmax_partial_t(scores, values_t):
    r = scores[0].shape[1]
    m8 = None
    for sc in scores:
        c8 = jnp.max(sc.reshape(-1, 8, r), axis=0)
        m8 = c8 if m8 is None else jnp.maximum(m8, c8)
    m = jnp.max(m8, axis=0, keepdims=True)
    l8 = None
    probs = []
    for sc in scores:
        p = jnp.exp2(sc - m)
        p8 = jnp.sum(p.reshape(-1, 8, r), axis=0)
        l8 = p8 if l8 is None else l8 + p8
        probs.append(p.astype(BF16))
    l = jnp.sum(l8, axis=0, keepdims=True)
    acc = _dot(jnp.concatenate(values_t, axis=1), jnp.concatenate(probs, axis=0))
    return acc, m, l


MASK_BIG = 2.0 ** 100


def _slc_win_kernel(q_ref, ks_ref, vst_ref, kw_ref, vwt_ref, selt_ref, blk_ref, oc_ref, gate_ref, o_ref,
                    gt_s, os_s, ow_s):
    tq = vwt_ref.shape[4]
    tk = vst_ref.shape[4]
    n_sub = q_ref.shape[1] // tq
    n_chunks = vst_ref.shape[2]
    n_sel = selt_ref.shape[2]
    g = pl.program_id(1)
    t_blk = pl.program_id(2) * (n_sub * tq)

    def tile_heads(mask):
        return jnp.concatenate([mask] * HPG, axis=1)

    key_loc = lax.broadcasted_iota(jnp.int32, (tq, tq), 0)
    qry_loc = lax.broadcasted_iota(jnp.int32, (tq, tq), 1)
    k_loc = lax.broadcasted_iota(jnp.int32, (tk, tq), 0)
    t_loc = lax.broadcasted_iota(jnp.int32, (tk, tq), 1)
    n_win = WINDOW // tq + 1
    q_ts, q_augs = [], []
    for sub in range(n_sub):
        rows = slice(sub * tq, (sub + 1) * tq)
        q_t = jnp.concatenate([q_ref[0, rows, hh * HEAD_DIM:(hh + 1) * HEAD_DIM].astype(F32).T.astype(BF16)
                               for hh in range(HPG)], axis=1)
        sel_bias = ((selt_ref[0, 0, :, rows] - 1.0) * MASK_BIG).astype(BF16)
        q_ts.append(q_t)
        q_augs.append(jnp.concatenate([q_t, tile_heads(sel_bias),
                                       jnp.zeros((HEAD_DIM - n_sel, HPG * tq), BF16)], axis=0))

    def window_scores(in_range, sub):
        t0 = t_blk + sub * tq
        if in_range:
            k0 = pl.multiple_of(t0 - WINDOW, tq)
            sc = _dot(kw_ref[0, pl.ds(k0, n_win * tq), :], q_ts[sub])
            scores = [sc[cc * tq:(cc + 1) * tq] for cc in range(n_win)]
            scores[0] = jnp.where(tile_heads(key_loc > qry_loc), scores[0], NEG_INF)
            scores[-1] = jnp.where(tile_heads(key_loc <= qry_loc), scores[-1], NEG_INF)
            return scores, [vwt_ref[0, 0, k0 // tq + cc] for cc in range(n_win)]
        sc = _dot(kw_ref[0, 0:WINDOW, :], q_ts[sub])
        scores = [jnp.where(tile_heads(cc * tq + key_loc <= t0 + qry_loc), sc[cc * tq:(cc + 1) * tq], NEG_INF)
                  for cc in range(WINDOW // tq)]
        return scores, [vwt_ref[0, 0, cc] for cc in range(WINDOW // tq)]

    def block_variant(n_used):
        n_groups = min(n_used, 2)
        bounds = [(n_used * gg) // n_groups for gg in range(n_groups + 1)]
        sel_groups, win = [], []
        for sub in range(n_sub):
            groups = []
            for lo, hi in zip(bounds[:-1], bounds[1:]):
                rows = slice(lo * tk, hi * tk)
                sc = _dot(jnp.concatenate([ks_ref[0, rows, :], blk_ref[rows, :]], axis=1), q_augs[sub])
                groups.append([sc[cc * tk:(cc + 1) * tk] for cc in range(hi - lo)])
            causal = ((n_used - 1) * tk + k_loc) <= (t_blk + sub * tq + t_loc)
            groups[-1][-1] = jnp.where(tile_heads(causal), groups[-1][-1], NEG_INF)
            sel_groups.append(groups)
            win.append(window_scores((n_used - 1) * tk >= WINDOW, sub))
        values = [[vst_ref[0, 0, cc] for cc in range(bounds[gg], bounds[gg + 1])] for gg in range(n_groups)]
        parts = [[_softmax_partial_t(sel_groups[sub][0], values[0])] for sub in range(n_sub)]
        for sub in range(n_sub):
            ow_s[sub] = _softmax_merge_t([_softmax_partial_t(*win[sub])])
        for gg in range(1, n_groups):
            for sub in range(n_sub):
                parts[sub].append(_softmax_partial_t(sel_groups[sub][gg], values[gg]))
        for sub in range(n_sub):
            os_s[sub] = _softmax_merge_t(parts[sub])

    for vv in range(n_chunks):
        pl.when(t_blk // tk == vv)(functools.partial(block_variant, vv + 1))

    for sub in range(n_sub):
        rows = slice(sub * tq, (sub + 1) * tq)
        gt = _sigmoid(gate_ref[0, rows, :])
        gt_s[sub] = gt.T
        lane = lax.broadcasted_iota(jnp.int32, gt.shape, 1)
        for hh in range(HPG):
            base = (g * HPG + hh) * 3
            g_cmp = jnp.sum(jnp.where(lane == base, gt, 0.0), axis=1, keepdims=True)
            cs = slice(hh * tq, (hh + 1) * tq)
            mix_t = (gt_s[sub, pl.ds(base + 1, 1), :] * os_s[sub, :, cs]
                     + gt_s[sub, pl.ds(base + 2, 1), :] * ow_s[sub, :, cs])
            sl = slice(hh * HEAD_DIM, (hh + 1) * HEAD_DIM)
            o_ref[0, rows, sl] = (g_cmp * oc_ref[0, rows, sl] + mix_t.T).astype(o_ref.dtype)


def _slc_win_attention(q_r, ks, vs_t, kw, vw_t, sel_t, o_cmp, gates):
    bsz, s, _ = q_r.shape
    n_sel = s // SEL_BLOCK
    gw = HPG * HEAD_DIM
    tb = vs_t.shape[4]
    tq = vw_t.shape[4]
    assert tq == LANES and tb % tq == 0 and WINDOW % tb == 0
    n_sub = tb // tq
    kv_spec = pl.BlockSpec((1, s, HEAD_DIM), lambda b, g, i: (b, 0, g))
    vt_spec = lambda a: pl.BlockSpec((1, 1) + a.shape[2:], lambda b, g, i: (b, g, 0, 0, 0))
    block_onehot = (jnp.arange(s)[:, None] // SEL_BLOCK == jnp.arange(HEAD_DIM)[None, :]).astype(BF16)
    return pl.pallas_call(
        _slc_win_kernel,
        grid=(bsz, NSA_KV_HEADS, s // tb),
        in_specs=[pl.BlockSpec((1, tb, gw), lambda b, g, i: (b, i, g)),
                  kv_spec, vt_spec(vs_t), kv_spec, vt_spec(vw_t),
                  pl.BlockSpec((1, 1, n_sel, tb), lambda b, g, i: (b, g, 0, i)),
                  pl.BlockSpec((s, HEAD_DIM), lambda b, g, i: (0, 0)),
                  pl.BlockSpec((1, tb, gw), lambda b, g, i: (b, i, g)),
                  pl.BlockSpec((1, tb, LANES), lambda b, g, i: (b, i, 0))],
        out_specs=pl.BlockSpec((1, tb, gw), lambda b, g, i: (b, i, g)),
        out_shape=jax.ShapeDtypeStruct((bsz, s, NSA_HEADS * HEAD_DIM), BF16),
        scratch_shapes=[pltpu.VMEM((n_sub, LANES, tq), F32), pltpu.VMEM((n_sub, HEAD_DIM, HPG * tq), F32),
                        pltpu.VMEM((n_sub, HEAD_DIM, HPG * tq), F32)],
        compiler_params=_params("parallel", "parallel", "arbitrary"),
        name="slc_win_attention",
    )(q_r, ks, vs_t, kw, vw_t, sel_t, block_onehot, o_cmp, gates)


def _pad_cols(w, n):
    return jnp.pad(w, ((0, 0), (0, n - w.shape[1])))


def _conv_deltanet_mixer(xb, bsz, s, w_in, sc_conv_w, dn_conv_w, a_log, dt_bias, norm_w, w_out):
    sc_w = sc_conv_w.shape[1]
    dn_w = dn_conv_w.shape[1] // 3
    n_heads = dn_w // HEAD_DIM
    main = 3 * sc_w + 4 * dn_w
    proj, ba = _in_proj(xb, w_in.astype(BF16), main, _pad_cols(w_in[:, main:], LANES).astype(BF16),
                        tm=1024, tn=1024)
    proj = proj.reshape(bsz, s, main)
    y_sc = _short_conv(proj, sc_conv_w, sc_w, tc=256)
    y_dn = _deltanet(proj, 3 * sc_w, 3 * sc_w + 3 * dn_w, dn_conv_w, ba.reshape(bsz, s, LANES), a_log, dt_bias,
                     norm_w, n_heads)
    wo = w_out.astype(BF16)
    return [y_sc.reshape(bsz * s, sc_w), y_dn.reshape(bsz * s, dn_w)], [wo[:sc_w], wo[sc_w:]]


def _nsa_mixer(xb, bsz, s, positions, w_in, cmp_pos_k, cmp_w1_k, cmp_w2_k, cmp_pos_v, cmp_w1_v, cmp_w2_v, w_out):
    qw = NSA_HEADS * HEAD_DIM
    kvw = NSA_KV_HEADS * HEAD_DIM
    main = qw + 6 * kvw
    proj, gates = _in_proj(xb, w_in.astype(BF16), main, _pad_cols(w_in[:, main:], LANES).astype(BF16),
                           tm=1024, tn=1024)
    proj = proj.reshape(bsz, s, main)
    half = HEAD_DIM // 2
    inv = jnp.power(ROPE_THETA, -jnp.arange(half, dtype=F32) / half)
    inv = jnp.concatenate([inv, inv])
    ang = positions.astype(F32)[..., None] * inv
    cmp_end = jnp.minimum(jnp.arange(s // CMP_STRIDE) * CMP_STRIDE + CMP_BLOCK - 1, s - 1)
    ang_cmp = positions[:, cmp_end].astype(F32)[..., None] * inv
    q_r, ks, kw, vs_t, vw_t = _rope_qkv(proj, ang, ts=512, slc_chunk=256, win_chunk=LANES)
    k_cmp = _compress(proj, qw, cmp_pos_k, cmp_w1_k, cmp_w2_k, ang_cmp, rope=True)
    v_cmp = _compress(proj, qw + kvw, cmp_pos_v, cmp_w1_v, cmp_w2_v, ang_cmp, rope=False)
    o_cmp, sel_t = _cmp_attention(q_r, k_cmp, v_cmp, tq=s)
    o = _slc_win_attention(q_r, ks, vs_t, kw, vw_t, sel_t, o_cmp, gates.reshape(bsz, s, LANES))
    return [o.reshape(bsz * s, qw)], [w_out.astype(BF16)]


def kernel(x, positions, ln_mix_g, ln_mix_b, ln_ffn_g, ln_ffn_b, ffn_w_in, ffn_w_out, hy_w_in, sc_conv_w, dn_conv_w, dn_a_log, dn_dt_bias, dn_norm_w, hy_w_out, nsa_w_in, cmp_pos_k, cmp_w1_k, cmp_w2_k, cmp_pos_v, cmp_w1_v, cmp_w2_v, nsa_w_out):
    bsz, s, d = x.shape
    xf = x.reshape(bsz * s, d)
    xb = xf
    ffn_w_out_b = ffn_w_out.astype(BF16)
    for i in range(DEPTH):
        j = i // 2
        if i % 2 == 0:
            ys, wos = _conv_deltanet_mixer(xb, bsz, s, hy_w_in[j], sc_conv_w[j], dn_conv_w[j], dn_a_log[j],
                                           dn_dt_bias[j], dn_norm_w[j], hy_w_out[j])
        else:
            ys, wos = _nsa_mixer(xb, bsz, s, positions, nsa_w_in[j], cmp_pos_k[j], cmp_w1_k[j], cmp_w2_k[j],
                                 cmp_pos_v[j], cmp_w1_v[j], cmp_w2_v[j], nsa_w_out[j])
        xf, xb = _matmul_ln(ys, wos, xf, ln_mix_g[i], ln_mix_b[i], tm=512, tn=d)
        hmid = _ffn_in(xb, ffn_w_in, i, tm=1024, tn=512)
        xf, xb = _matmul_ln([hmid], [ffn_w_out_b], xf, ln_ffn_g[i], ln_ffn_b[i], tm=512, tn=512, layer=i)
    return xf.reshape(bsz, s, d)
```

```python
import functools
import math

import jax
import jax.numpy as jnp
from jax import lax
from jax.experimental import pallas as pl
from jax.experimental.pallas import tpu as pltpu

F32 = jnp.float32
BF16 = jnp.bfloat16
HIGHEST = lax.Precision.HIGHEST

LANES = 128
VMEM_LIMIT = 48 * 1024 * 1024
VMEM_LIMIT_BIG = 56 * 1024 * 1024

DN_HEADS = 8
DN_CHUNK = 64
DN_CONV = 4
SC_KERNEL = 3
NSA_HEADS = 16
NSA_KV_HEADS = 4
HPG = NSA_HEADS // NSA_KV_HEADS
HEAD_DIM = 128
CMP_BLOCK = 32
CMP_STRIDE = 16
SEL_BLOCK = 64
N_SELECT = 16
WINDOW = 512
ROPE_THETA = 10000.0
LN_EPS = 1e-5
NORM_EPS = 1e-6
NEG_INF = -1e30
DEPTH = 2
ALPHA = (2 * DEPTH) ** 0.25
ATTN_SCALE = HEAD_DIM ** -0.5
Q_SCALE = ATTN_SCALE * math.log2(math.e)


def _params(*sem, vmem=VMEM_LIMIT):
    return pltpu.CompilerParams(dimension_semantics=sem, vmem_limit_bytes=vmem)


def _sigmoid(x):
    return 1.0 / (1.0 + jnp.exp(-x))


def _silu(x):
    return x * _sigmoid(x)


def _dot(a, b):
    return jnp.dot(a, b, preferred_element_type=F32)


def _dot_nt(a, b):
    return lax.dot_general(a, b, (((1,), (1,)), ((), ())), preferred_element_type=F32)


def _dot_hi(a, b):
    return jnp.dot(a, b, precision=HIGHEST, preferred_element_type=F32)


def _proj_kernel(x_ref, w_ref, ws_ref, o_ref, os_ref, *xb_scratch):
    if xb_scratch:
        @pl.when(pl.program_id(1) == 0)
        def _():
            xb_scratch[0][...] = x_ref[...].astype(BF16)

        xb = xb_scratch[0][...]
    else:
        xb = x_ref[...]
    o_ref[...] = _dot(xb, w_ref[...])

    @pl.when(pl.program_id(1) == 0)
    def _():
        os_ref[...] = _dot(xb, ws_ref[...])


def _in_proj(x, w, n, w_side, *, tm, tn):
    m, k = x.shape
    ns = w_side.shape[1]
    return pl.pallas_call(
        _proj_kernel,
        grid=(m // tm, n // tn),
        in_specs=[pl.BlockSpec((tm, k), lambda i, j: (i, 0)),
                  pl.BlockSpec((k, tn), lambda i, j: (0, j)),
                  pl.BlockSpec((k, ns), lambda i, j: (0, 0))],
        out_specs=[pl.BlockSpec((tm, tn), lambda i, j: (i, j)),
                   pl.BlockSpec((tm, ns), lambda i, j: (i, 0))],
        out_shape=[jax.ShapeDtypeStruct((m, n), F32), jax.ShapeDtypeStruct((m, ns), F32)],
        scratch_shapes=[] if x.dtype == BF16 else [pltpu.VMEM((tm, k), BF16)],
        compiler_params=_params("parallel", "arbitrary"),
        name="in_proj",
    )(x, w, w_side)


def _ffn_in_kernel(x_ref, wg_ref, wu_ref, o_ref, wgb_s, wub_s):
    @pl.when(pl.program_id(1) == 0)
    def _():
        wgb_s[...] = wg_ref[...].astype(BF16)
        wub_s[...] = wu_ref[...].astype(BF16)

    x = x_ref[...]
    gate = _dot(x, wgb_s[...])
    up = _dot(x, wub_s[...])
    o_ref[...] = (_silu(gate) * up).astype(o_ref.dtype)


def _ffn_in(xb, w_in, layer, *, tm, tn):
    m, k = xb.shape
    hidden = w_in.shape[2] // 2
    nj = hidden // tn
    return pl.pallas_call(
        _ffn_in_kernel,
        grid=(nj, m // tm),
        in_specs=[pl.BlockSpec((tm, k), lambda j, i: (i, 0)),
                  pl.BlockSpec((None, k, tn), lambda j, i: (layer, 0, j)),
                  pl.BlockSpec((None, k, tn), lambda j, i: (layer, 0, j + nj))],
        out_specs=pl.BlockSpec((tm, tn), lambda j, i: (i, j)),
        out_shape=jax.ShapeDtypeStruct((m, hidden), BF16),
        scratch_shapes=[pltpu.VMEM((k, tn), BF16), pltpu.VMEM((k, tn), BF16)],
        compiler_params=_params("parallel", "arbitrary"),
        name="ffn_in",
    )(xb, w_in, w_in)


MM_LN_ROWS = 128


def _mm_ln_kernel(*refs, n_pairs, nj):
    xs = refs[:n_pairs]
    ws = refs[n_pairs:2 * n_pairs]
    r_ref, g_ref, b_ref, o_ref, ob_ref, y_s = refs[2 * n_pairs:2 * n_pairs + 6]
    tm = o_ref.shape[0]
    j = pl.program_id(1)
    part = _dot(xs[0][...], ws[0][...])
    for x_ref, w_ref in zip(xs[1:], ws[1:]):
        part = part + _dot(x_ref[...], w_ref[...])
    y_s[j] = part

    @pl.when(j == nj - 1)
    def _():
        for r0 in range(0, tm, MM_LN_ROWS):
            rows = slice(r0, r0 + MM_LN_ROWS)
            y = jnp.concatenate([y_s[jj, rows, :] for jj in range(nj)], axis=1)
            v = ALPHA * r_ref[rows, :] + y
            mu = jnp.mean(v, axis=-1, keepdims=True)
            dv = v - mu
            var = jnp.mean(dv * dv, axis=-1, keepdims=True)
            out = dv * lax.rsqrt(var + LN_EPS) * g_ref[...] + b_ref[...]
            o_ref[rows, :] = out
            ob_ref[rows, :] = out.astype(BF16)


def _matmul_ln(xs, ws, resid, g, b, *, tm, tn, layer=None):
    m, d = resid.shape
    n_pairs = len(xs)
    nj = d // tn
    ks = [x.shape[1] for x in xs]
    if layer is None:
        w_specs = [pl.BlockSpec((k, tn), lambda i, j: (0, j)) for k in ks]
    else:
        w_specs = [pl.BlockSpec((None, k, tn), lambda i, j: (layer, 0, j)) for k in ks]
    in_specs = ([pl.BlockSpec((tm, k), lambda i, j: (i, 0)) for k in ks]
                + w_specs
                + [pl.BlockSpec((tm, d), lambda i, j: (i, 0)),
                   pl.BlockSpec((1, d), lambda i, j: (0, 0)),
                   pl.BlockSpec((1, d), lambda i, j: (0, 0))])
    return pl.pallas_call(
        functools.partial(_mm_ln_kernel, n_pairs=n_pairs, nj=nj),
        grid=(m // tm, nj),
        in_specs=in_specs,
        out_specs=[pl.BlockSpec((tm, d), lambda i, j: (i, 0)),
                   pl.BlockSpec((tm, d), lambda i, j: (i, 0))],
        out_shape=[jax.ShapeDtypeStruct((m, d), F32), jax.ShapeDtypeStruct((m, d), BF16)],
        scratch_shapes=[pltpu.VMEM((nj, tm, tn), F32)],
        compiler_params=_params("parallel", "arbitrary", vmem=VMEM_LIMIT_BIG),
        name="matmul_ln",
    )(*xs, *ws, resid, g.reshape(1, d), b.reshape(1, d))


def _causal_conv(u, w_ref, taps):
    def tap_sum(x, shift):
        acc = x * w_ref[taps - 1:taps, :]
        for sh in range(1, taps):
            acc = acc + shift(x, sh) * w_ref[taps - 1 - sh:taps - sh, :]
        return acc

    body = tap_sum(u, lambda x, sh: pltpu.roll(x, sh, axis=0))
    row = lax.broadcasted_iota(jnp.int32, (8, u.shape[1]), 0)
    head = tap_sum(u[0:8], lambda x, sh: jnp.where(row >= sh, pltpu.roll(x, sh, axis=0), 0.0))
    return jnp.concatenate([head, body[8:]], axis=0)


def _sc_kernel(b_ref, c_ref, h_ref, w_ref, o_ref):
    u = c_ref[0] * h_ref[0]
    o_ref[0] = (b_ref[0] * _causal_conv(u, w_ref, SC_KERNEL)).astype(o_ref.dtype)


def _short_conv(proj, conv_w, width, *, tc):
    bsz, s, _ = proj.shape
    nb = width // tc
    w = jnp.zeros((8, width), F32).at[:SC_KERNEL].set(conv_w)
    return pl.pallas_call(
        _sc_kernel,
        grid=(bsz, nb),
        in_specs=[pl.BlockSpec((1, s, tc), lambda b, j: (b, 0, j)),
                  pl.BlockSpec((1, s, tc), lambda b, j: (b, 0, j + nb)),
                  pl.BlockSpec((1, s, tc), lambda b, j: (b, 0, j + 2 * nb)),
                  pl.BlockSpec((8, tc), lambda b, j: (0, j))],
        out_specs=pl.BlockSpec((1, s, tc), lambda b, j: (b, 0, j)),
        out_shape=jax.ShapeDtypeStruct((bsz, s, width), BF16),
        compiler_params=_params("parallel", "parallel"),
        name="short_conv",
    )(proj, proj, proj, w)


def _split(x):
    hi = x.astype(BF16)
    return hi, (x - hi.astype(F32)).astype(BF16)


def _dotb(a, b):
    return _dot(a.astype(BF16), b.astype(BF16))


DN_GROUP = 16
DN_HEADS_PER_STEP = 2


def _dn_group_local(base, scr, masks, out):
    q_s, k_s, kb_s, qd_s, kf_s, kbe_s, vb_s, gc_s, gcd_s = scr
    incl, strict, m8, m16, eye = masks
    c = DN_CHUNK
    idx = range(DN_HEADS_PER_STEP * DN_GROUP)
    rows = [(e // DN_GROUP, pl.ds(base + (e % DN_GROUP) * c, c)) for e in idx]
    gc = [gc_s[hh, r, :] for hh, r in rows]
    decay = []
    for e in idx:
        cc = e % DN_GROUP
        gc_j = gcd_s[e // DN_GROUP, pl.ds(base // LANES + cc // 2, 1), (cc % 2) * c:(cc % 2) * c + c]
        decay.append(jnp.where(incl, jnp.exp(jnp.where(incl, gc[e][:, :c] - gc_j, 0.0)), 0.0))
    kbf = [k_s[hh, r, :] for hh, r in rows]
    kk = [_dot_nt(kb_s[rows[cc][0], rows[cc][1], :], kbf[cc]) for cc in idx]
    qk = [_dot_nt(q_s[rows[cc][0], rows[cc][1], :], kbf[cc]) for cc in idx]
    yield
    a = [jnp.where(strict, kk[cc] * decay[cc], 0.0) for cc in idx]
    intra = [(qk[cc] * decay[cc]).astype(BF16) for cc in idx]
    ad = [jnp.where(m8, x, 0.0) for x in a]
    adb = [x.astype(BF16) for x in ad]
    a2 = [_dot(x, x) for x in adb]
    yield
    a2b = [x.astype(BF16) for x in a2]
    p = [eye - x for x in ad]
    p1, a4 = [], []
    for cc in idx:
        p1.append(p[cc] + _dotb(p[cc], a2b[cc]))
        a4.append(_dot(a2b[cc], a2b[cc]))
    yield
    p2 = [p1[cc] + _dotb(p1[cc], a4[cc]) for cc in idx]
    yield
    pb = [x.astype(BF16) for x in p2]
    t = [_dotb(pb[cc], jnp.where(m16, a[cc] - ad[cc], 0.0)) for cc in idx]
    yield
    dinv = [(p2[cc] - _dotb(t[cc], pb[cc])).astype(BF16) for cc in idx]
    yield
    db, da = [], []
    for cc in idx:
        hh, r = rows[cc]
        rhs = jnp.concatenate([vb_s[hh, r, :], kbe_s[hh, r, :]], axis=1)
        db.append(_dotb(dinv[cc], rhs))
        da.append(_dotb(dinv[cc], jnp.where(m16, 0.0, a[cc])).astype(BF16))
    yield
    blocks = [[x[0:16]] for x in db]
    for s4 in range(1, c // 16):
        rs = slice(16 * s4, 16 * s4 + 16)
        for cc in idx:
            xprev = jnp.concatenate(blocks[cc] + [jnp.zeros((c - 16 * s4, 2 * HEAD_DIM), F32)], axis=0)
            blocks[cc].append(db[cc][rs] - _dotb(da[cc][rs], xprev))
        yield
    for cc in idx:
        sol = jnp.concatenate(blocks[cc], axis=0)
        sol_hi, sol_lo = _split(sol)
        hh, r = rows[cc]
        g_last = gc[cc][c - 1:c, :]
        k_dec_t = (kf_s[hh, r, :] * jnp.exp(g_last - gc[cc])).T.astype(BF16)
        kw = _dot(k_dec_t, sol_hi) + _dot(k_dec_t, sol_lo)
        iw = _dot(intra[cc], sol_hi) + _dot(intra[cc], sol_lo)
        out.append((kw[:, HEAD_DIM:].astype(BF16), kw[:, :HEAD_DIM],
                    (qd_s[hh, r, :] - iw[:, HEAD_DIM:]).astype(BF16), iw[:, :HEAD_DIM], jnp.exp(g_last)))
    yield


def _dn_chunk_seq(state, loc, z, nw):
    w2, n_mat, qp, op, eg_last = loc
    sb = state.astype(BF16)
    o = _dot(qp, sb) + op
    state = (state * eg_last - _dot(w2, sb)) + n_mat
    o = o * lax.rsqrt(jnp.mean(o * o, axis=-1, keepdims=True) + NORM_EPS) * nw * _silu(z)
    return state, o


def _dn_kernel(qp_ref, kp_ref, vp_ref, z_ref, ba_ref, arow_ref, dtrow_ref, cwq_ref, cwk_ref, cwv_ref, nw_ref, o_ref,
               q_s, k_s, kb_s, qd_s, kf_s, kbe_s, vb_s, gc_s, beta_s, gates_s, gcd_s, *, n_heads):
    hb = DN_HEADS_PER_STEP
    s = qp_ref.shape[1]
    c = DN_CHUNK
    @pl.when(pl.program_id(1) == 0)
    def _():
        ba = ba_ref[0]
        xa = ba + dtrow_ref[...]
        softplus = jnp.maximum(xa, 0.0) + jnp.log(1.0 + jnp.exp(-jnp.abs(xa)))
        lane0 = lax.broadcasted_iota(jnp.int32, ba.shape, 1)
        gates = jnp.where(lane0 < n_heads, _sigmoid(ba), -jnp.exp(arow_ref[...]) * softplus)
        for blk in range(s // LANES):
            rs = slice(blk * LANES, (blk + 1) * LANES)
            gates_s[rs, :] = gates[rs, :].T

    n_blk = s // LANES
    pos = lax.broadcasted_iota(jnp.int32, (n_blk, LANES), 1) & (c - 1)
    for hh in range(hb):
        h = pl.program_id(1) * hb + hh
        cols = slice(hh * HEAD_DIM, (hh + 1) * HEAD_DIM)
        beta_d = jnp.concatenate([gates_s[pl.ds(blk * LANES + h, 1), :] for blk in range(n_blk)], axis=0)
        gc_d = jnp.concatenate([gates_s[pl.ds(blk * LANES + h + n_heads, 1), :] for blk in range(n_blk)], axis=0)
        sh = 1
        while sh < c:
            gc_d = gc_d + jnp.where(pos >= sh, pltpu.roll(gc_d, sh, axis=1), 0.0)
            sh *= 2
        gcd_s[hh, 0:n_blk, :] = gc_d
        for blk in range(n_blk):
            rs = slice(blk * LANES, (blk + 1) * LANES)
            gc_s[hh, rs, :] = jnp.broadcast_to(gc_d[blk:blk + 1, :], (LANES, LANES)).T
            beta_s[hh, rs, :] = jnp.broadcast_to(beta_d[blk:blk + 1, :], (LANES, LANES)).T
        gc = gc_s[hh]
        beta = beta_s[hh]
        eg = jnp.exp(gc)
        q = _silu(_causal_conv(qp_ref[0, :, cols], cwq_ref[:, cols], DN_CONV))
        q = q * (lax.rsqrt(jnp.sum(q * q, axis=-1, keepdims=True) + NORM_EPS) * (HEAD_DIM ** -0.5))
        q_s[hh] = q.astype(BF16)
        qd_s[hh] = q * eg
        k = _silu(_causal_conv(kp_ref[0, :, cols], cwk_ref[:, cols], DN_CONV))
        k = k * lax.rsqrt(jnp.sum(k * k, axis=-1, keepdims=True) + NORM_EPS)
        kb = k * beta
        kf_s[hh] = k
        k_s[hh] = k.astype(BF16)
        kb_s[hh] = kb.astype(BF16)
        kbe_s[hh] = kb * eg
        vb_s[hh] = _silu(_causal_conv(vp_ref[0, :, cols], cwv_ref[:, cols], DN_CONV)) * beta

    row = lax.broadcasted_iota(jnp.int32, (c, c), 0)
    col = lax.broadcasted_iota(jnp.int32, (c, c), 1)
    masks = (row >= col, row > col, (row >> 3) == (col >> 3), (row >> 4) == (col >> 4), (row == col).astype(F32))
    scr = (q_s, k_s, kb_s, qd_s, kf_s, kbe_s, vb_s, gc_s, gcd_s)
    nw = nw_ref[...]
    rows_per_group = DN_GROUP * c
    n_groups = s // rows_per_group

    def group_base(gi):
        base = gi * rows_per_group
        return base if isinstance(base, int) else pl.multiple_of(base, rows_per_group)

    def run(gi_local, gi_seq, states, locs):
        nxt = []
        stages = iter(()) if gi_local is None else _dn_group_local(group_base(gi_local), scr, masks, nxt)
        todo = list(range(DN_GROUP)) if gi_seq is not None else []
        states = list(states)
        done = False
        while todo or not done:
            if not done:
                done = next(stages, "end") == "end"
            if todo:
                cc = todo.pop(0)
                rows = pl.ds(group_base(gi_seq) + cc * c, c)
                for hh in range(hb):
                    cols = slice(hh * HEAD_DIM, (hh + 1) * HEAD_DIM)
                    states[hh], o = _dn_chunk_seq(states[hh], locs[hh * DN_GROUP + cc], z_ref[0, rows, cols], nw)
                    o_ref[0, rows, cols] = o.astype(o_ref.dtype)
        return tuple(states), tuple(nxt)

    def body(gi, carry):
        return run(gi + 1, gi, *carry)

    carry = run(0, None, (jnp.zeros((HEAD_DIM, HEAD_DIM), F32),) * hb, None)
    carry = lax.fori_loop(0, n_groups - 1, body, carry)
    run(None, n_groups - 1, *carry)


def _deltanet(proj, qkv_col0, z_col0, conv_w, ba, a_log, dt_bias, norm_w, n_heads):
    bsz, s, _ = proj.shape
    qb0 = qkv_col0 // HEAD_DIM
    zb0 = z_col0 // HEAD_DIM
    arow = jnp.zeros((1, LANES), F32).at[0, n_heads:2 * n_heads].set(a_log)
    dtrow = jnp.zeros((1, LANES), F32).at[0, n_heads:2 * n_heads].set(dt_bias)
    cw = jnp.zeros((8, 3 * n_heads * HEAD_DIM), F32).at[:DN_CONV].set(conv_w)
    hb = DN_HEADS_PER_STEP
    assert n_heads % hb == 0 and qb0 % hb == 0 and zb0 % hb == 0
    blk = (1, s, hb * HEAD_DIM)
    col_spec = lambda off: pl.BlockSpec(blk, lambda b, h, off=off: (b, 0, h + off // hb))
    cw_spec = lambda off: pl.BlockSpec((8, hb * HEAD_DIM), lambda b, h, off=off: (0, h + off // hb))
    row_spec = pl.BlockSpec((1, LANES), lambda b, h: (0, 0))
    per_head = lambda dtype: pltpu.VMEM((hb, s, HEAD_DIM), dtype)
    return pl.pallas_call(
        functools.partial(_dn_kernel, n_heads=n_heads),
        grid=(bsz, n_heads // hb),
        in_specs=[col_spec(qb0), col_spec(qb0 + n_heads), col_spec(qb0 + 2 * n_heads), col_spec(zb0),
                  pl.BlockSpec((1, s, LANES), lambda b, h: (b, 0, 0)), row_spec, row_spec,
                  cw_spec(0), cw_spec(n_heads), cw_spec(2 * n_heads), row_spec],
        out_specs=pl.BlockSpec(blk, lambda b, h: (b, 0, h)),
        out_shape=jax.ShapeDtypeStruct((bsz, s, n_heads * HEAD_DIM), BF16),
        scratch_shapes=([per_head(BF16)] * 3 + [per_head(F32)] * 6 + [pltpu.VMEM((s, LANES), F32)]
                        + [pltpu.VMEM((hb, max(8, s // LANES), LANES), F32)]),
        compiler_params=_params("parallel", "arbitrary", vmem=VMEM_LIMIT_BIG),
        name="deltanet",
    )(proj, proj, proj, proj, ba, arow, dtrow, cw, cw, cw, norm_w.reshape(1, HEAD_DIM))


def _rope_tables(ang):
    lane = lax.broadcasted_iota(jnp.int32, ang.shape, 1)
    sin = jnp.sin(ang)
    return jnp.cos(ang), jnp.where(lane < HEAD_DIM // 2, -sin, sin)


def _rope(x, cos, sin_signed):
    return x * cos + pltpu.roll(x, HEAD_DIM // 2, axis=1) * sin_signed


def _rope_kernel(ang_ref, q_ref, ks_ref, kw_ref, vs_ref, vw_ref, qo_ref, kso_ref, kwo_ref, vso_ref, vwo_ref):
    cos, sin = _rope_tables(ang_ref[0])
    for hh in range(NSA_HEADS):
        sl = slice(hh * HEAD_DIM, (hh + 1) * HEAD_DIM)
        qo_ref[0, :, sl] = (_rope(q_ref[0, :, sl], cos, sin) * Q_SCALE).astype(BF16)
    for g in range(NSA_KV_HEADS):
        sl = slice(g * HEAD_DIM, (g + 1) * HEAD_DIM)
        kso_ref[0, :, sl] = _rope(ks_ref[0, :, sl], cos, sin).astype(BF16)
        kwo_ref[0, :, sl] = _rope(kw_ref[0, :, sl], cos, sin).astype(BF16)
    ts = vs_ref.shape[1]
    for v_ref, vo_ref in ((vs_ref, vso_ref), (vw_ref, vwo_ref)):
        ck = vo_ref.shape[4]
        for g in range(NSA_KV_HEADS):
            for cc in range(ts // ck):
                parts = [v_ref[0, cc * ck + r:cc * ck + r + LANES, g * HEAD_DIM:(g + 1) * HEAD_DIM].T
                         for r in range(0, ck, LANES)]
                vo_ref[0, g, cc] = jnp.concatenate(parts, axis=1).astype(BF16)


def _rope_qkv(proj, ang, *, ts, slc_chunk, win_chunk):
    bsz, s, _ = proj.shape
    qw = NSA_HEADS * HEAD_DIM
    kvw = NSA_KV_HEADS * HEAD_DIM
    kv_spec = lambda blk: pl.BlockSpec((1, ts, kvw), lambda b, i, blk=blk: (b, i, blk))
    kv_out = pl.BlockSpec((1, ts, kvw), lambda b, i: (b, i, 0))
    kv_shape = jax.ShapeDtypeStruct((bsz, s, kvw), BF16)
    vt_out = lambda ck: pl.BlockSpec((1, NSA_KV_HEADS, ts // ck, HEAD_DIM, ck), lambda b, i: (b, 0, i, 0, 0))
    vt_shape = lambda ck: jax.ShapeDtypeStruct((bsz, NSA_KV_HEADS, s // ck, HEAD_DIM, ck), BF16)
    base = qw // kvw
    return pl.pallas_call(
        _rope_kernel,
        grid=(bsz, s // ts),
        in_specs=[pl.BlockSpec((1, ts, HEAD_DIM), lambda b, i: (b, i, 0)),
                  pl.BlockSpec((1, ts, qw), lambda b, i: (b, i, 0)),
                  kv_spec(base + 2), kv_spec(base + 4), kv_spec(base + 3), kv_spec(base + 5)],
        out_specs=[pl.BlockSpec((1, ts, qw), lambda b, i: (b, i, 0)), kv_out, kv_out,
                   vt_out(slc_chunk), vt_out(win_chunk)],
        out_shape=[jax.ShapeDtypeStruct((bsz, s, qw), BF16), kv_shape, kv_shape,
                   vt_shape(slc_chunk), vt_shape(win_chunk)],
        compiler_params=_params("parallel", "parallel"),
        name="rope_qkv",
    )(ang, proj, proj, proj, proj, proj)


def _gelu_tanh(x):
    return x * (0.5 * (1.0 + jnp.tanh(math.sqrt(2.0 / math.pi) * (x + 0.044715 * (x * x * x)))))


def _compress_kernel(x_ref, w1_ref, w2_ref, pos_ref, ang_ref, o_ref, *, rope):
    nsub = x_ref.shape[1] // CMP_STRIDE
    hid = w1_ref.shape[1]
    pa = jnp.zeros((nsub, hid), F32)
    pb = jnp.zeros((nsub, hid), F32)
    for l in range(CMP_STRIDE):
        xl = x_ref[0, pl.ds(l, nsub, stride=CMP_STRIDE), :].astype(BF16)
        pa = pa + _dot(xl, w1_ref[l * HEAD_DIM:(l + 1) * HEAD_DIM, :])
        pb = pb + _dot(xl, w1_ref[(CMP_STRIDE + l) * HEAD_DIM:(CMP_STRIDE + l + 1) * HEAD_DIM, :])
    bias = _dot(pos_ref[...], w1_ref[...])[0:1, :]
    hpre = pa + pltpu.roll(pb, nsub - 1, axis=0) + bias
    out = _dot(_gelu_tanh(hpre).astype(BF16), w2_ref[...])
    if rope:
        cos, sin = _rope_tables(ang_ref[0])
        out = _rope(out, cos, sin)
    o_ref[0, 0] = out.astype(o_ref.dtype)


def _compress(proj, col0, pos_emb, w1, w2, ang_cmp, *, rope):
    bsz, s, _ = proj.shape
    nsub = s // CMP_STRIDE
    blk0 = col0 // HEAD_DIM
    hid = w1.shape[1]
    pos = jnp.zeros((8, CMP_BLOCK * HEAD_DIM), BF16).at[0].set(pos_emb.reshape(-1).astype(BF16))
    return pl.pallas_call(
        functools.partial(_compress_kernel, rope=rope),
        grid=(bsz, NSA_KV_HEADS),
        in_specs=[pl.BlockSpec((1, s, HEAD_DIM), lambda b, g: (b, 0, g + blk0)),
                  pl.BlockSpec((CMP_BLOCK * HEAD_DIM, hid), lambda b, g: (0, 0)),
                  pl.BlockSpec((hid, HEAD_DIM), lambda b, g: (0, 0)),
                  pl.BlockSpec((8, CMP_BLOCK * HEAD_DIM), lambda b, g: (0, 0)),
                  pl.BlockSpec((1, nsub, HEAD_DIM), lambda b, g: (b, 0, 0))],
        out_specs=pl.BlockSpec((1, 1, nsub, HEAD_DIM), lambda b, g: (b, g, 0, 0)),
        out_shape=jax.ShapeDtypeStruct((bsz, NSA_KV_HEADS, nsub, HEAD_DIM), BF16),
        compiler_params=_params("parallel", "parallel"),
        name="compress",
    )(proj, w1.astype(BF16), w2.astype(BF16), pos, ang_cmp)


def _cmp_attn_kernel(q_ref, kc_ref, vc_ref, smat_ref, o_ref, sel_ref):
    tq = q_ref.shape[1]
    ncol = kc_ref.shape[2]
    t = pl.program_id(2) * tq + lax.broadcasted_iota(jnp.int32, (tq, ncol), 0)
    n = lax.broadcasted_iota(jnp.int32, (tq, ncol), 1)
    valid = (n * CMP_STRIDE + CMP_BLOCK - 1) <= t
    kc = kc_ref[0, 0]
    vc = vc_ref[0, 0]
    p_grp = jnp.zeros((tq, ncol), F32)
    for hh in range(HPG):
        sl = slice(hh * HEAD_DIM, (hh + 1) * HEAD_DIM)
        sc = jnp.where(valid, _dot_nt(q_ref[0, :, sl], kc), NEG_INF)
        e = jnp.exp2(sc - jnp.max(sc, axis=-1, keepdims=True))
        p = jnp.where(valid, e / jnp.sum(e, axis=-1, keepdims=True), 0.0)
        o_ref[0, :, sl] = _dot(p.astype(BF16), vc)
        p_grp = p_grp + p
    score = _dot_hi(p_grp, smat_ref[...])
    n_sel = sel_ref.shape[2]
    score = jnp.concatenate([score[r:r + LANES].T for r in range(0, tq, LANES)], axis=1)[:n_sel]
    n = lax.broadcasted_iota(jnp.int32, (n_sel, tq), 0)
    t = pl.program_id(2) * tq + lax.broadcasted_iota(jnp.int32, (n_sel, tq), 1)
    cur = t >> int(math.log2(SEL_BLOCK))
    forced = (n == 0) | (n == cur) | (n == cur - 1)
    future = n * SEL_BLOCK > t
    score = jnp.where(forced, jnp.inf, jnp.where(future, -jnp.inf, score))
    rank = jnp.zeros((n_sel, tq), jnp.int32)
    for kk in range(n_sel):
        ck = score[kk:kk + 1, :]
        ahead = (ck > score) | ((ck == score) & (kk < n))
        rank = rank + ahead.astype(jnp.int32)
    sel_ref[0, 0] = (rank < N_SELECT).astype(sel_ref.dtype)


def _sel_matrix(ncol, n_sel):
    rs = SEL_BLOCK // CMP_STRIDE
    rc = CMP_BLOCK // CMP_STRIDE
    mat = [[0.0] * ncol for _ in range(ncol)]
    for j in range(n_sel):
        for m in range(rs):
            for n in range(rc):
                i = rs * j + m + n - (rc - 1)
                if 0 <= i < ncol - 1:
                    mat[i][j] += 1.0
    return jnp.array(mat, F32)


def _cmp_attention(q_r, k_cmp, v_cmp, *, tq):
    bsz, s, _ = q_r.shape
    ncol = k_cmp.shape[2]
    n_sel = s // SEL_BLOCK
    gw = HPG * HEAD_DIM
    return pl.pallas_call(
        _cmp_attn_kernel,
        grid=(bsz, NSA_KV_HEADS, s // tq),
        in_specs=[pl.BlockSpec((1, tq, gw), lambda b, g, i: (b, i, g)),
                  pl.BlockSpec((1, 1, ncol, HEAD_DIM), lambda b, g, i: (b, g, 0, 0)),
                  pl.BlockSpec((1, 1, ncol, HEAD_DIM), lambda b, g, i: (b, g, 0, 0)),
                  pl.BlockSpec((ncol, ncol), lambda b, g, i: (0, 0))],
        out_specs=[pl.BlockSpec((1, tq, gw), lambda b, g, i: (b, i, g)),
                   pl.BlockSpec((1, 1, n_sel, tq), lambda b, g, i: (b, g, 0, i))],
        out_shape=[jax.ShapeDtypeStruct((bsz, s, NSA_HEADS * HEAD_DIM), F32),
                   jax.ShapeDtypeStruct((bsz, NSA_KV_HEADS, n_sel, s), F32)],
        compiler_params=_params("parallel", "parallel", "parallel"),
        name="cmp_attention",
    )(q_r, k_cmp, v_cmp, _sel_matrix(ncol, n_sel))


def _softmax_merge_t(parts):
    if len(parts) == 1:
        return parts[0][0] / parts[0][2]
    m = functools.reduce(jnp.maximum, [p[1] for p in parts])
    scale = [jnp.exp2(p[1] - m) for p in parts]
    acc = functools.reduce(lambda x, y: x + y, [p[0] * s for p, s in zip(parts, scale)])
    l = functools.reduce(lambda x, y: x + y, [p[2] * s for p, s in zip(parts, scale)])
    return acc / l


def _softmax_partial_t(scores, values_t):
    r = scores[0].shape[1]
    m8 = None
    for sc in scores:
        c8 = jnp.max(sc.reshape(-1, 8, r), axis=0)
        m8 = c8 if m8 is None else jnp.maximum(m8, c8)
    m = jnp.max(m8, axis=0, keepdims=True)
    l8 = None
    probs = []
    for sc in scores:
        p = jnp.exp2(sc - m)
        p8 = jnp.sum(p.reshape(-1, 8, r), axis=0)
        l8 = p8 if l8 is None else l8 + p8
        probs.append(p.astype(BF16))
    l = jnp.sum(l8, axis=0, keepdims=True)
    acc = _dot(jnp.concatenate(values_t, axis=1), jnp.concatenate(probs, axis=0))
    return acc, m, l


MASK_BIG = 2.0 ** 100


def _slc_win_kernel(q_ref, ks_ref, vst_ref, kw_ref, vwt_ref, selt_ref, blk_ref, oc_ref, gate_ref, o_ref,
                    gt_s, os_s, ow_s):
    tq = vwt_ref.shape[4]
    tk = vst_ref.shape[4]
    n_sub = q_ref.shape[1] // tq
    n_chunks = vst_ref.shape[2]
    n_sel = selt_ref.shape[2]
    g = pl.program_id(1)
    t_blk = pl.program_id(2) * (n_sub * tq)

    def tile_heads(mask):
        return jnp.concatenate([mask] * HPG, axis=1)

    key_loc = lax.broadcasted_iota(jnp.int32, (tq, tq), 0)
    qry_loc = lax.broadcasted_iota(jnp.int32, (tq, tq), 1)
    k_loc = lax.broadcasted_iota(jnp.int32, (tk, tq), 0)
    t_loc = lax.broadcasted_iota(jnp.int32, (tk, tq), 1)
    n_win = WINDOW // tq + 1
    q_ts, q_augs = [], []
    for sub in range(n_sub):
        rows = slice(sub * tq, (sub + 1) * tq)
        q_t = jnp.concatenate([q_ref[0, rows, hh * HEAD_DIM:(hh + 1) * HEAD_DIM].astype(F32).T.astype(BF16)
                               for hh in range(HPG)], axis=1)
        sel_bias = ((selt_ref[0, 0, :, rows] - 1.0) * MASK_BIG).astype(BF16)
        q_ts.append(q_t)
        q_augs.append(jnp.concatenate([q_t, tile_heads(sel_bias),
                                       jnp.zeros((HEAD_DIM - n_sel, HPG * tq), BF16)], axis=0))

    def window_scores(in_range, sub):
        t0 = t_blk + sub * tq
        if in_range:
            k0 = pl.multiple_of(t0 - WINDOW, tq)
            sc = _dot(kw_ref[0, pl.ds(k0, n_win * tq), :], q_ts[sub])
            scores = [sc[cc * tq:(cc + 1) * tq] for cc in range(n_win)]
            scores[0] = jnp.where(tile_heads(key_loc > qry_loc), scores[0], NEG_INF)
            scores[-1] = jnp.where(tile_heads(key_loc <= qry_loc), scores[-1], NEG_INF)
            return scores, [vwt_ref[0, 0, k0 // tq + cc] for cc in range(n_win)]
        sc = _dot(kw_ref[0, 0:WINDOW, :], q_ts[sub])
        scores = [jnp.where(tile_heads(cc * tq + key_loc <= t0 + qry_loc), sc[cc * tq:(cc + 1) * tq], NEG_INF)
                  for cc in range(WINDOW // tq)]
        return scores, [vwt_ref[0, 0, cc] for cc in range(WINDOW // tq)]

    def block_variant(n_used):
        n_groups = min(n_used, 2)
        bounds = [(n_used * gg) // n_groups for gg in range(n_groups + 1)]
        sel_groups, win = [], []
        for sub in range(n_sub):
            groups = []
            for lo, hi in zip(bounds[:-1], bounds[1:]):
                rows = slice(lo * tk, hi * tk)
                sc = _dot(jnp.concatenate([ks_ref[0, rows, :], blk_ref[rows, :]], axis=1), q_augs[sub])
                groups.append([sc[cc * tk:(cc + 1) * tk] for cc in range(hi - lo)])
            causal = ((n_used - 1) * tk + k_loc) <= (t_blk + sub * tq + t_loc)
            groups[-1][-1] = jnp.where(tile_heads(causal), groups[-1][-1], NEG_INF)
            sel_groups.append(groups)
            win.append(window_scores((n_used - 1) * tk >= WINDOW, sub))
        values = [[vst_ref[0, 0, cc] for cc in range(bounds[gg], bounds[gg + 1])] for gg in range(n_groups)]
        parts = [[_softmax_partial_t(sel_groups[sub][0], values[0])] for sub in range(n_sub)]
        for sub in range(n_sub):
            ow_s[sub] = _softmax_merge_t([_softmax_partial_t(*win[sub])])
        for gg in range(1, n_groups):
            for sub in range(n_sub):
                parts[sub].append(_softmax_partial_t(sel_groups[sub][gg], values[gg]))
        for sub in range(n_sub):
            os_s[sub] = _softmax_merge_t(parts[sub])

    for vv in range(n_chunks):
        pl.when(t_blk // tk == vv)(functools.partial(block_variant, vv + 1))

    for sub in range(n_sub):
        rows = slice(sub * tq, (sub + 1) * tq)
        gt = _sigmoid(gate_ref[0, rows, :])
        gt_s[sub] = gt.T
        lane = lax.broadcasted_iota(jnp.int32, gt.shape, 1)
        for hh in range(HPG):
            base = (g * HPG + hh) * 3
            g_cmp = jnp.sum(jnp.where(lane == base, gt, 0.0), axis=1, keepdims=True)
            cs = slice(hh * tq, (hh + 1) * tq)
            mix_t = (gt_s[sub, pl.ds(base + 1, 1), :] * os_s[sub, :, cs]
                     + gt_s[sub, pl.ds(base + 2, 1), :] * ow_s[sub, :, cs])
            sl = slice(hh * HEAD_DIM, (hh + 1) * HEAD_DIM)
            o_ref[0, rows, sl] = (g_cmp * oc_ref[0, rows, sl] + mix_t.T).astype(o_ref.dtype)


def _slc_win_attention(q_r, ks, vs_t, kw, vw_t, sel_t, o_cmp, gates):
    bsz, s, _ = q_r.shape
    n_sel = s // SEL_BLOCK
    gw = HPG * HEAD_DIM
    tb = vs_t.shape[4]
    tq = vw_t.shape[4]
    assert tq == LANES and tb % tq == 0 and WINDOW % tb == 0
    n_sub = tb // tq
    kv_spec = pl.BlockSpec((1, s, HEAD_DIM), lambda b, g, i: (b, 0, g))
    vt_spec = lambda a: pl.BlockSpec((1, 1) + a.shape[2:], lambda b, g, i: (b, g, 0, 0, 0))
    block_onehot = (jnp.arange(s)[:, None] // SEL_BLOCK == jnp.arange(HEAD_DIM)[None, :]).astype(BF16)
    return pl.pallas_call(
        _slc_win_kernel,
        grid=(bsz, NSA_KV_HEADS, s // tb),
        in_specs=[pl.BlockSpec((1, tb, gw), lambda b, g, i: (b, i, g)),
                  kv_spec, vt_spec(vs_t), kv_spec, vt_spec(vw_t),
                  pl.BlockSpec((1, 1, n_sel, tb), lambda b, g, i: (b, g, 0, i)),
                  pl.BlockSpec((s, HEAD_DIM), lambda b, g, i: (0, 0)),
                  pl.BlockSpec((1, tb, gw), lambda b, g, i: (b, i, g)),
                  pl.BlockSpec((1, tb, LANES), lambda b, g, i: (b, i, 0))],
        out_specs=pl.BlockSpec((1, tb, gw), lambda b, g, i: (b, i, g)),
        out_shape=jax.ShapeDtypeStruct((bsz, s, NSA_HEADS * HEAD_DIM), BF16),
        scratch_shapes=[pltpu.VMEM((n_sub, LANES, tq), F32), pltpu.VMEM((n_sub, HEAD_DIM, HPG * tq), F32),
                        pltpu.VMEM((n_sub, HEAD_DIM, HPG * tq), F32)],
        compiler_params=_params("parallel", "parallel", "arbitrary"),
        name="slc_win_attention",
    )(q_r, ks, vs_t, kw, vw_t, sel_t, block_onehot, o_cmp, gates)


def _pad_cols(w, n):
    return jnp.pad(w, ((0, 0), (0, n - w.shape[1])))


def _conv_deltanet_mixer(xb, bsz, s, w_in, sc_conv_w, dn_conv_w, a_log, dt_bias, norm_w, w_out):
    sc_w = sc_conv_w.shape[1]
    dn_w = dn_conv_w.shape[1] // 3
    n_heads = dn_w // HEAD_DIM
    main = 3 * sc_w + 4 * dn_w
    proj, ba = _in_proj(xb, w_in.astype(BF16), main, _pad_cols(w_in[:, main:], LANES).astype(BF16),
                        tm=1024, tn=1024)
    proj = proj.reshape(bsz, s, main)
    y_sc = _short_conv(proj, sc_conv_w, sc_w, tc=256)
    y_dn = _deltanet(proj, 3 * sc_w, 3 * sc_w + 3 * dn_w, dn_conv_w, ba.reshape(bsz, s, LANES), a_log, dt_bias,
                     norm_w, n_heads)
    wo = w_out.astype(BF16)
    return [y_sc.reshape(bsz * s, sc_w), y_dn.reshape(bsz * s, dn_w)], [wo[:sc_w], wo[sc_w:]]


def _nsa_mixer(xb, bsz, s, positions, w_in, cmp_pos_k, cmp_w1_k, cmp_w2_k, cmp_pos_v, cmp_w1_v, cmp_w2_v, w_out):
    qw = NSA_HEADS * HEAD_DIM
    kvw = NSA_KV_HEADS * HEAD_DIM
    main = qw + 6 * kvw
    proj, gates = _in_proj(xb, w_in.astype(BF16), main, _pad_cols(w_in[:, main:], LANES).astype(BF16),
                           tm=1024, tn=1024)
    proj = proj.reshape(bsz, s, main)
    half = HEAD_DIM // 2
    inv = jnp.power(ROPE_THETA, -jnp.arange(half, dtype=F32) / half)
    inv = jnp.concatenate([inv, inv])
    ang = positions.astype(F32)[..., None] * inv
    cmp_end = jnp.minimum(jnp.arange(s // CMP_STRIDE) * CMP_STRIDE + CMP_BLOCK - 1, s - 1)
    ang_cmp = positions[:, cmp_end].astype(F32)[..., None] * inv
    q_r, ks, kw, vs_t, vw_t = _rope_qkv(proj, ang, ts=512, slc_chunk=256, win_chunk=LANES)
    k_cmp = _compress(proj, qw, cmp_pos_k, cmp_w1_k, cmp_w2_k, ang_cmp, rope=True)
    v_cmp = _compress(proj, qw + kvw, cmp_pos_v, cmp_w1_v, cmp_w2_v, ang_cmp, rope=False)
    o_cmp, sel_t = _cmp_attention(q_r, k_cmp, v_cmp, tq=s)
    o = _slc_win_attention(q_r, ks, vs_t, kw, vw_t, sel_t, o_cmp, gates.reshape(bsz, s, LANES))
    return [o.reshape(bsz * s, qw)], [w_out.astype(BF16)]


def kernel(x, positions, ln_mix_g, ln_mix_b, ln_ffn_g, ln_ffn_b, ffn_w_in, ffn_w_out, hy_w_in, sc_conv_w, dn_conv_w, dn_a_log, dn_dt_bias, dn_norm_w, hy_w_out, nsa_w_in, cmp_pos_k, cmp_w1_k, cmp_w2_k, cmp_pos_v, cmp_w1_v, cmp_w2_v, nsa_w_out):
    bsz, s, d = x.shape
    xf = x.reshape(bsz * s, d)
    xb = xf
    ffn_w_out_b = ffn_w_out.astype(BF16)
    for i in range(DEPTH):
        j = i // 2
        if i % 2 == 0:
            ys, wos = _conv_deltanet_mixer(xb, bsz, s, hy_w_in[j], sc_conv_w[j], dn_conv_w[j], dn_a_log[j],
                                           dn_dt_bias[j], dn_norm_w[j], hy_w_out[j])
        else:
            ys, wos = _nsa_mixer(xb, bsz, s, positions, nsa_w_in[j], cmp_pos_k[j], cmp_w1_k[j], cmp_w2_k[j],
                                 cmp_pos_v[j], cmp_w1_v[j], cmp_w2_v[j], nsa_w_out[j])
        xf, xb = _matmul_ln(ys, wos, xf, ln_mix_g[i], ln_mix_b[i], tm=512, tn=d)
        hmid = _ffn_in(xb, ffn_w_in, i, tm=1024, tn=512)
        xf, xb = _matmul_ln([hmid], [ffn_w_out_b], xf, ln_ffn_g[i], ln_ffn_b[i], tm=512, tn=512, layer=i)
    return xf.reshape(bsz, s, d)
```
